```python
import math
import jax, jax.numpy as jnp
from jax import lax
import numpy as np

D_MODEL = 1024
BATCH = 2
SEQ = 8192
DEPTH = 1

N_META = 16
MIX_WIDTH = D_MODEL
POOL_WIDTH = MIX_WIDTH // 2
POOL_WINDOWS = (2, 4, 8, 16)
POOL_GROUPS = len(POOL_WINDOWS)
POOL_GROUP_DIM = POOL_WIDTH // POOL_GROUPS
ATTN_WIDTH = MIX_WIDTH - POOL_WIDTH
N_HEADS = 4
V_DIM = ATTN_WIDTH // N_HEADS
QK_DIM = V_DIM // 2
QK_COLS = N_HEADS * 2 * QK_DIM
IN_COLS = POOL_WIDTH + 2 * QK_COLS + ATTN_WIDTH
D_FF = 2816
CONV_WIDTH = 3
Q_BLOCK = 128
EPS = 1e-6

kernel_name = "hymba_pool_diffattn_convglu"


def rmsnorm(x, g):
    xf = x.astype(jnp.float32)
    y = xf * lax.rsqrt(jnp.mean(xf * xf, axis=-1, keepdims=True) + EPS)
    return (y * g.astype(jnp.float32)).astype(x.dtype)


def causal_multiscale_pool(u):
    B, L, _ = u.shape
    ug = u.reshape(B, L, POOL_GROUPS, POOL_GROUP_DIM).astype(jnp.float32)
    cs = jnp.cumsum(ug, axis=1)
    t = jnp.arange(L)
    outs = []
    for g, w in enumerate(POOL_WINDOWS):
        c = cs[:, :, g]
        prev = jnp.pad(c, ((0, 0), (w, 0), (0, 0)))[:, :L]
        cnt = jnp.minimum(t + 1, w).astype(jnp.float32)[None, :, None]
        outs.append((c - prev) / cnt - ug[:, :, g])
    return jnp.stack(outs, axis=2)


def diff_attention(q, k, v, lam, lam_init, subln_g):
    B, L = q.shape[:2]
    Lp = -(-L // Q_BLOCK) * Q_BLOCK
    pad = Lp - L
    q = jnp.pad(q, ((0, 0), (0, pad), (0, 0), (0, 0), (0, 0)))
    k = jnp.pad(k, ((0, 0), (0, pad), (0, 0), (0, 0), (0, 0)))
    v = jnp.pad(v, ((0, 0), (0, pad), (0, 0), (0, 0)))
    kt = k.transpose(0, 2, 3, 1, 4)
    vt = v.transpose(0, 2, 1, 3)
    nblk = Lp // Q_BLOCK
    qb = q.reshape(B, nblk, Q_BLOCK, N_HEADS, 2, QK_DIM).transpose(1, 0, 3, 4, 2, 5)
    key_pos = jnp.arange(Lp)
    scale = QK_DIM ** -0.5

    def one_block(args):
        qblk, i = args
        s = jnp.einsum('bhcqd,bhckd->bhcqk', qblk, kt).astype(jnp.float32) * scale
        q_pos = i * Q_BLOCK + jnp.arange(Q_BLOCK)
        mask = key_pos[None, :] <= q_pos[:, None]
        s = jnp.where(mask, s, -jnp.inf)
        p = jax.nn.softmax(s, axis=-1)
        a = p[:, :, 0] - lam * p[:, :, 1]
        return jnp.einsum('bhqk,bhkd->bqhd', a.astype(vt.dtype), vt)

    o = lax.map(one_block, (qb, jnp.arange(nblk)))
    o = o.transpose(1, 0, 2, 3, 4).reshape(B, Lp, N_HEADS, V_DIM)[:, :L]
    o = rmsnorm(o, subln_g) * (1.0 - lam_init)
    return o.reshape(B, L, ATTN_WIDTH)


def conv_glu_ffn(h, w_up, conv_w, conv_b, w_down):
    u = h @ w_up
    L = u.shape[1]
    up = jnp.pad(u, ((0, 0), (CONV_WIDTH - 1, 0), (0, 0)))
    c = up[:, 0:L] * conv_w[0] + up[:, 1:L + 1] * conv_w[1] + up[:, 2:L + 2] * conv_w[2] + conv_b
    gate, val = jnp.split(c, 2, axis=-1)
    return (jax.nn.silu(gate) * val) @ w_down


def setup_inputs(seed: int = 0) -> dict:
    key = jax.random.key(seed)
    ks = jax.random.split(key, 20)
    f32 = jnp.float32
    nrm = lambda k, shape, s: (jax.random.normal(k, shape, f32) * s)
    Dp = DEPTH
    return {
        "x": nrm(ks[0], (BATCH, SEQ, D_MODEL), 1.0),
        "meta_tokens": nrm(ks[1], (N_META, D_MODEL), 1.0),
        "norm_mix_g": 1.0 + nrm(ks[2], (Dp, D_MODEL), 0.02),
        "w_in": nrm(ks[3], (Dp, D_MODEL, IN_COLS), D_MODEL ** -0.5),
        "w_pool": nrm(ks[4], (Dp, POOL_GROUPS, POOL_GROUP_DIM, POOL_GROUP_DIM), POOL_GROUP_DIM ** -0.5),
        "b_pool": nrm(ks[5], (Dp, POOL_GROUPS, POOL_GROUP_DIM), 0.02),
        "pool_scale": 1.0 + nrm(ks[6], (Dp, POOL_WIDTH), 0.05),
        "q_norm_g": 1.0 + nrm(ks[7], (Dp, QK_DIM), 0.02),
        "k_norm_g": 1.0 + nrm(ks[8], (Dp, QK_DIM), 0.02),
        "lambda_q1": nrm(ks[9], (Dp, QK_DIM), 0.1),
        "lambda_k1": nrm(ks[10], (Dp, QK_DIM), 0.1),
        "lambda_q2": nrm(ks[11], (Dp, QK_DIM), 0.1),
        "lambda_k2": nrm(ks[12], (Dp, QK_DIM), 0.1),
        "subln_g": 1.0 + nrm(ks[13], (Dp, V_DIM), 0.02),
        "w_out": nrm(ks[14], (Dp, MIX_WIDTH, D_MODEL), MIX_WIDTH ** -0.5),
        "norm_ffn_g": 1.0 + nrm(ks[15], (Dp, D_MODEL), 0.02),
        "w_up": nrm(ks[16], (Dp, D_MODEL, 2 * D_FF), D_MODEL ** -0.5),
        "conv_w": nrm(ks[17], (Dp, CONV_WIDTH, 2 * D_FF), 0.3) + jnp.array([0.0, 0.0, 1.0], f32)[None, :, None],
        "conv_b": nrm(ks[18], (Dp, 2 * D_FF), 0.02),
        "w_down": nrm(ks[19], (Dp, D_FF, D_MODEL), D_FF ** -0.5),
    }


def reference(x, meta_tokens, norm_mix_g, w_in, w_pool, b_pool, pool_scale, q_norm_g, k_norm_g,
              lambda_q1, lambda_k1, lambda_q2, lambda_k2, subln_g, w_out, norm_ffn_g,
              w_up, conv_w, conv_b, w_down):
    B = x.shape[0]
    meta = jnp.broadcast_to(meta_tokens[None].astype(x.dtype), (B, N_META, D_MODEL))
    h = jnp.concatenate([meta, x], axis=1)
    L = h.shape[1]
    for i in range(DEPTH):
        lam_init = 0.8 - 0.6 * math.exp(-0.3 * i)
        n = rmsnorm(h, norm_mix_g[i])
        proj = n @ w_in[i]
        u, q, k, v = jnp.split(proj, [POOL_WIDTH, POOL_WIDTH + QK_COLS, POOL_WIDTH + 2 * QK_COLS], axis=-1)
        pooled = causal_multiscale_pool(u)
        ya = jnp.einsum('blgc,gcd->blgd', pooled, w_pool[i].astype(jnp.float32)) + b_pool[i].astype(jnp.float32)
        ya = (ya.reshape(B, L, POOL_WIDTH) * pool_scale[i].astype(jnp.float32)).astype(h.dtype)
        q = rmsnorm(q.reshape(B, L, N_HEADS, 2, QK_DIM), q_norm_g[i])
        k = rmsnorm(k.reshape(B, L, N_HEADS, 2, QK_DIM), k_norm_g[i])
        v = v.reshape(B, L, N_HEADS, V_DIM)
        lam = (jnp.exp(jnp.sum(lambda_q1[i].astype(jnp.float32) * lambda_k1[i].astype(jnp.float32)))
               - jnp.exp(jnp.sum(lambda_q2[i].astype(jnp.float32) * lambda_k2[i].astype(jnp.float32)))
               + lam_init)
        yb = diff_attention(q, k, v, lam, lam_init, subln_g[i])
        mix = jnp.concatenate([ya, yb], axis=-1)
        h = h + mix @ w_out[i]
        h = h + conv_glu_ffn(rmsnorm(h, norm_ffn_g[i]), w_up[i], conv_w[i], conv_b[i], w_down[i])
    return h[:, N_META:]
```

```python
import functools
import math

import jax
import jax.numpy as jnp
from jax import lax
from jax.experimental import pallas as pl
from jax.experimental.pallas import tpu as pltpu

F32 = jnp.float32
BF16 = jnp.bfloat16

N_META = 16
POOL_WINDOWS = (2, 4, 8, 16)
N_HEADS = 4
HEAD_DIM = 128
QK_DIM = 64
CONV_WIDTH = 3
EPS = 1e-6

LANES = 128
HALO = 16
VMEM_LIMIT_BYTES = 56 * 1024 * 1024

SEQ_TILE = 512
ATTN_TILE = 512
FF_CHUNK = 256


def _rmsnorm(x, g):
    return x * lax.rsqrt(jnp.mean(x * x, axis=-1, keepdims=True) + EPS) * g


def _const_spec(shape):
    nd = len(shape)
    return pl.BlockSpec(shape, lambda *_: (0,) * nd, pipeline_mode=pl.Buffered(1))


def _mix_in_kernel(x_ref, meta_ref, g_ref, win_ref, wpool_ref, bpool_ref, pscale_ref,
                   qg_ref, kg_ref,
                   ya_ref, q_ref, k_ref, v_ref, yam_ref, qm_ref, km_ref, vm_ref,
                   uext_ref):
    pool_w = bpool_ref.shape[-1]
    qk_w = q_ref.shape[-1]

    def heads(rows, is_meta, ya_out, q_out, k_out, v_out):
        r = rows.shape[0]
        n = _rmsnorm(rows, g_ref[...])
        proj = jnp.dot(n.astype(BF16), win_ref[...], preferred_element_type=F32)
        u = proj[:, :pool_w]
        uext_ref[HALO:HALO + r, :] = u
        for gi, w in enumerate(POOL_WINDOWS):
            cols = slice(gi * LANES, (gi + 1) * LANES)
            win_sum = u[:, cols]
            for j in range(1, w):
                win_sum = win_sum + uext_ref[HALO - j:HALO - j + r, cols]
            if is_meta:
                pos = lax.broadcasted_iota(jnp.int32, (r, 1), 0)
                mean = win_sum / jnp.minimum(pos + 1, w).astype(F32)
            else:
                mean = win_sum * (1.0 / w)
            pooled = mean - u[:, cols]
            y = jnp.dot(pooled.astype(BF16), wpool_ref[gi], preferred_element_type=F32)
            y = (y + bpool_ref[:, cols]) * pscale_ref[:, cols]
            ya_out[0, :, cols] = y.astype(ya_out.dtype)
        uext_ref[0:HALO, :] = uext_ref[r:r + HALO, :]

        lane = lax.broadcasted_iota(jnp.int32, (r, LANES), 1)
        lo = lane < QK_DIM
        for src, gain_ref, out in ((proj[:, pool_w:pool_w + qk_w], qg_ref, q_out),
                                   (proj[:, pool_w + qk_w:pool_w + 2 * qk_w], kg_ref, k_out)):
            for h in range(N_HEADS):
                cols = slice(h * LANES, (h + 1) * LANES)
                slab = src[:, cols]
                sq = slab * slab
                s_lo = jnp.sum(jnp.where(lo, sq, 0.0), axis=-1, keepdims=True)
                s_hi = jnp.sum(jnp.where(lo, 0.0, sq), axis=-1, keepdims=True)
                ms = jnp.where(lo, s_lo, s_hi) * (1.0 / QK_DIM)
                out[0, :, cols] = (slab * lax.rsqrt(ms + EPS) * gain_ref[...]).astype(out.dtype)
        v_out[0] = proj[:, pool_w + 2 * qk_w:].astype(v_out.dtype)

    @pl.when(pl.program_id(1) == 0)
    def _():
        uext_ref[0:HALO, :] = jnp.zeros((HALO, pool_w), F32)
        heads(meta_ref[...], True, yam_ref, qm_ref, km_ref, vm_ref)

    heads(x_ref[0], False, ya_ref, q_ref, k_ref, v_ref)


def _mix_in(x, meta, g, w_in, w_pool, b_pool, pool_scale, qg, kg):
    b, seq, d = x.shape
    pool_w = b_pool.shape[-1]
    qk_w = N_HEADS * HEAD_DIM
    tm = SEQ_TILE
    assert seq % tm == 0
    tile = lambda w: pl.BlockSpec((1, tm, w), lambda bi, i: (bi, i, 0))
    mtile = lambda w: pl.BlockSpec((1, N_META, w), lambda bi, i: (bi, 0, 0))
    big = lambda w: jax.ShapeDtypeStruct((b, seq, w), BF16)
    small = lambda w: jax.ShapeDtypeStruct((b, N_META, w), BF16)
    return pl.pallas_call(
        _mix_in_kernel,
        grid=(b, seq // tm),
        in_specs=[tile(d), _const_spec(meta.shape), _const_spec(g.shape), _const_spec(w_in.shape),
                  _const_spec(w_pool.shape), _const_spec(b_pool.shape), _const_spec(pool_scale.shape),
                  _const_spec(qg.shape), _const_spec(kg.shape)],
        out_specs=[tile(pool_w), tile(qk_w), tile(qk_w), tile(qk_w),
                   mtile(pool_w), mtile(qk_w), mtile(qk_w), mtile(qk_w)],
        out_shape=[big(pool_w), big(qk_w), big(qk_w), big(qk_w),
                   small(pool_w), small(qk_w), small(qk_w), small(qk_w)],
        scratch_shapes=[pltpu.VMEM((HALO + tm, pool_w), F32)],
        compiler_params=pltpu.CompilerParams(
            dimension_semantics=("arbitrary", "arbitrary"), vmem_limit_bytes=VMEM_LIMIT_BYTES),
        name="mix_in",
    )(x, meta, g, w_in, w_pool, b_pool, pool_scale, qg, kg)


def _stack_subheads(q):
    lane = lax.broadcasted_iota(jnp.int32, q.shape, 1)
    zero = jnp.zeros_like(q)
    return jnp.concatenate([jnp.where(lane < QK_DIM, q, zero), jnp.where(lane < QK_DIM, zero, q)], axis=0)


def _online_softmax_step(qs, k, v, state, causal):
    m, l, acc = state
    s = lax.dot_general(qs, k, (((1,), (1,)), ((), ())), preferred_element_type=F32)
    if causal:
        r = qs.shape[0] // 2
        row = lax.broadcasted_iota(jnp.int32, s.shape, 0)
        row = jnp.where(row >= r, row - r, row)
        col = lax.broadcasted_iota(jnp.int32, s.shape, 1)
        s = jnp.where(col <= row, s, -jnp.inf)
    m_new = jnp.maximum(m, jnp.max(s, axis=-1, keepdims=True))
    alpha = jnp.exp(m - m_new)
    p = jnp.exp(s - m_new)
    l = alpha * l + jnp.sum(p, axis=-1, keepdims=True)
    acc = alpha * acc + jnp.dot(p.astype(v.dtype), v, preferred_element_type=F32)
    return m_new, l, acc


def _init_state(rows, width):
    return (jnp.full((rows, 1), -jnp.inf, F32), jnp.zeros((rows, 1), F32), jnp.zeros((rows, width), F32))


def _diff_attn_kernel(lam_init, q_ref, k_ref, v_ref, qm_ref, km_ref, vm_ref,
                      lq1_ref, lk1_ref, lq2_ref, lk2_ref, sg_ref, yb_ref, ybm_ref):
    i = pl.program_id(2)
    t = q_ref.shape[1]
    lam = (jnp.exp(jnp.sum(lq1_ref[...] * lk1_ref[...], axis=-1, keepdims=True))
           - jnp.exp(jnp.sum(lq2_ref[...] * lk2_ref[...], axis=-1, keepdims=True)) + lam_init)

    def finish(state, out_ref):
        _, l, acc = state
        r = acc.shape[0] // 2
        o = acc[:r] / l[:r] - lam * (acc[r:] / l[r:])
        out_ref[0] = (_rmsnorm(o, sg_ref[...]) * (1.0 - lam_init)).astype(out_ref.dtype)

    km = km_ref[0]
    vm = vm_ref[0]

    @pl.when(i == 0)
    def _():
        qs = _stack_subheads(qm_ref[0])
        finish(_online_softmax_step(qs, km, vm, _init_state(2 * N_META, HEAD_DIM), True), ybm_ref)

    qs = _stack_subheads(q_ref[0])
    state = _online_softmax_step(qs, km, vm, _init_state(2 * t, HEAD_DIM), False)

    def body(j, state):
        start = pl.multiple_of(j * t, t)
        return _online_softmax_step(qs, k_ref[0, pl.ds(start, t), :], v_ref[0, pl.ds(start, t), :],
                                    state, False)

    state = lax.fori_loop(0, i, body, state)
    start = pl.multiple_of(i * t, t)
    state = _online_softmax_step(qs, k_ref[0, pl.ds(start, t), :], v_ref[0, pl.ds(start, t), :],
                                 state, True)
    finish(state, yb_ref)


def _diff_attn(q, k, v, qm, km, vm, lq1, lk1, lq2, lk2, subln_g, lam_init):
    b, seq, width = q.shape
    t = ATTN_TILE
    assert seq % t == 0 and width == N_HEADS * HEAD_DIM
    qtile = pl.BlockSpec((1, t, HEAD_DIM), lambda bi, h, i: (bi, i, h))
    full = pl.BlockSpec((1, seq, HEAD_DIM), lambda bi, h, i: (bi, 0, h))
    mtile = pl.BlockSpec((1, N_META, HEAD_DIM), lambda bi, h, i: (bi, 0, h))
    return pl.pallas_call(
        functools.partial(_diff_attn_kernel, lam_init),
        grid=(b, N_HEADS, seq // t),
        in_specs=[qtile, full, full, mtile, mtile, mtile,
                  _const_spec(lq1.shape), _const_spec(lk1.shape), _const_spec(lq2.shape),
                  _const_spec(lk2.shape), _const_spec(subln_g.shape)],
        out_specs=[qtile, mtile],
        out_shape=[jax.ShapeDtypeStruct((b, seq, width), BF16),
                   jax.ShapeDtypeStruct((b, N_META, width), BF16)],
        compiler_params=pltpu.CompilerParams(
            dimension_semantics=("arbitrary", "arbitrary", "arbitrary"),
            vmem_limit_bytes=VMEM_LIMIT_BYTES),
        name="diff_attn",
    )(q, k, v, qm, km, vm, lq1, lk1, lq2, lk2, subln_g)


def _mix_out_kernel(x_ref, ya_ref, yb_ref, meta_ref, yam_ref, ybm_ref, wo_ref, g_ref,
                    wup_ref, cw_ref, cb_ref, wdown_ref, out_ref, next_ref, act_ref):
    tm = x_ref.shape[1]
    pool_w = ya_ref.shape[-1]
    d_ff = wdown_ref.shape[0]

    def mix_residual(rows, ya, yb):
        h1 = (rows + jnp.dot(ya, wo_ref[:pool_w, :], preferred_element_type=F32)
              + jnp.dot(yb, wo_ref[pool_w:, :], preferred_element_type=F32))
        return h1, _rmsnorm(h1, g_ref[...]).astype(BF16)

    @pl.when(pl.program_id(1) == 0)
    def _():
        _, nm = mix_residual(meta_ref[...], yam_ref[0], ybm_ref[0])
        next_ref[0:HALO, :] = nm

    h1, n2 = mix_residual(x_ref[0], ya_ref[0], yb_ref[0])
    next_ref[HALO:HALO + tm, :] = n2

    def conv(u, cols):
        out = cb_ref[:, cols]
        for tap in range(CONV_WIDTH):
            off = HALO - (CONV_WIDTH - 1) + tap
            out = out + u[off:off + tm, :] * cw_ref[tap:tap + 1, cols]
        return out

    for c in range(d_ff // FF_CHUNK):
        gcols = slice(c * FF_CHUNK, (c + 1) * FF_CHUNK)
        vcols = slice(d_ff + c * FF_CHUNK, d_ff + (c + 1) * FF_CHUNK)
        ug = jnp.dot(next_ref[...], wup_ref[:, gcols], preferred_element_type=F32)
        uv = jnp.dot(next_ref[...], wup_ref[:, vcols], preferred_element_type=F32)
        gate = conv(ug, gcols)
        act_ref[:, gcols] = (gate * jax.nn.sigmoid(gate) * conv(uv, vcols)).astype(BF16)
    out_ref[0] = h1 + jnp.dot(act_ref[...], wdown_ref[...], preferred_element_type=F32)
    next_ref[0:HALO, :] = next_ref[tm:tm + HALO, :]


def _mix_out(x, ya, yb, meta, yam, ybm, w_out, g, w_up, conv_w, conv_b, w_down):
    b, seq, d = x.shape
    pool_w = ya.shape[-1]
    tm = SEQ_TILE
    assert seq % tm == 0 and w_down.shape[0] % FF_CHUNK == 0
    tile = lambda w: pl.BlockSpec((1, tm, w), lambda bi, i: (bi, i, 0))
    mtile = lambda w: pl.BlockSpec((1, N_META, w), lambda bi, i: (bi, 0, 0))
    return pl.pallas_call(
        _mix_out_kernel,
        grid=(b, seq // tm),
        in_specs=[tile(d), tile(pool_w), tile(yb.shape[-1]), _const_spec(meta.shape),
                  mtile(pool_w), mtile(ybm.shape[-1]), _const_spec(w_out.shape), _const_spec(g.shape),
                  _const_spec(w_up.shape), _const_spec(conv_w.shape), _const_spec(conv_b.shape),
                  _const_spec(w_down.shape)],
        out_specs=tile(d),
        out_shape=jax.ShapeDtypeStruct((b, seq, d), x.dtype),
        scratch_shapes=[pltpu.VMEM((HALO + tm, d), BF16), pltpu.VMEM((tm, w_down.shape[0]), BF16)],
        compiler_params=pltpu.CompilerParams(
            dimension_semantics=("arbitrary", "arbitrary"), vmem_limit_bytes=VMEM_LIMIT_BYTES),
        name="mix_out",
    )(x, ya, yb, meta, yam, ybm, w_out, g, w_up, conv_w, conv_b, w_down)


def kernel(x, meta_tokens, norm_mix_g, w_in, w_pool, b_pool, pool_scale, q_norm_g, k_norm_g,
           lambda_q1, lambda_k1, lambda_q2, lambda_k2, subln_g, w_out, norm_ffn_g,
           w_up, conv_w, conv_b, w_down):
    depth = w_in.shape[0]
    assert depth == 1 and meta_tokens.shape[0] == N_META
    row = lambda a: a.reshape(1, -1).astype(F32)
    h = x
    meta = meta_tokens.astype(x.dtype)
    for i in range(depth):
        lam_init = 0.8 - 0.6 * math.exp(-0.3 * i)
        qg = jnp.tile(row(q_norm_g[i]), (1, 2)) * (QK_DIM ** -0.5)
        kg = jnp.tile(row(k_norm_g[i]), (1, 2))
        ya, q, k, v, yam, qm, km, vm = _mix_in(
            h, meta, row(norm_mix_g[i]), w_in[i].astype(BF16), w_pool[i].astype(BF16),
            row(b_pool[i]), row(pool_scale[i]), qg, kg)
        yb, ybm = _diff_attn(q, k, v, qm, km, vm, row(lambda_q1[i]), row(lambda_k1[i]),
                             row(lambda_q2[i]), row(lambda_k2[i]), row(subln_g[i]), lam_init)
        h = _mix_out(h, ya, yb, meta, yam, ybm, w_out[i].astype(BF16), row(norm_ffn_g[i]),
                     w_up[i].astype(BF16), conv_w[i].astype(F32), row(conv_b[i]), w_down[i].astype(BF16))
    return h
```

```python
import functools
import math

import jax
import jax.numpy as jnp
from jax import lax
from jax.experimental import pallas as pl
from jax.experimental.pallas import tpu as pltpu

F32 = jnp.float32
BF16 = jnp.bfloat16

N_META = 16
POOL_WINDOWS = (2, 4, 8, 16)
N_HEADS = 4
HEAD_DIM = 128
QK_DIM = 64
CONV_WIDTH = 3
EPS = 1e-6

LANES = 128
HALO = 16
VMEM_LIMIT_BYTES = 56 * 1024 * 1024

SEQ_TILE = 512
ATTN_TILE = 512
FF_CHUNK = 256


def _rmsnorm(x, g):
    return x * lax.rsqrt(jnp.mean(x * x, axis=-1, keepdims=True) + EPS) * g


def _const_spec(shape):
    nd = len(shape)
    return pl.BlockSpec(shape, lambda *_: (0,) * nd, pipeline_mode=pl.Buffered(1))


def _mix_in_kernel(x_ref, meta_ref, g_ref, win_ref, wpool_ref, bpool_ref, pscale_ref,
                   qg_ref, kg_ref,
                   ya_ref, q_ref, k_ref, v_ref, yam_ref, qm_ref, km_ref, vm_ref,
                   uext_ref):
    pool_w = bpool_ref.shape[-1]
    qk_w = q_ref.shape[-1]

    def heads(rows, is_meta, ya_out, q_out, k_out, v_out):
        r = rows.shape[0]
        n = _rmsnorm(rows, g_ref[...])
        proj = jnp.dot(n.astype(BF16), win_ref[...], preferred_element_type=F32)
        u = proj[:, :pool_w]
        uext_ref[HALO:HALO + r, :] = u
        for gi, w in enumerate(POOL_WINDOWS):
            cols = slice(gi * LANES, (gi + 1) * LANES)
            win_sum = u[:, cols]
            for j in range(1, w):
                win_sum = win_sum + uext_ref[HALO - j:HALO - j + r, cols]
            if is_meta:
                pos = lax.broadcasted_iota(jnp.int32, (r, 1), 0)
                mean = win_sum / jnp.minimum(pos + 1, w).astype(F32)
            else:
                mean = win_sum * (1.0 / w)
            pooled = mean - u[:, cols]
            y = jnp.dot(pooled.astype(BF16), wpool_ref[gi], preferred_element_type=F32)
            y = (y + bpool_ref[:, cols]) * pscale_ref[:, cols]
            ya_out[0, :, cols] = y.astype(ya_out.dtype)
        uext_ref[0:HALO, :] = uext_ref[r:r + HALO, :]

        lane = lax.broadcasted_iota(jnp.int32, (r, LANES), 1)
        lo = lane < QK_DIM
        for src, gain_ref, out in ((proj[:, pool_w:pool_w + qk_w], qg_ref, q_out),
                                   (proj[:, pool_w + qk_w:pool_w + 2 * qk_w], kg_ref, k_out)):
            for h in range(N_HEADS):
                cols = slice(h * LANES, (h + 1) * LANES)
                slab = src[:, cols]
                sq = slab * slab
                s_lo = jnp.sum(jnp.where(lo, sq, 0.0), axis=-1, keepdims=True)
                s_hi = jnp.sum(jnp.where(lo, 0.0, sq), axis=-1, keepdims=True)
                ms = jnp.where(lo, s_lo, s_hi) * (1.0 / QK_DIM)
                out[0, :, cols] = (slab * lax.rsqrt(ms + EPS) * gain_ref[...]).astype(out.dtype)
        v_out[0] = proj[:, pool_w + 2 * qk_w:].astype(v_out.dtype)

    @pl.when(pl.program_id(1) == 0)
    def _():
        uext_ref[0:HALO, :] = jnp.zeros((HALO, pool_w), F32)
        heads(meta_ref[...], True, yam_ref, qm_ref, km_ref, vm_ref)

    heads(x_ref[0], False, ya_ref, q_ref, k_ref, v_ref)


def _mix_in(x, meta, g, w_in, w_pool, b_pool, pool_scale, qg, kg):
    b, seq, d = x.shape
    pool_w = b_pool.shape[-1]
    qk_w = N_HEADS * HEAD_DIM
    tm = SEQ_TILE
    assert seq % tm == 0
    tile = lambda w: pl.BlockSpec((1, tm, w), lambda bi, i: (bi, i, 0))
    mtile = lambda w: pl.BlockSpec((1, N_META, w), lambda bi, i: (bi, 0, 0))
    big = lambda w: jax.ShapeDtypeStruct((b, seq, w), BF16)
    small = lambda w: jax.ShapeDtypeStruct((b, N_META, w), BF16)
    return pl.pallas_call(
        _mix_in_kernel,
        grid=(b, seq // tm),
        in_specs=[tile(d), _const_spec(meta.shape), _const_spec(g.shape), _const_spec(w_in.shape),
                  _const_spec(w_pool.shape), _const_spec(b_pool.shape), _const_spec(pool_scale.shape),
                  _const_spec(qg.shape), _const_spec(kg.shape)],
        out_specs=[tile(pool_w), tile(qk_w), tile(qk_w), tile(qk_w),
                   mtile(pool_w), mtile(qk_w), mtile(qk_w), mtile(qk_w)],
        out_shape=[big(pool_w), big(qk_w), big(qk_w), big(qk_w),
                   small(pool_w), small(qk_w), small(qk_w), small(qk_w)],
        scratch_shapes=[pltpu.VMEM((HALO + tm, pool_w), F32)],
        compiler_params=pltpu.CompilerParams(
            dimension_semantics=("arbitrary", "arbitrary"), vmem_limit_bytes=VMEM_LIMIT_BYTES),
        name="mix_in",
    )(x, meta, g, w_in, w_pool, b_pool, pool_scale, qg, kg)


def _stack_subheads(q):
    lane = lax.broadcasted_iota(jnp.int32, q.shape, 1)
    zero = jnp.zeros_like(q)
    return jnp.concatenate([jnp.where(lane < QK_DIM, q, zero), jnp.where(lane < QK_DIM, zero, q)], axis=0)


def _scores(qs, k):
    return lax.dot_general(qs, k, (((1,), (1,)), ((), ())), preferred_element_type=F32)


def _causal_mask(s):
    r = s.shape[0] // 2
    row = lax.broadcasted_iota(jnp.int32, s.shape, 0)
    row = jnp.where(row >= r, row - r, row)
    col = lax.broadcasted_iota(jnp.int32, s.shape, 1)
    return jnp.where(col <= row, s, -jnp.inf)


def _online_softmax_step(s, v, m_ref, l_ref, acc_ref):
    m = m_ref[...]
    m_new = jnp.maximum(m, jnp.max(s, axis=-1, keepdims=True))
    alpha = jnp.exp(m - m_new)
    width = s.shape[1]
    m_wide = pltpu.repeat(m_new, width // LANES, axis=1) if width > LANES else m_new[:, :width]
    p = jnp.exp(s - m_wide)
    l_ref[...] = alpha * l_ref[...] + jnp.sum(p, axis=-1, keepdims=True)
    acc_ref[...] = alpha * acc_ref[...] + jnp.dot(p.astype(v.dtype), v, preferred_element_type=F32)
    m_ref[...] = m_new


def _init_state(m_ref, l_ref, acc_ref):
    m_ref[...] = jnp.full(m_ref.shape, -jnp.inf, F32)
    l_ref[...] = jnp.zeros(l_ref.shape, F32)
    acc_ref[...] = jnp.zeros(acc_ref.shape, F32)


def _diff_attn_kernel(lam_init, q_ref, k_ref, v_ref, qm_ref, km_ref, vm_ref,
                      lq1_ref, lk1_ref, lq2_ref, lk2_ref, sg_ref, yb_ref, ybm_ref,
                      qs_ref, m_ref, l_ref, acc_ref):
    i = pl.program_id(2)
    t = q_ref.shape[1]
    lam = (jnp.exp(jnp.sum(lq1_ref[...] * lk1_ref[...], axis=-1, keepdims=True))
           - jnp.exp(jnp.sum(lq2_ref[...] * lk2_ref[...], axis=-1, keepdims=True)) + lam_init)

    def finish(r, out_ref):
        o = (acc_ref[0:r, :] / l_ref[0:r, :]
             - lam * (acc_ref[t:t + r, :] / l_ref[t:t + r, :]))
        out_ref[0] = (_rmsnorm(o, sg_ref[...]) * (1.0 - lam_init)).astype(out_ref.dtype)

    km = km_ref[0]
    vm = vm_ref[0]

    @pl.when(i == 0)
    def _():
        qm = _stack_subheads(qm_ref[0])
        for half in range(2):
            rows = slice(half * t, half * t + N_META)
            _init_state(m_ref.at[rows], l_ref.at[rows], acc_ref.at[rows])
        sm = _causal_mask(_scores(qm, km))
        for half in range(2):
            rows = slice(half * t, half * t + N_META)
            _online_softmax_step(sm[half * N_META:(half + 1) * N_META], vm,
                                 m_ref.at[rows], l_ref.at[rows], acc_ref.at[rows])
        finish(N_META, ybm_ref)

    def kv_tile(ref, j):
        return ref[0, pl.ds(pl.multiple_of(j * t, t), t), :]

    qs_ref[...] = _stack_subheads(q_ref[0])
    _init_state(m_ref, l_ref, acc_ref)
    _online_softmax_step(_scores(qs_ref[...], km), vm, m_ref, l_ref, acc_ref)

    @pl.loop(0, i)
    def _(j):
        _online_softmax_step(_scores(qs_ref[...], kv_tile(k_ref, j)), kv_tile(v_ref, j),
                             m_ref, l_ref, acc_ref)

    _online_softmax_step(_causal_mask(_scores(qs_ref[...], kv_tile(k_ref, i))), kv_tile(v_ref, i),
                         m_ref, l_ref, acc_ref)
    finish(t, yb_ref)


def _diff_attn(q, k, v, qm, km, vm, lq1, lk1, lq2, lk2, subln_g, lam_init):
    b, seq, width = q.shape
    t = ATTN_TILE
    assert seq % t == 0 and width == N_HEADS * HEAD_DIM
    qtile = pl.BlockSpec((1, t, HEAD_DIM), lambda bi, h, i: (bi, i, h))
    full = pl.BlockSpec((1, seq, HEAD_DIM), lambda bi, h, i: (bi, 0, h))
    mtile = pl.BlockSpec((1, N_META, HEAD_DIM), lambda bi, h, i: (bi, 0, h))
    return pl.pallas_call(
        functools.partial(_diff_attn_kernel, lam_init),
        grid=(b, N_HEADS, seq // t),
        in_specs=[qtile, full, full, mtile, mtile, mtile,
                  _const_spec(lq1.shape), _const_spec(lk1.shape), _const_spec(lq2.shape),
                  _const_spec(lk2.shape), _const_spec(subln_g.shape)],
        out_specs=[qtile, mtile],
        out_shape=[jax.ShapeDtypeStruct((b, seq, width), BF16),
                   jax.ShapeDtypeStruct((b, N_META, width), BF16)],
        scratch_shapes=[pltpu.VMEM((2 * t, HEAD_DIM), BF16), pltpu.VMEM((2 * t, LANES), F32),
                        pltpu.VMEM((2 * t, LANES), F32), pltpu.VMEM((2 * t, HEAD_DIM), F32)],
        compiler_params=pltpu.CompilerParams(
            dimension_semantics=("arbitrary", "arbitrary", "arbitrary"),
            vmem_limit_bytes=VMEM_LIMIT_BYTES),
        name="diff_attn",
    )(q, k, v, qm, km, vm, lq1, lk1, lq2, lk2, subln_g)


def _mix_out_kernel(x_ref, ya_ref, yb_ref, meta_ref, yam_ref, ybm_ref, wo_ref, g_ref,
                    wup_ref, cw_ref, cb_ref, wdown_ref, out_ref, next_ref, act_ref):
    tm = x_ref.shape[1]
    pool_w = ya_ref.shape[-1]
    d_ff = wdown_ref.shape[0]

    def mix_residual(rows, ya, yb):
        h1 = (rows + jnp.dot(ya, wo_ref[:pool_w, :], preferred_element_type=F32)
              + jnp.dot(yb, wo_ref[pool_w:, :], preferred_element_type=F32))
        return h1, _rmsnorm(h1, g_ref[...]).astype(BF16)

    @pl.when(pl.program_id(1) == 0)
    def _():
        _, nm = mix_residual(meta_ref[...], yam_ref[0], ybm_ref[0])
        next_ref[0:HALO, :] = nm

    h1, n2 = mix_residual(x_ref[0], ya_ref[0], yb_ref[0])
    next_ref[HALO:HALO + tm, :] = n2

    def conv(u, cols):
        out = cb_ref[:, cols]
        for tap in range(CONV_WIDTH):
            off = HALO - (CONV_WIDTH - 1) + tap
            out = out + u[off:off + tm, :] * cw_ref[tap:tap + 1, cols]
        return out

    for c in range(d_ff // FF_CHUNK):
        gcols = slice(c * FF_CHUNK, (c + 1) * FF_CHUNK)
        vcols = slice(d_ff + c * FF_CHUNK, d_ff + (c + 1) * FF_CHUNK)
        ug = jnp.dot(next_ref[...], wup_ref[:, gcols], preferred_element_type=F32)
        uv = jnp.dot(next_ref[...], wup_ref[:, vcols], preferred_element_type=F32)
        gate = conv(ug, gcols)
        act_ref[:, gcols] = (gate * jax.nn.sigmoid(gate) * conv(uv, vcols)).astype(BF16)
    out_ref[0] = h1 + jnp.dot(act_ref[...], wdown_ref[...], preferred_element_type=F32)
    next_ref[0:HALO, :] = next_ref[tm:tm + HALO, :]


def _mix_out(x, ya, yb, meta, yam, ybm, w_out, g, w_up, conv_w, conv_b, w_down):
    b, seq, d = x.shape
    pool_w = ya.shape[-1]
    tm = SEQ_TILE
    assert seq % tm == 0 and w_down.shape[0] % FF_CHUNK == 0
    tile = lambda w: pl.BlockSpec((1, tm, w), lambda bi, i: (bi, i, 0))
    mtile = lambda w: pl.BlockSpec((1, N_META, w), lambda bi, i: (bi, 0, 0))
    return pl.pallas_call(
        _mix_out_kernel,
        grid=(b, seq // tm),
        in_specs=[tile(d), tile(pool_w), tile(yb.shape[-1]), _const_spec(meta.shape),
                  mtile(pool_w), mtile(ybm.shape[-1]), _const_spec(w_out.shape), _const_spec(g.shape),
                  _const_spec(w_up.shape), _const_spec(conv_w.shape), _const_spec(conv_b.shape),
                  _const_spec(w_down.shape)],
        out_specs=tile(d),
        out_shape=jax.ShapeDtypeStruct((b, seq, d), x.dtype),
        scratch_shapes=[pltpu.VMEM((HALO + tm, d), BF16), pltpu.VMEM((tm, w_down.shape[0]), BF16)],
        compiler_params=pltpu.CompilerParams(
            dimension_semantics=("arbitrary", "arbitrary"), vmem_limit_bytes=VMEM_LIMIT_BYTES),
        name="mix_out",
    )(x, ya, yb, meta, yam, ybm, w_out, g, w_up, conv_w, conv_b, w_down)


def kernel(x, meta_tokens, norm_mix_g, w_in, w_pool, b_pool, pool_scale, q_norm_g, k_norm_g,
           lambda_q1, lambda_k1, lambda_q2, lambda_k2, subln_g, w_out, norm_ffn_g,
           w_up, conv_w, conv_b, w_down):
    depth = w_in.shape[0]
    assert depth == 1 and meta_tokens.shape[0] == N_META
    row = lambda a: a.reshape(1, -1).astype(F32)
    h = x
    meta = meta_tokens.astype(x.dtype)
    for i in range(depth):
        lam_init = 0.8 - 0.6 * math.exp(-0.3 * i)
        qg = jnp.tile(row(q_norm_g[i]), (1, 2)) * (QK_DIM ** -0.5)
        kg = jnp.tile(row(k_norm_g[i]), (1, 2))
        ya, q, k, v, yam, qm, km, vm = _mix_in(
            h, meta, row(norm_mix_g[i]), w_in[i].astype(BF16), w_pool[i].astype(BF16),
            row(b_pool[i]), row(pool_scale[i]), qg, kg)
        yb, ybm = _diff_attn(q, k, v, qm, km, vm, row(lambda_q1[i]), row(lambda_k1[i]),
                             row(lambda_q2[i]), row(lambda_k2[i]), row(subln_g[i]), lam_init)
        h = _mix_out(h, ya, yb, meta, yam, ybm, w_out[i].astype(BF16), row(norm_ffn_g[i]),
                     w_up[i].astype(BF16), conv_w[i].astype(F32), row(conv_b[i]), w_down[i].astype(BF16))
    return h
```

```python
import functools
import math

import jax
import jax.numpy as jnp
from jax import lax
from jax.experimental import pallas as pl
from jax.experimental.pallas import tpu as pltpu

F32 = jnp.float32
BF16 = jnp.bfloat16

N_META = 16
POOL_WINDOWS = (2, 4, 8, 16)
N_HEADS = 4
HEAD_DIM = 128
QK_DIM = 64
CONV_WIDTH = 3
EPS = 1e-6

LANES = 128
HALO = 16
VMEM_LIMIT_BYTES = 56 * 1024 * 1024

SEQ_TILE = 512
ATTN_TILE = 512
FF_CHUNK = 256


def _rmsnorm(x, g):
    return x * lax.rsqrt(jnp.mean(x * x, axis=-1, keepdims=True) + EPS) * g


def _const_spec(shape):
    nd = len(shape)
    return pl.BlockSpec(shape, lambda *_: (0,) * nd, pipeline_mode=pl.Buffered(1))


def _mix_in_kernel(x_ref, meta_ref, g_ref, win_ref, wpool_ref, bpool_ref, pscale_ref,
                   qg_ref, kg_ref,
                   ya_ref, q_ref, k_ref, v_ref, yam_ref, qm_ref, km_ref, vm_ref,
                   uext_ref):
    pool_w = bpool_ref.shape[-1]
    qk_w = q_ref.shape[-1]

    def heads(rows, is_meta, ya_out, q_out, k_out, v_out):
        r = rows.shape[0]
        n = _rmsnorm(rows, g_ref[...])
        proj = jnp.dot(n.astype(BF16), win_ref[...], preferred_element_type=F32)
        u = proj[:, :pool_w]
        uext_ref[HALO:HALO + r, :] = u
        for gi, w in enumerate(POOL_WINDOWS):
            cols = slice(gi * LANES, (gi + 1) * LANES)
            win_sum = u[:, cols]
            for j in range(1, w):
                win_sum = win_sum + uext_ref[HALO - j:HALO - j + r, cols]
            if is_meta:
                pos = lax.broadcasted_iota(jnp.int32, (r, 1), 0)
                mean = win_sum / jnp.minimum(pos + 1, w).astype(F32)
            else:
                mean = win_sum * (1.0 / w)
            pooled = mean - u[:, cols]
            y = jnp.dot(pooled.astype(BF16), wpool_ref[gi], preferred_element_type=F32)
            y = (y + bpool_ref[:, cols]) * pscale_ref[:, cols]
            ya_out[0, :, cols] = y.astype(ya_out.dtype)
        uext_ref[0:HALO, :] = uext_ref[r:r + HALO, :]

        lane = lax.broadcasted_iota(jnp.int32, (r, LANES), 1)
        lo = lane < QK_DIM
        for src, gain_ref, out in ((proj[:, pool_w:pool_w + qk_w], qg_ref, q_out),
                                   (proj[:, pool_w + qk_w:pool_w + 2 * qk_w], kg_ref, k_out)):
            for h in range(N_HEADS):
                cols = slice(h * LANES, (h + 1) * LANES)
                slab = src[:, cols]
                sq = slab * slab
                s_lo = jnp.sum(jnp.where(lo, sq, 0.0), axis=-1, keepdims=True)
                s_hi = jnp.sum(jnp.where(lo, 0.0, sq), axis=-1, keepdims=True)
                ms = jnp.where(lo, s_lo, s_hi) * (1.0 / QK_DIM)
                out[0, :, cols] = (slab * lax.rsqrt(ms + EPS) * gain_ref[...]).astype(out.dtype)
        for h in range(N_HEADS):
            vcols = slice(pool_w + 2 * qk_w + h * HEAD_DIM, pool_w + 2 * qk_w + (h + 1) * HEAD_DIM)
            v_out[0, :, 2 * h * HEAD_DIM:(2 * h + 1) * HEAD_DIM] = proj[:, vcols].astype(v_out.dtype)
            v_out[0, :, (2 * h + 1) * HEAD_DIM:(2 * h + 2) * HEAD_DIM] = jnp.ones((r, HEAD_DIM), v_out.dtype)

    @pl.when(pl.program_id(1) == 0)
    def _():
        uext_ref[0:HALO, :] = jnp.zeros((HALO, pool_w), F32)
        heads(meta_ref[...], True, yam_ref, qm_ref, km_ref, vm_ref)

    heads(x_ref[0], False, ya_ref, q_ref, k_ref, v_ref)


def _mix_in(x, meta, g, w_in, w_pool, b_pool, pool_scale, qg, kg):
    b, seq, d = x.shape
    pool_w = b_pool.shape[-1]
    qk_w = N_HEADS * HEAD_DIM
    tm = SEQ_TILE
    assert seq % tm == 0
    tile = lambda w: pl.BlockSpec((1, tm, w), lambda bi, i: (bi, i, 0))
    mtile = lambda w: pl.BlockSpec((1, N_META, w), lambda bi, i: (bi, 0, 0))
    big = lambda w: jax.ShapeDtypeStruct((b, seq, w), BF16)
    small = lambda w: jax.ShapeDtypeStruct((b, N_META, w), BF16)
    return pl.pallas_call(
        _mix_in_kernel,
        grid=(b, seq // tm),
        in_specs=[tile(d), _const_spec(meta.shape), _const_spec(g.shape), _const_spec(w_in.shape),
                  _const_spec(w_pool.shape), _const_spec(b_pool.shape), _const_spec(pool_scale.shape),
                  _const_spec(qg.shape), _const_spec(kg.shape)],
        out_specs=[tile(pool_w), tile(qk_w), tile(qk_w), tile(2 * qk_w),
                   mtile(pool_w), mtile(qk_w), mtile(qk_w), mtile(2 * qk_w)],
        out_shape=[big(pool_w), big(qk_w), big(qk_w), big(2 * qk_w),
                   small(pool_w), small(qk_w), small(qk_w), small(2 * qk_w)],
        scratch_shapes=[pltpu.VMEM((HALO + tm, pool_w), F32)],
        compiler_params=pltpu.CompilerParams(
            dimension_semantics=("arbitrary", "arbitrary"), vmem_limit_bytes=VMEM_LIMIT_BYTES),
        name="mix_in",
    )(x, meta, g, w_in, w_pool, b_pool, pool_scale, qg, kg)


def _stack_subheads(q):
    lane = lax.broadcasted_iota(jnp.int32, q.shape, 1)
    zero = jnp.zeros_like(q)
    return jnp.concatenate([jnp.where(lane < QK_DIM, q, zero), jnp.where(lane < QK_DIM, zero, q)], axis=0)


def _scores(qs, k):
    return lax.dot_general(qs, k, (((1,), (1,)), ((), ())), preferred_element_type=F32)


def _causal_mask(s):
    r = s.shape[0] // 2
    row = lax.broadcasted_iota(jnp.int32, s.shape, 0)
    row = jnp.where(row >= r, row - r, row)
    col = lax.broadcasted_iota(jnp.int32, s.shape, 1)
    return jnp.where(col <= row, s, -jnp.inf)


def _online_softmax_step(s, v_ext, m_ref, acc_ref):
    width = s.shape[1]
    m = m_ref[...]
    m_new = jnp.maximum(m, jnp.max(s, axis=-1, keepdims=True))
    alpha = jnp.exp2(m - m_new)
    m_wide = jnp.concatenate([m_new] * (width // LANES), axis=1) if width > LANES else m_new[:, :width]
    p = jnp.exp2(s - m_wide)
    acc_ref[...] = (jnp.concatenate([alpha, alpha], axis=1) * acc_ref[...]
                    + jnp.dot(p.astype(v_ext.dtype), v_ext, preferred_element_type=F32))
    m_ref[...] = m_new


def _init_state(m_ref, acc_ref):
    m_ref[...] = jnp.full(m_ref.shape, -jnp.inf, F32)
    acc_ref[...] = jnp.zeros(acc_ref.shape, F32)


def _diff_attn_kernel(lam_init, q_ref, k_ref, v_ref, qm_ref, km_ref, vm_ref,
                      lq1_ref, lk1_ref, lq2_ref, lk2_ref, sg_ref, yb_ref, ybm_ref,
                      qs_ref, m_ref, acc_ref, sa_ref, sb_ref):
    i = pl.program_id(2)
    t = q_ref.shape[1]
    lam = (jnp.exp(jnp.sum(lq1_ref[...] * lk1_ref[...], axis=-1, keepdims=True))
           - jnp.exp(jnp.sum(lq2_ref[...] * lk2_ref[...], axis=-1, keepdims=True)) + lam_init)

    def finish(r, out_ref):
        o = (acc_ref[0:r, :HEAD_DIM] / acc_ref[0:r, HEAD_DIM:]
             - lam * (acc_ref[t:t + r, :HEAD_DIM] / acc_ref[t:t + r, HEAD_DIM:]))
        out_ref[0] = (_rmsnorm(o, sg_ref[...]) * (1.0 - lam_init)).astype(out_ref.dtype)

    km = km_ref[0]
    vm = vm_ref[0]

    @pl.when(i == 0)
    def _():
        qm = _stack_subheads(qm_ref[0])
        for half in range(2):
            rows = slice(half * t, half * t + N_META)
            _init_state(m_ref.at[rows], acc_ref.at[rows])
        sm = _causal_mask(_scores(qm, km))
        for half in range(2):
            rows = slice(half * t, half * t + N_META)
            _online_softmax_step(sm[half * N_META:(half + 1) * N_META], vm,
                                 m_ref.at[rows], acc_ref.at[rows])
        finish(N_META, ybm_ref)

    def kv_tile(ref, j):
        return ref[0, pl.ds(pl.multiple_of(j * t, t), t), :]

    qs_ref[...] = _stack_subheads(q_ref[0])
    _init_state(m_ref, acc_ref)
    _online_softmax_step(_scores(qs_ref[...], km), vm, m_ref, acc_ref)

    def step(cur_ref, nxt_ref, j, causal=False):
        if nxt_ref is not None:
            nxt_ref[...] = _scores(qs_ref[...], kv_tile(k_ref, j + 1))
        s = _causal_mask(cur_ref[...]) if causal else cur_ref[...]
        _online_softmax_step(s, kv_tile(v_ref, j), m_ref, acc_ref)

    sa_ref[...] = _scores(qs_ref[...], kv_tile(k_ref, 0))

    @pl.loop(0, i // 2)
    def _(jj):
        step(sa_ref, sb_ref, 2 * jj)
        step(sb_ref, sa_ref, 2 * jj + 1)

    @pl.when(i % 2 == 1)
    def _():
        step(sa_ref, sb_ref, i - 1)
        step(sb_ref, None, i, causal=True)

    @pl.when(i % 2 == 0)
    def _():
        step(sa_ref, None, i, causal=True)

    finish(t, yb_ref)


def _diff_attn(q, k, v, qm, km, vm, lq1, lk1, lq2, lk2, subln_g, lam_init):
    b, seq, width = q.shape
    t = ATTN_TILE
    assert seq % t == 0 and width == N_HEADS * HEAD_DIM
    qtile = pl.BlockSpec((1, t, HEAD_DIM), lambda bi, h, i: (bi, i, h))
    full = lambda w: pl.BlockSpec((1, seq, w), lambda bi, h, i: (bi, 0, h))
    mtile = lambda w: pl.BlockSpec((1, N_META, w), lambda bi, h, i: (bi, 0, h))
    return pl.pallas_call(
        functools.partial(_diff_attn_kernel, lam_init),
        grid=(b, N_HEADS, seq // t),
        in_specs=[qtile, full(HEAD_DIM), full(2 * HEAD_DIM), mtile(HEAD_DIM), mtile(HEAD_DIM), mtile(2 * HEAD_DIM),
                  _const_spec(lq1.shape), _const_spec(lk1.shape), _const_spec(lq2.shape),
                  _const_spec(lk2.shape), _const_spec(subln_g.shape)],
        out_specs=[qtile, mtile(HEAD_DIM)],
        out_shape=[jax.ShapeDtypeStruct((b, seq, width), BF16),
                   jax.ShapeDtypeStruct((b, N_META, width), BF16)],
        scratch_shapes=[pltpu.VMEM((2 * t, HEAD_DIM), BF16), pltpu.VMEM((2 * t, LANES), F32),
                        pltpu.VMEM((2 * t, 2 * HEAD_DIM), F32),
                        pltpu.VMEM((2 * t, t), F32), pltpu.VMEM((2 * t, t), F32)],
        compiler_params=pltpu.CompilerParams(
            dimension_semantics=("arbitrary", "arbitrary", "arbitrary"),
            vmem_limit_bytes=VMEM_LIMIT_BYTES),
        name="diff_attn",
    )(q, k, v, qm, km, vm, lq1, lk1, lq2, lk2, subln_g)


def _mix_out_kernel(x_ref, ya_ref, yb_ref, meta_ref, yam_ref, ybm_ref, wo_ref, g_ref,
                    wup_ref, cw_ref, cb_ref, wdown_ref, out_ref, next_ref, act_ref):
    tm = x_ref.shape[1]
    pool_w = ya_ref.shape[-1]
    d_ff = wdown_ref.shape[0]

    def mix_residual(rows, ya, yb):
        h1 = (rows + jnp.dot(ya, wo_ref[:pool_w, :], preferred_element_type=F32)
              + jnp.dot(yb, wo_ref[pool_w:, :], preferred_element_type=F32))
        return h1, _rmsnorm(h1, g_ref[...]).astype(BF16)

    @pl.when(pl.program_id(1) == 0)
    def _():
        _, nm = mix_residual(meta_ref[...], yam_ref[0], ybm_ref[0])
        next_ref[0:HALO, :] = nm

    h1, n2 = mix_residual(x_ref[0], ya_ref[0], yb_ref[0])
    next_ref[HALO:HALO + tm, :] = n2

    def conv(u, cols):
        out = cb_ref[:, cols]
        for tap in range(CONV_WIDTH):
            off = HALO - (CONV_WIDTH - 1) + tap
            out = out + u[off:off + tm, :] * cw_ref[tap:tap + 1, cols]
        return out

    for c in range(d_ff // FF_CHUNK):
        gcols = slice(c * FF_CHUNK, (c + 1) * FF_CHUNK)
        vcols = slice(d_ff + c * FF_CHUNK, d_ff + (c + 1) * FF_CHUNK)
        ug = jnp.dot(next_ref[...], wup_ref[:, gcols], preferred_element_type=F32)
        uv = jnp.dot(next_ref[...], wup_ref[:, vcols], preferred_element_type=F32)
        gate = conv(ug, gcols)
        act_ref[:, gcols] = (gate * jax.nn.sigmoid(gate) * conv(uv, vcols)).astype(BF16)
    out_ref[0] = h1 + jnp.dot(act_ref[...], wdown_ref[...], preferred_element_type=F32)
    next_ref[0:HALO, :] = next_ref[tm:tm + HALO, :]


def _mix_out(x, ya, yb, meta, yam, ybm, w_out, g, w_up, conv_w, conv_b, w_down):
    b, seq, d = x.shape
    pool_w = ya.shape[-1]
    tm = SEQ_TILE
    assert seq % tm == 0 and w_down.shape[0] % FF_CHUNK == 0
    tile = lambda w: pl.BlockSpec((1, tm, w), lambda bi, i: (bi, i, 0))
    mtile = lambda w: pl.BlockSpec((1, N_META, w), lambda bi, i: (bi, 0, 0))
    return pl.pallas_call(
        _mix_out_kernel,
        grid=(b, seq // tm),
        in_specs=[tile(d), tile(pool_w), tile(yb.shape[-1]), _const_spec(meta.shape),
                  mtile(pool_w), mtile(ybm.shape[-1]), _const_spec(w_out.shape), _const_spec(g.shape),
                  _const_spec(w_up.shape), _const_spec(conv_w.shape), _const_spec(conv_b.shape),
                  _const_spec(w_down.shape)],
        out_specs=tile(d),
        out_shape=jax.ShapeDtypeStruct((b, seq, d), x.dtype),
        scratch_shapes=[pltpu.VMEM((HALO + tm, d), BF16), pltpu.VMEM((tm, w_down.shape[0]), BF16)],
        compiler_params=pltpu.CompilerParams(
            dimension_semantics=("arbitrary", "arbitrary"), vmem_limit_bytes=VMEM_LIMIT_BYTES),
        name="mix_out",
    )(x, ya, yb, meta, yam, ybm, w_out, g, w_up, conv_w, conv_b, w_down)


def kernel(x, meta_tokens, norm_mix_g, w_in, w_pool, b_pool, pool_scale, q_norm_g, k_norm_g,
           lambda_q1, lambda_k1, lambda_q2, lambda_k2, subln_g, w_out, norm_ffn_g,
           w_up, conv_w, conv_b, w_down):
    depth = w_in.shape[0]
    assert depth == 1 and meta_tokens.shape[0] == N_META
    row = lambda a: a.reshape(1, -1).astype(F32)
    h = x
    meta = meta_tokens.astype(x.dtype)
    for i in range(depth):
        lam_init = 0.8 - 0.6 * math.exp(-0.3 * i)
        qg = jnp.tile(row(q_norm_g[i]), (1, 2)) * (QK_DIM ** -0.5 * math.log2(math.e))
        kg = jnp.tile(row(k_norm_g[i]), (1, 2))
        ya, q, k, v, yam, qm, km, vm = _mix_in(
            h, meta, row(norm_mix_g[i]), w_in[i].astype(BF16), w_pool[i].astype(BF16),
            row(b_pool[i]), row(pool_scale[i]), qg, kg)
        yb, ybm = _diff_attn(q, k, v, qm, km, vm, row(lambda_q1[i]), row(lambda_k1[i]),
                             row(lambda_q2[i]), row(lambda_k2[i]), row(subln_g[i]), lam_init)
        h = _mix_out(h, ya, yb, meta, yam, ybm, w_out[i].astype(BF16), row(norm_ffn_g[i]),
                     w_up[i].astype(BF16), conv_w[i].astype(F32), row(conv_b[i]), w_down[i].astype(BF16))
    return h
```

```python
import functools
import math

import jax
import jax.numpy as jnp
from jax import lax
from jax.experimental import pallas as pl
from jax.experimental.pallas import tpu as pltpu

F32 = jnp.float32
BF16 = jnp.bfloat16

N_META = 16
POOL_WINDOWS = (2, 4, 8, 16)
N_HEADS = 4
HEAD_DIM = 128
QK_DIM = 64
CONV_WIDTH = 3
EPS = 1e-6

LANES = 128
HALO = 16
VMEM_LIMIT_BYTES = 56 * 1024 * 1024

SEQ_TILE = 512
ATTN_TILE = 512
FF_CHUNK = 256

def _rmsnorm(x, g):
    return x * lax.rsqrt(jnp.mean(x * x, axis=-1, keepdims=True) + EPS) * g


def _const_spec(shape):
    nd = len(shape)
    return pl.BlockSpec(shape, lambda *_: (0,) * nd, pipeline_mode=pl.Buffered(1))


def _mix_in_kernel(x_ref, meta_ref, g_ref, win_ref, wpool_ref, bpool_ref, pscale_ref,
                   qg_ref, kg_ref,
                   ya_ref, q_ref, k_ref, v_ref, yam_ref, qm_ref, km_ref, vm_ref,
                   uext_ref):
    pool_w = bpool_ref.shape[-1]
    qk_w = q_ref.shape[-1]

    def heads(rows, is_meta, ya_out, q_out, k_out, v_out):
        r = rows.shape[0]
        n = _rmsnorm(rows, g_ref[...])
        proj = jnp.dot(n.astype(BF16), win_ref[...], preferred_element_type=F32)
        u = proj[:, :pool_w]
        uext_ref[HALO:HALO + r, :] = u
        for gi, w in enumerate(POOL_WINDOWS):
            cols = slice(gi * LANES, (gi + 1) * LANES)
            win_sum = u[:, cols]
            for j in range(1, w):
                win_sum = win_sum + uext_ref[HALO - j:HALO - j + r, cols]
            if is_meta:
                pos = lax.broadcasted_iota(jnp.int32, (r, 1), 0)
                mean = win_sum / jnp.minimum(pos + 1, w).astype(F32)
            else:
                mean = win_sum * (1.0 / w)
            pooled = mean - u[:, cols]
            y = jnp.dot(pooled.astype(BF16), wpool_ref[gi], preferred_element_type=F32)
            y = (y + bpool_ref[:, cols]) * pscale_ref[:, cols]
            ya_out[0, :, cols] = y.astype(ya_out.dtype)
        uext_ref[0:HALO, :] = uext_ref[r:r + HALO, :]

        lane = lax.broadcasted_iota(jnp.int32, (r, LANES), 1)
        lo = lane < QK_DIM
        for src, gain_ref, out in ((proj[:, pool_w:pool_w + qk_w], qg_ref, q_out),
                                   (proj[:, pool_w + qk_w:pool_w + 2 * qk_w], kg_ref, k_out)):
            for h in range(N_HEADS):
                cols = slice(h * LANES, (h + 1) * LANES)
                slab = src[:, cols]
                sq = slab * slab
                s_lo = jnp.sum(jnp.where(lo, sq, 0.0), axis=-1, keepdims=True)
                s_hi = jnp.sum(jnp.where(lo, 0.0, sq), axis=-1, keepdims=True)
                ms = jnp.where(lo, s_lo, s_hi) * (1.0 / QK_DIM)
                out[0, :, cols] = (slab * lax.rsqrt(ms + EPS) * gain_ref[...]).astype(out.dtype)
        for h in range(N_HEADS):
            vcols = slice(pool_w + 2 * qk_w + h * HEAD_DIM, pool_w + 2 * qk_w + (h + 1) * HEAD_DIM)
            v_out[0, :, 2 * h * HEAD_DIM:(2 * h + 1) * HEAD_DIM] = proj[:, vcols].astype(v_out.dtype)
            v_out[0, :, (2 * h + 1) * HEAD_DIM:(2 * h + 2) * HEAD_DIM] = jnp.ones((r, HEAD_DIM), v_out.dtype)

    @pl.when(pl.program_id(1) == 0)
    def _():
        uext_ref[0:HALO, :] = jnp.zeros((HALO, pool_w), F32)
        heads(meta_ref[...], True, yam_ref, qm_ref, km_ref, vm_ref)

    heads(x_ref[0], False, ya_ref, q_ref, k_ref, v_ref)


def _mix_in(x, meta, g, w_in, w_pool, b_pool, pool_scale, qg, kg):
    b, seq, d = x.shape
    pool_w = b_pool.shape[-1]
    qk_w = N_HEADS * HEAD_DIM
    tm = SEQ_TILE
    assert seq % tm == 0
    tile = lambda w: pl.BlockSpec((1, tm, w), lambda bi, i: (bi, i, 0))
    mtile = lambda w: pl.BlockSpec((1, N_META, w), lambda bi, i: (bi, 0, 0))
    big = lambda w: jax.ShapeDtypeStruct((b, seq, w), BF16)
    small = lambda w: jax.ShapeDtypeStruct((b, N_META, w), BF16)
    return pl.pallas_call(
        _mix_in_kernel,
        grid=(b, seq // tm),
        in_specs=[tile(d), _const_spec(meta.shape), _const_spec(g.shape), _const_spec(w_in.shape),
                  _const_spec(w_pool.shape), _const_spec(b_pool.shape), _const_spec(pool_scale.shape),
                  _const_spec(qg.shape), _const_spec(kg.shape)],
        out_specs=[tile(pool_w), tile(qk_w), tile(qk_w), tile(2 * qk_w),
                   mtile(pool_w), mtile(qk_w), mtile(qk_w), mtile(2 * qk_w)],
        out_shape=[big(pool_w), big(qk_w), big(qk_w), big(2 * qk_w),
                   small(pool_w), small(qk_w), small(qk_w), small(2 * qk_w)],
        scratch_shapes=[pltpu.VMEM((HALO + tm, pool_w), F32)],
        compiler_params=pltpu.CompilerParams(
            dimension_semantics=("arbitrary", "arbitrary"), vmem_limit_bytes=VMEM_LIMIT_BYTES),
        name="mix_in",
    )(x, meta, g, w_in, w_pool, b_pool, pool_scale, qg, kg)


def _stack_subheads(q):
    lane = lax.broadcasted_iota(jnp.int32, q.shape, 1)
    zero = jnp.zeros_like(q)
    return jnp.concatenate([jnp.where(lane < QK_DIM, q, zero), jnp.where(lane < QK_DIM, zero, q)], axis=0)


def _scores(qs, k):
    return lax.dot_general(qs, k, (((1,), (1,)), ((), ())), preferred_element_type=F32)


def _causal_mask(s):
    r = s.shape[0] // 2
    row = lax.broadcasted_iota(jnp.int32, s.shape, 0)
    row = jnp.where(row >= r, row - r, row)
    col = lax.broadcasted_iota(jnp.int32, s.shape, 1)
    return jnp.where(col <= row, s, -jnp.inf)


def _online_softmax_step(s, v_ext, m_ref, acc_ref):
    width = s.shape[1]
    m = m_ref[...]
    m_new = jnp.maximum(m, jnp.max(s, axis=-1, keepdims=True))
    alpha = jnp.exp2(m - m_new)
    m_wide = jnp.concatenate([m_new] * (width // LANES), axis=1) if width > LANES else m_new[:, :width]
    p = jnp.exp2(s - m_wide)
    acc_ref[...] = (jnp.concatenate([alpha, alpha], axis=1) * acc_ref[...]
                    + jnp.dot(p.astype(v_ext.dtype), v_ext, preferred_element_type=F32))
    m_ref[...] = m_new


def _init_state(m_ref, acc_ref):
    m_ref[...] = jnp.full(m_ref.shape, -jnp.inf, F32)
    acc_ref[...] = jnp.zeros(acc_ref.shape, F32)


def _first_softmax_step(s, v_ext, m_ref, acc_ref):
    m = jnp.max(s, axis=-1, keepdims=True)
    p = jnp.exp2(s - m)
    m_ref[...] = jnp.broadcast_to(m, m_ref.shape)
    acc_ref[...] = jnp.dot(p.astype(v_ext.dtype), v_ext, preferred_element_type=F32)


def _diff_attn_kernel(lam_init, q_ref, qnext_ref, k_ref, v_ref, qm_ref, km_ref, vm_ref,
                      lq1_ref, lk1_ref, lq2_ref, lk2_ref, sg_ref, yb_ref, ybm_ref,
                      qs_ref, qn_ref, m_ref, acc_ref, sa_ref, sb_ref, sc_ref):
    i = pl.program_id(2)
    t = q_ref.shape[1]
    lam = (jnp.exp(jnp.sum(lq1_ref[...] * lk1_ref[...], axis=-1, keepdims=True))
           - jnp.exp(jnp.sum(lq2_ref[...] * lk2_ref[...], axis=-1, keepdims=True)) + lam_init)

    def finish(r, out_ref):
        o = (acc_ref[0:r, :HEAD_DIM] / acc_ref[0:r, HEAD_DIM:]
             - lam * (acc_ref[t:t + r, :HEAD_DIM] / acc_ref[t:t + r, HEAD_DIM:]))
        out_ref[0] = (_rmsnorm(o, sg_ref[...]) * (1.0 - lam_init)).astype(out_ref.dtype)

    km = km_ref[0]
    vm = vm_ref[0]

    @pl.when(i == 0)
    def _():
        qm = _stack_subheads(qm_ref[0])
        for half in range(2):
            rows = slice(half * t, half * t + N_META)
            _init_state(m_ref.at[rows], acc_ref.at[rows])
        sm = _causal_mask(_scores(qm, km))
        for half in range(2):
            rows = slice(half * t, half * t + N_META)
            _online_softmax_step(sm[half * N_META:(half + 1) * N_META], vm,
                                 m_ref.at[rows], acc_ref.at[rows])
        finish(N_META, ybm_ref)

    def kv_tile(ref, j):
        return ref[0, pl.ds(pl.multiple_of(j * t, t), t), :]

    def step(cur_ref, nxt_ref, j, last=False):
        if nxt_ref is not None:
            nxt_ref[...] = _scores(qs_ref[...], kv_tile(k_ref, j + 1))
        if last:
            qn_ref[...] = _stack_subheads(qnext_ref[0])
            sc_ref[...] = _scores(qn_ref[...], kv_tile(k_ref, 0))
        s = _causal_mask(cur_ref[...]) if last else cur_ref[...]
        _online_softmax_step(s, kv_tile(v_ref, j), m_ref, acc_ref)

    @pl.when(i == 0)
    def _():
        qs_ref[...] = _stack_subheads(q_ref[0])
        sa_ref[...] = _scores(qs_ref[...], kv_tile(k_ref, 0))
        _first_softmax_step(_scores(qs_ref[...], km), vm, m_ref, acc_ref)
        step(sa_ref, None, 0, last=True)

    @pl.when(i > 0)
    def _():
        qs_ref[...] = qn_ref[...]
        _first_softmax_step(_scores(qs_ref[...], km), vm, m_ref, acc_ref)
        step(sc_ref, sa_ref, 0)

        @pl.loop(0, (i - 1) // 2)
        def _(jj):
            step(sa_ref, sb_ref, 2 * jj + 1)
            step(sb_ref, sa_ref, 2 * jj + 2)

        @pl.when(i % 2 == 0)
        def _():
            step(sa_ref, sb_ref, i - 1)
            step(sb_ref, None, i, last=True)

        @pl.when(i % 2 == 1)
        def _():
            step(sa_ref, None, i, last=True)

    finish(t, yb_ref)


def _diff_attn(q, k, v, qm, km, vm, lq1, lk1, lq2, lk2, subln_g, lam_init):
    b, seq, width = q.shape
    t = ATTN_TILE
    assert seq % t == 0 and width == N_HEADS * HEAD_DIM
    nq = seq // t
    qtile = pl.BlockSpec((1, t, HEAD_DIM), lambda bi, h, i: (bi, i, h))
    qnext = pl.BlockSpec((1, t, HEAD_DIM), lambda bi, h, i: (bi, jnp.minimum(i + 1, nq - 1), h))
    full = lambda w: pl.BlockSpec((1, seq, w), lambda bi, h, i: (bi, 0, h))
    mtile = lambda w: pl.BlockSpec((1, N_META, w), lambda bi, h, i: (bi, 0, h))
    return pl.pallas_call(
        functools.partial(_diff_attn_kernel, lam_init),
        grid=(b, N_HEADS, nq),
        in_specs=[qtile, qnext, full(HEAD_DIM), full(2 * HEAD_DIM), mtile(HEAD_DIM), mtile(HEAD_DIM), mtile(2 * HEAD_DIM),
                  _const_spec(lq1.shape), _const_spec(lk1.shape), _const_spec(lq2.shape),
                  _const_spec(lk2.shape), _const_spec(subln_g.shape)],
        out_specs=[qtile, mtile(HEAD_DIM)],
        out_shape=[jax.ShapeDtypeStruct((b, seq, width), BF16),
                   jax.ShapeDtypeStruct((b, N_META, width), BF16)],
        scratch_shapes=[pltpu.VMEM((2 * t, HEAD_DIM), BF16), pltpu.VMEM((2 * t, HEAD_DIM), BF16),
                        pltpu.VMEM((2 * t, LANES), F32), pltpu.VMEM((2 * t, 2 * HEAD_DIM), F32),
                        pltpu.VMEM((2 * t, t), F32), pltpu.VMEM((2 * t, t), F32), pltpu.VMEM((2 * t, t), F32)],
        compiler_params=pltpu.CompilerParams(
            dimension_semantics=("arbitrary", "arbitrary", "arbitrary"),
            vmem_limit_bytes=VMEM_LIMIT_BYTES),
        name="diff_attn",
    )(q, q, k, v, qm, km, vm, lq1, lk1, lq2, lk2, subln_g)


def _mix_out_kernel(x_ref, ya_ref, yb_ref, meta_ref, yam_ref, ybm_ref, wo_ref, g_ref,
                    wup_ref, cw_ref, cb_ref, wdown_ref, out_ref, next_ref, act_ref):
    tm = x_ref.shape[1]
    pool_w = ya_ref.shape[-1]
    d_ff = wdown_ref.shape[0]

    def mix_residual(rows, ya, yb):
        h1 = (rows + jnp.dot(ya, wo_ref[:pool_w, :], preferred_element_type=F32)
              + jnp.dot(yb, wo_ref[pool_w:, :], preferred_element_type=F32))
        return h1, _rmsnorm(h1, g_ref[...]).astype(BF16)

    @pl.when(pl.program_id(1) == 0)
    def _():
        _, nm = mix_residual(meta_ref[...], yam_ref[0], ybm_ref[0])
        next_ref[0:HALO, :] = nm

    h1, n2 = mix_residual(x_ref[0], ya_ref[0], yb_ref[0])
    next_ref[HALO:HALO + tm, :] = n2

    def conv(u, cols):
        out = cb_ref[:, cols]
        for tap in range(CONV_WIDTH):
            off = HALO - (CONV_WIDTH - 1) + tap
            out = out + u[off:off + tm, :] * cw_ref[tap:tap + 1, cols]
        return out

    for c in range(d_ff // FF_CHUNK):
        gcols = slice(c * FF_CHUNK, (c + 1) * FF_CHUNK)
        vcols = slice(d_ff + c * FF_CHUNK, d_ff + (c + 1) * FF_CHUNK)
        ug = jnp.dot(next_ref[...], wup_ref[:, gcols], preferred_element_type=F32)
        uv = jnp.dot(next_ref[...], wup_ref[:, vcols], preferred_element_type=F32)
        gate = conv(ug, gcols)
        act_ref[:, gcols] = (gate * jax.nn.sigmoid(gate) * conv(uv, vcols)).astype(BF16)
    out_ref[0] = h1 + jnp.dot(act_ref[...], wdown_ref[...], preferred_element_type=F32)
    next_ref[0:HALO, :] = next_ref[tm:tm + HALO, :]


def _mix_out(x, ya, yb, meta, yam, ybm, w_out, g, w_up, conv_w, conv_b, w_down):
    b, seq, d = x.shape
    pool_w = ya.shape[-1]
    tm = SEQ_TILE
    assert seq % tm == 0 and w_down.shape[0] % FF_CHUNK == 0
    tile = lambda w: pl.BlockSpec((1, tm, w), lambda bi, i: (bi, i, 0))
    mtile = lambda w: pl.BlockSpec((1, N_META, w), lambda bi, i: (bi, 0, 0))
    return pl.pallas_call(
        _mix_out_kernel,
        grid=(b, seq // tm),
        in_specs=[tile(d), tile(pool_w), tile(yb.shape[-1]), _const_spec(meta.shape),
                  mtile(pool_w), mtile(ybm.shape[-1]), _const_spec(w_out.shape), _const_spec(g.shape),
                  _const_spec(w_up.shape), _const_spec(conv_w.shape), _const_spec(conv_b.shape),
                  _const_spec(w_down.shape)],
        out_specs=tile(d),
        out_shape=jax.ShapeDtypeStruct((b, seq, d), x.dtype),
        scratch_shapes=[pltpu.VMEM((HALO + tm, d), BF16), pltpu.VMEM((tm, w_down.shape[0]), BF16)],
        compiler_params=pltpu.CompilerParams(
            dimension_semantics=("arbitrary", "arbitrary"), vmem_limit_bytes=VMEM_LIMIT_BYTES),
        name="mix_out",
    )(x, ya, yb, meta, yam, ybm, w_out, g, w_up, conv_w, conv_b, w_down)


def kernel(x, meta_tokens, norm_mix_g, w_in, w_pool, b_pool, pool_scale, q_norm_g, k_norm_g,
           lambda_q1, lambda_k1, lambda_q2, lambda_k2, subln_g, w_out, norm_ffn_g,
           w_up, conv_w, conv_b, w_down):
    depth = w_in.shape[0]
    assert depth == 1 and meta_tokens.shape[0] == N_META
    row = lambda a: a.reshape(1, -1).astype(F32)
    h = x
    meta = meta_tokens.astype(x.dtype)
    for i in range(depth):
        lam_init = 0.8 - 0.6 * math.exp(-0.3 * i)
        qg = jnp.tile(row(q_norm_g[i]), (1, 2)) * (QK_DIM ** -0.5 * math.log2(math.e))
        kg = jnp.tile(row(k_norm_g[i]), (1, 2))
        ya, q, k, v, yam, qm, km, vm = _mix_in(
            h, meta, row(norm_mix_g[i]), w_in[i].astype(BF16), w_pool[i].astype(BF16),
            row(b_pool[i]), row(pool_scale[i]), qg, kg)
        yb, ybm = _diff_attn(q, k, v, qm, km, vm, row(lambda_q1[i]), row(lambda_k1[i]),
                             row(lambda_q2[i]), row(lambda_k2[i]), row(subln_g[i]), lam_init)
        h = _mix_out(h, ya, yb, meta, yam, ybm, w_out[i].astype(BF16), row(norm_ffn_g[i]),
                     w_up[i].astype(BF16), conv_w[i].astype(F32), row(conv_b[i]), w_down[i].astype(BF16))
    return h
```

```python
import functools
import math

import jax
import jax.numpy as jnp
from jax import lax
from jax.experimental import pallas as pl
from jax.experimental.pallas import tpu as pltpu

F32 = jnp.float32
BF16 = jnp.bfloat16

N_META = 16
POOL_WINDOWS = (2, 4, 8, 16)
N_HEADS = 4
HEAD_DIM = 128
QK_DIM = 64
CONV_WIDTH = 3
EPS = 1e-6

LANES = 128
SUBLANES = 8
HALO = 16
VMEM_LIMIT_BYTES = 56 * 1024 * 1024

SEQ_TILE = 512
ATTN_TILE = 512
FF_CHUNK = 256


def _rmsnorm(x, g):
    return x * lax.rsqrt(jnp.mean(x * x, axis=-1, keepdims=True) + EPS) * g


def _const_spec(shape):
    nd = len(shape)
    return pl.BlockSpec(shape, lambda *_: (0,) * nd, pipeline_mode=pl.Buffered(1))


def _mix_in_kernel(x_ref, meta_ref, g_ref, win_ref, wpool_ref, bpool_ref, pscale_ref,
                   qg_ref, kg_ref,
                   ya_ref, q_ref, k_ref, v_ref, yam_ref, qm_ref, km_ref, vm_ref,
                   uext_ref):
    pool_w = bpool_ref.shape[-1]
    qk_w = q_ref.shape[-1]

    def heads(rows, is_meta, ya_out, q_out, k_out, v_out):
        r = rows.shape[0]
        n = _rmsnorm(rows, g_ref[...])
        proj = jnp.dot(n.astype(BF16), win_ref[...], preferred_element_type=F32)
        u = proj[:, :pool_w]
        uext_ref[HALO:HALO + r, :] = u
        for gi, w in enumerate(POOL_WINDOWS):
            cols = slice(gi * LANES, (gi + 1) * LANES)
            win_sum = u[:, cols]
            for j in range(1, w):
                win_sum = win_sum + uext_ref[HALO - j:HALO - j + r, cols]
            if is_meta:
                pos = lax.broadcasted_iota(jnp.int32, (r, 1), 0)
                mean = win_sum / jnp.minimum(pos + 1, w).astype(F32)
            else:
                mean = win_sum * (1.0 / w)
            pooled = mean - u[:, cols]
            y = jnp.dot(pooled.astype(BF16), wpool_ref[gi], preferred_element_type=F32)
            y = (y + bpool_ref[:, cols]) * pscale_ref[:, cols]
            ya_out[0, :, cols] = y.astype(ya_out.dtype)
        uext_ref[0:HALO, :] = uext_ref[r:r + HALO, :]

        lane = lax.broadcasted_iota(jnp.int32, (r, LANES), 1)
        lo = lane < QK_DIM
        for src, gain_ref, out in ((proj[:, pool_w:pool_w + qk_w], qg_ref, q_out),
                                   (proj[:, pool_w + qk_w:pool_w + 2 * qk_w], kg_ref, k_out)):
            for h in range(N_HEADS):
                cols = slice(h * LANES, (h + 1) * LANES)
                slab = src[:, cols]
                sq = slab * slab
                s_lo = jnp.sum(jnp.where(lo, sq, 0.0), axis=-1, keepdims=True)
                s_hi = jnp.sum(jnp.where(lo, 0.0, sq), axis=-1, keepdims=True)
                ms = jnp.where(lo, s_lo, s_hi) * (1.0 / QK_DIM)
                out[0, :, cols] = (slab * lax.rsqrt(ms + EPS) * gain_ref[...]).astype(out.dtype)
        for h in range(N_HEADS):
            vcols = slice(pool_w + 2 * qk_w + h * HEAD_DIM, pool_w + 2 * qk_w + (h + 1) * HEAD_DIM)
            v_out[0, :, 2 * h * HEAD_DIM:(2 * h + 1) * HEAD_DIM] = proj[:, vcols].astype(v_out.dtype)
            v_out[0, :, (2 * h + 1) * HEAD_DIM:(2 * h + 2) * HEAD_DIM] = jnp.ones((r, HEAD_DIM), v_out.dtype)

    @pl.when(pl.program_id(1) == 0)
    def _():
        uext_ref[0:HALO, :] = jnp.zeros((HALO, pool_w), F32)
        heads(meta_ref[...], True, yam_ref, qm_ref, km_ref, vm_ref)

    heads(x_ref[0], False, ya_ref, q_ref, k_ref, v_ref)


def _mix_in(x, meta, g, w_in, w_pool, b_pool, pool_scale, qg, kg):
    b, seq, d = x.shape
    pool_w = b_pool.shape[-1]
    qk_w = N_HEADS * HEAD_DIM
    tm = SEQ_TILE
    assert seq % tm == 0
    tile = lambda w: pl.BlockSpec((1, tm, w), lambda bi, i: (bi, i, 0))
    mtile = lambda w: pl.BlockSpec((1, N_META, w), lambda bi, i: (bi, 0, 0))
    big = lambda w: jax.ShapeDtypeStruct((b, seq, w), BF16)
    small = lambda w: jax.ShapeDtypeStruct((b, N_META, w), BF16)
    return pl.pallas_call(
        _mix_in_kernel,
        grid=(b, seq // tm),
        in_specs=[tile(d), _const_spec(meta.shape), _const_spec(g.shape), _const_spec(w_in.shape),
                  _const_spec(w_pool.shape), _const_spec(b_pool.shape), _const_spec(pool_scale.shape),
                  _const_spec(qg.shape), _const_spec(kg.shape)],
        out_specs=[tile(pool_w), tile(qk_w), tile(qk_w), tile(2 * qk_w),
                   mtile(pool_w), mtile(qk_w), mtile(qk_w), mtile(2 * qk_w)],
        out_shape=[big(pool_w), big(qk_w), big(qk_w), big(2 * qk_w),
                   small(pool_w), small(qk_w), small(qk_w), small(2 * qk_w)],
        scratch_shapes=[pltpu.VMEM((HALO + tm, pool_w), F32)],
        compiler_params=pltpu.CompilerParams(
            dimension_semantics=("arbitrary", "arbitrary"), vmem_limit_bytes=VMEM_LIMIT_BYTES),
        name="mix_in",
    )(x, meta, g, w_in, w_pool, b_pool, pool_scale, qg, kg)


def _stack_subheads(q):
    lane = lax.broadcasted_iota(jnp.int32, q.shape, 1)
    zero = jnp.zeros_like(q)
    return jnp.concatenate([jnp.where(lane < QK_DIM, q, zero), jnp.where(lane < QK_DIM, zero, q)], axis=0)


def _scores(qs, k):
    return lax.dot_general(qs, k, (((1,), (1,)), ((), ())), preferred_element_type=F32)


def _causal_mask(s):
    r = s.shape[0] // 2
    row = lax.broadcasted_iota(jnp.int32, s.shape, 0)
    row = jnp.where(row >= r, row - r, row)
    col = lax.broadcasted_iota(jnp.int32, s.shape, 1)
    return jnp.where(col <= row, s, -jnp.inf)


def _online_softmax_step(s, v_ext, m_ref, acc_ref):
    width = s.shape[1]
    m = m_ref[...]
    m_new = jnp.maximum(m, jnp.max(s, axis=-1, keepdims=True))
    alpha = jnp.exp2(m - m_new)
    m_wide = jnp.concatenate([m_new] * (width // LANES), axis=1) if width > LANES else m_new[:, :width]
    p = jnp.exp2(s - m_wide)
    acc_ref[...] = (jnp.concatenate([alpha, alpha], axis=1) * acc_ref[...]
                    + jnp.dot(p.astype(v_ext.dtype), v_ext, preferred_element_type=F32))
    m_ref[...] = m_new


def _init_state(m_ref, acc_ref):
    m_ref[...] = jnp.full(m_ref.shape, -jnp.inf, F32)
    acc_ref[...] = jnp.zeros(acc_ref.shape, F32)


def _first_softmax_step(s, v_ext, m_ref, acc_ref):
    m = jnp.max(s, axis=-1, keepdims=True)
    p = jnp.exp2(s - m)
    m_ref[...] = jnp.broadcast_to(m, m_ref.shape)
    acc_ref[...] = jnp.dot(p.astype(v_ext.dtype), v_ext, preferred_element_type=F32)


def _diff_attn_kernel(lam_init, q_ref, qnext_ref, k_ref, v_ref, qm_ref, km_ref, vm_ref,
                      lq1_ref, lk1_ref, lq2_ref, lk2_ref, sg_ref, yb_ref, ybm_ref,
                      qs_ref, qn_ref, m_ref, acc_ref, sa_ref, sb_ref, sc_ref):
    i = pl.program_id(2)
    t = q_ref.shape[1]
    lam = (jnp.exp(jnp.sum(lq1_ref[...] * lk1_ref[...], axis=-1, keepdims=True))
           - jnp.exp(jnp.sum(lq2_ref[...] * lk2_ref[...], axis=-1, keepdims=True)) + lam_init)

    def finish(r, out_ref):
        o = (acc_ref[0:r, :HEAD_DIM] / acc_ref[0:r, HEAD_DIM:]
             - lam * (acc_ref[t:t + r, :HEAD_DIM] / acc_ref[t:t + r, HEAD_DIM:]))
        out_ref[0] = (_rmsnorm(o, sg_ref[...]) * (1.0 - lam_init)).astype(out_ref.dtype)

    km = km_ref[0]
    vm = vm_ref[0]

    @pl.when(i == 0)
    def _():
        qm = _stack_subheads(qm_ref[0])
        for half in range(2):
            rows = slice(half * t, half * t + N_META)
            _init_state(m_ref.at[rows], acc_ref.at[rows])
        sm = _causal_mask(_scores(qm, km))
        for half in range(2):
            rows = slice(half * t, half * t + N_META)
            _online_softmax_step(sm[half * N_META:(half + 1) * N_META], vm,
                                 m_ref.at[rows], acc_ref.at[rows])
        finish(N_META, ybm_ref)

    def kv_tile(ref, j):
        return ref[0, pl.ds(pl.multiple_of(j * t, t), t), :]

    def step(cur_ref, nxt_ref, j, last=False):
        if nxt_ref is not None:
            nxt_ref[...] = _scores(qs_ref[...], kv_tile(k_ref, j + 1))
        if last:
            qn_ref[...] = _stack_subheads(qnext_ref[0])
            sc_ref[...] = _scores(qn_ref[...], kv_tile(k_ref, 0))
        s = _causal_mask(cur_ref[...]) if last else cur_ref[...]
        _online_softmax_step(s, kv_tile(v_ref, j), m_ref, acc_ref)

    @pl.when(i == 0)
    def _():
        qs_ref[...] = _stack_subheads(q_ref[0])
        sa_ref[...] = _scores(qs_ref[...], kv_tile(k_ref, 0))
        _first_softmax_step(_scores(qs_ref[...], km), vm, m_ref, acc_ref)
        step(sa_ref, None, 0, last=True)

    @pl.when(i > 0)
    def _():
        qs_ref[...] = qn_ref[...]
        _first_softmax_step(_scores(qs_ref[...], km), vm, m_ref, acc_ref)
        step(sc_ref, sa_ref, 0)

        @pl.loop(0, (i - 1) // 2)
        def _(jj):
            step(sa_ref, sb_ref, 2 * jj + 1)
            step(sb_ref, sa_ref, 2 * jj + 2)

        @pl.when(i % 2 == 0)
        def _():
            step(sa_ref, sb_ref, i - 1)
            step(sb_ref, None, i, last=True)

        @pl.when(i % 2 == 1)
        def _():
            step(sa_ref, None, i, last=True)

    finish(t, yb_ref)


def _diff_attn(q, k, v, qm, km, vm, lq1, lk1, lq2, lk2, subln_g, lam_init):
    b, seq, width = q.shape
    t = ATTN_TILE
    assert seq % t == 0 and width == N_HEADS * HEAD_DIM
    nq = seq // t
    qtile = pl.BlockSpec((1, t, HEAD_DIM), lambda bi, h, i: (bi, i, h))
    qnext = pl.BlockSpec((1, t, HEAD_DIM), lambda bi, h, i: (bi, jnp.minimum(i + 1, nq - 1), h))
    full = lambda w: pl.BlockSpec((1, seq, w), lambda bi, h, i: (bi, 0, h))
    mtile = lambda w: pl.BlockSpec((1, N_META, w), lambda bi, h, i: (bi, 0, h))
    return pl.pallas_call(
        functools.partial(_diff_attn_kernel, lam_init),
        grid=(b, N_HEADS, nq),
        in_specs=[qtile, qnext, full(HEAD_DIM), full(2 * HEAD_DIM), mtile(HEAD_DIM), mtile(HEAD_DIM), mtile(2 * HEAD_DIM),
                  _const_spec(lq1.shape), _const_spec(lk1.shape), _const_spec(lq2.shape),
                  _const_spec(lk2.shape), _const_spec(subln_g.shape)],
        out_specs=[qtile, mtile(HEAD_DIM)],
        out_shape=[jax.ShapeDtypeStruct((b, seq, width), BF16),
                   jax.ShapeDtypeStruct((b, N_META, width), BF16)],
        scratch_shapes=[pltpu.VMEM((2 * t, HEAD_DIM), BF16), pltpu.VMEM((2 * t, HEAD_DIM), BF16),
                        pltpu.VMEM((2 * t, LANES), F32), pltpu.VMEM((2 * t, 2 * HEAD_DIM), F32),
                        pltpu.VMEM((2 * t, t), F32), pltpu.VMEM((2 * t, t), F32), pltpu.VMEM((2 * t, t), F32)],
        compiler_params=pltpu.CompilerParams(
            dimension_semantics=("arbitrary", "arbitrary", "arbitrary"),
            vmem_limit_bytes=VMEM_LIMIT_BYTES),
        name="diff_attn",
    )(q, q, k, v, qm, km, vm, lq1, lk1, lq2, lk2, subln_g)


def _mix_out_kernel(x_ref, ya_ref, yb_ref, meta_ref, yam_ref, ybm_ref, wo_ref, g_ref,
                    wup_ref, cw_ref, cb_ref, wdown_ref, out_ref, nperm_ref, next_ref, act_ref, operm_ref):
    tm = x_ref.shape[1]
    d = x_ref.shape[2]
    pool_w = ya_ref.shape[-1]
    d_ff = wdown_ref.shape[0]
    n_slabs = d // LANES
    nv = tm // SUBLANES

    def mix_residual(rows, ya, yb):
        h1 = (rows + jnp.dot(ya, wo_ref[:pool_w, :], preferred_element_type=F32)
              + jnp.dot(yb, wo_ref[pool_w:, :], preferred_element_type=F32))
        return h1, _rmsnorm(h1, g_ref[...])

    @pl.when(pl.program_id(1) == 0)
    def _():
        _, nm = mix_residual(meta_ref[...], yam_ref[0], ybm_ref[0])
        halo = jnp.concatenate([nm[N_META - 9:N_META - 1], nm[N_META - 8:]], axis=0)
        for c in range(n_slabs):
            nperm_ref[c, 0:HALO, :] = halo[:, c * LANES:(c + 1) * LANES]

    h1, n2 = mix_residual(x_ref[0], ya_ref[0], yb_ref[0])
    for s in range(SUBLANES):
        for v0 in range(0, nv, SUBLANES):
            t0 = s * nv + v0
            for c in range(n_slabs):
                nperm_ref[c, pl.ds(HALO + v0 * SUBLANES + s, SUBLANES, stride=SUBLANES), :] = (
                    n2[t0:t0 + SUBLANES, c * LANES:(c + 1) * LANES])
    next_ref[...] = jnp.concatenate([nperm_ref[c] for c in range(n_slabs)], axis=1).astype(BF16)

    def conv(u, cols):
        cur = u[HALO:]
        wrap1 = jnp.concatenate([u[HALO - 1:HALO], u[HALO + tm - 8:HALO + tm - 1]], axis=0)
        wrap2 = jnp.concatenate([u[HALO - 9:HALO - 8], u[HALO + tm - 16:HALO + tm - 9]], axis=0)
        back1 = jnp.concatenate([wrap1, u[HALO:HALO + tm - 8]], axis=0)
        back2 = jnp.concatenate([wrap2, wrap1, u[HALO:HALO + tm - 16]], axis=0)
        return (cb_ref[:, cols] + back2 * cw_ref[0:1, cols] + back1 * cw_ref[1:2, cols]
                + cur * cw_ref[2:3, cols])

    for c in range(d_ff // FF_CHUNK):
        gcols = slice(c * FF_CHUNK, (c + 1) * FF_CHUNK)
        vcols = slice(d_ff + c * FF_CHUNK, d_ff + (c + 1) * FF_CHUNK)
        ug = jnp.dot(next_ref[...], wup_ref[:, gcols], preferred_element_type=F32)
        uv = jnp.dot(next_ref[...], wup_ref[:, vcols], preferred_element_type=F32)
        gate = conv(ug, gcols)
        act_ref[:, gcols] = (gate * jax.nn.sigmoid(gate) * conv(uv, vcols)).astype(BF16)
    ffn = jnp.dot(act_ref[...], wdown_ref[...], preferred_element_type=F32)
    for c in range(n_slabs):
        operm_ref[c] = ffn[:, c * LANES:(c + 1) * LANES]
        nperm_ref[c, 0:HALO, :] = nperm_ref[c, tm:tm + HALO, :]
    for s in range(SUBLANES):
        for v0 in range(0, nv, SUBLANES):
            t0 = s * nv + v0
            for c in range(n_slabs):
                lanes = slice(c * LANES, (c + 1) * LANES)
                out_ref[0, t0:t0 + SUBLANES, lanes] = (
                    h1[t0:t0 + SUBLANES, lanes]
                    + operm_ref[c, pl.ds(v0 * SUBLANES + s, SUBLANES, stride=SUBLANES), :])


def _mix_out(x, ya, yb, meta, yam, ybm, w_out, g, w_up, conv_w, conv_b, w_down):
    b, seq, d = x.shape
    pool_w = ya.shape[-1]
    tm = SEQ_TILE
    assert seq % tm == 0 and w_down.shape[0] % FF_CHUNK == 0
    tile = lambda w: pl.BlockSpec((1, tm, w), lambda bi, i: (bi, i, 0))
    mtile = lambda w: pl.BlockSpec((1, N_META, w), lambda bi, i: (bi, 0, 0))
    return pl.pallas_call(
        _mix_out_kernel,
        grid=(b, seq // tm),
        in_specs=[tile(d), tile(pool_w), tile(yb.shape[-1]), _const_spec(meta.shape),
                  mtile(pool_w), mtile(ybm.shape[-1]), _const_spec(w_out.shape), _const_spec(g.shape),
                  _const_spec(w_up.shape), _const_spec(conv_w.shape), _const_spec(conv_b.shape),
                  _const_spec(w_down.shape)],
        out_specs=tile(d),
        out_shape=jax.ShapeDtypeStruct((b, seq, d), x.dtype),
        scratch_shapes=[pltpu.VMEM((d // LANES, HALO + tm, LANES), F32), pltpu.VMEM((HALO + tm, d), BF16),
                        pltpu.VMEM((tm, w_down.shape[0]), BF16), pltpu.VMEM((d // LANES, tm, LANES), F32)],
        compiler_params=pltpu.CompilerParams(
            dimension_semantics=("arbitrary", "arbitrary"), vmem_limit_bytes=VMEM_LIMIT_BYTES),
        name="mix_out",
    )(x, ya, yb, meta, yam, ybm, w_out, g, w_up, conv_w, conv_b, w_down)


def kernel(x, meta_tokens, norm_mix_g, w_in, w_pool, b_pool, pool_scale, q_norm_g, k_norm_g,
           lambda_q1, lambda_k1, lambda_q2, lambda_k2, subln_g, w_out, norm_ffn_g,
           w_up, conv_w, conv_b, w_down):
    depth = w_in.shape[0]
    assert depth == 1 and meta_tokens.shape[0] == N_META
    row = lambda a: a.reshape(1, -1).astype(F32)
    h = x
    meta = meta_tokens.astype(x.dtype)
    for i in range(depth):
        lam_init = 0.8 - 0.6 * math.exp(-0.3 * i)
        qg = jnp.tile(row(q_norm_g[i]), (1, 2)) * (QK_DIM ** -0.5 * math.log2(math.e))
        kg = jnp.tile(row(k_norm_g[i]), (1, 2))
        ya, q, k, v, yam, qm, km, vm = _mix_in(
            h, meta, row(norm_mix_g[i]), w_in[i].astype(BF16), w_pool[i].astype(BF16),
            row(b_pool[i]), row(pool_scale[i]), qg, kg)
        yb, ybm = _diff_attn(q, k, v, qm, km, vm, row(lambda_q1[i]), row(lambda_k1[i]),
                             row(lambda_q2[i]), row(lambda_k2[i]), row(subln_g[i]), lam_init)
        h = _mix_out(h, ya, yb, meta, yam, ybm, w_out[i].astype(BF16), row(norm_ffn_g[i]),
                     w_up[i].astype(BF16), conv_w[i].astype(F32), row(conv_b[i]), w_down[i].astype(BF16))
    return h
```

```python
import functools
import math

import jax
import jax.numpy as jnp
from jax import lax
from jax.experimental import pallas as pl
from jax.experimental.pallas import tpu as pltpu

F32 = jnp.float32
BF16 = jnp.bfloat16

N_META = 16
POOL_WINDOWS = (2, 4, 8, 16)
N_HEADS = 4
HEAD_DIM = 128
QK_DIM = 64
CONV_WIDTH = 3
EPS = 1e-6

LANES = 128
SUBLANES = 8
HALO = 16
VMEM_LIMIT_BYTES = 56 * 1024 * 1024

SEQ_TILE = 512
ATTN_TILE = 512
FF_CHUNK = 256


def _rmsnorm(x, g):
    return x * lax.rsqrt(jnp.mean(x * x, axis=-1, keepdims=True) + EPS) * g


def _const_spec(shape):
    nd = len(shape)
    return pl.BlockSpec(shape, lambda *_: (0,) * nd, pipeline_mode=pl.Buffered(1))


def _mix_in_kernel(x_ref, meta_ref, g_ref, win_ref, wpool_ref, bpool_ref, pscale_ref,
                   qg_ref, kg_ref,
                   ya_ref, q_ref, k_ref, v_ref, yam_ref, qm_ref, km_ref, vm_ref,
                   uext_ref):
    pool_w = bpool_ref.shape[-1]
    qk_w = q_ref.shape[-1]

    def heads(rows, is_meta, ya_out, q_out, k_out, v_out):
        r = rows.shape[0]
        n = _rmsnorm(rows, g_ref[...])
        proj = jnp.dot(n.astype(BF16), win_ref[...], preferred_element_type=F32)
        u = proj[:, :pool_w]
        uext_ref[HALO:HALO + r, :] = u
        for gi, w in enumerate(POOL_WINDOWS):
            cols = slice(gi * LANES, (gi + 1) * LANES)
            win_sum = u[:, cols]
            for j in range(1, w):
                win_sum = win_sum + uext_ref[HALO - j:HALO - j + r, cols]
            if is_meta:
                pos = lax.broadcasted_iota(jnp.int32, (r, 1), 0)
                mean = win_sum / jnp.minimum(pos + 1, w).astype(F32)
            else:
                mean = win_sum * (1.0 / w)
            pooled = mean - u[:, cols]
            y = jnp.dot(pooled.astype(BF16), wpool_ref[gi], preferred_element_type=F32)
            y = (y + bpool_ref[:, cols]) * pscale_ref[:, cols]
            ya_out[0, :, cols] = y.astype(ya_out.dtype)
        uext_ref[0:HALO, :] = uext_ref[r:r + HALO, :]

        lane = lax.broadcasted_iota(jnp.int32, (r, LANES), 1)
        lo = lane < QK_DIM
        for src, gain_ref, out in ((proj[:, pool_w:pool_w + qk_w], qg_ref, q_out),
                                   (proj[:, pool_w + qk_w:pool_w + 2 * qk_w], kg_ref, k_out)):
            for h in range(N_HEADS):
                cols = slice(h * LANES, (h + 1) * LANES)
                slab = src[:, cols]
                sq = slab * slab
                s_lo = jnp.sum(jnp.where(lo, sq, 0.0), axis=-1, keepdims=True)
                s_hi = jnp.sum(jnp.where(lo, 0.0, sq), axis=-1, keepdims=True)
                ms = jnp.where(lo, s_lo, s_hi) * (1.0 / QK_DIM)
                out[0, :, cols] = (slab * lax.rsqrt(ms + EPS) * gain_ref[...]).astype(out.dtype)
        for h in range(N_HEADS):
            vcols = slice(pool_w + 2 * qk_w + h * HEAD_DIM, pool_w + 2 * qk_w + (h + 1) * HEAD_DIM)
            v_out[0, :, 2 * h * HEAD_DIM:(2 * h + 1) * HEAD_DIM] = proj[:, vcols].astype(v_out.dtype)
            v_out[0, :, (2 * h + 1) * HEAD_DIM:(2 * h + 2) * HEAD_DIM] = jnp.ones((r, HEAD_DIM), v_out.dtype)

    @pl.when(pl.program_id(1) == 0)
    def _():
        uext_ref[0:HALO, :] = jnp.zeros((HALO, pool_w), F32)
        heads(meta_ref[...], True, yam_ref, qm_ref, km_ref, vm_ref)

    heads(x_ref[0], False, ya_ref, q_ref, k_ref, v_ref)


def _mix_in(x, meta, g, w_in, w_pool, b_pool, pool_scale, qg, kg):
    b, seq, d = x.shape
    pool_w = b_pool.shape[-1]
    qk_w = N_HEADS * HEAD_DIM
    tm = SEQ_TILE
    assert seq % tm == 0
    tile = lambda w: pl.BlockSpec((1, tm, w), lambda bi, i: (bi, i, 0))
    mtile = lambda w: pl.BlockSpec((1, N_META, w), lambda bi, i: (bi, 0, 0))
    big = lambda w: jax.ShapeDtypeStruct((b, seq, w), BF16)
    small = lambda w: jax.ShapeDtypeStruct((b, N_META, w), BF16)
    return pl.pallas_call(
        _mix_in_kernel,
        grid=(b, seq // tm),
        in_specs=[tile(d), _const_spec(meta.shape), _const_spec(g.shape), _const_spec(w_in.shape),
                  _const_spec(w_pool.shape), _const_spec(b_pool.shape), _const_spec(pool_scale.shape),
                  _const_spec(qg.shape), _const_spec(kg.shape)],
        out_specs=[tile(pool_w), tile(qk_w), tile(qk_w), tile(2 * qk_w),
                   mtile(pool_w), mtile(qk_w), mtile(qk_w), mtile(2 * qk_w)],
        out_shape=[big(pool_w), big(qk_w), big(qk_w), big(2 * qk_w),
                   small(pool_w), small(qk_w), small(qk_w), small(2 * qk_w)],
        scratch_shapes=[pltpu.VMEM((HALO + tm, pool_w), F32)],
        compiler_params=pltpu.CompilerParams(
            dimension_semantics=("arbitrary", "arbitrary"), vmem_limit_bytes=VMEM_LIMIT_BYTES),
        name="mix_in",
    )(x, meta, g, w_in, w_pool, b_pool, pool_scale, qg, kg)


def _stack_subheads(q):
    lane = lax.broadcasted_iota(jnp.int32, q.shape, 1)
    zero = jnp.zeros_like(q)
    return jnp.concatenate([jnp.where(lane < QK_DIM, q, zero), jnp.where(lane < QK_DIM, zero, q)], axis=0)


def _scores(qs, k):
    return lax.dot_general(qs, k, (((1,), (1,)), ((), ())), preferred_element_type=F32)


def _causal_mask(s):
    r = s.shape[0] // 2
    row = lax.broadcasted_iota(jnp.int32, s.shape, 0)
    row = jnp.where(row >= r, row - r, row)
    col = lax.broadcasted_iota(jnp.int32, s.shape, 1)
    return jnp.where(col <= row, s, -jnp.inf)


def _online_softmax_step(s, v_ext, m_ref, acc_ref):
    width = s.shape[1]
    m = m_ref[...]
    m_new = jnp.maximum(m, jnp.max(s, axis=-1, keepdims=True))
    alpha = jnp.exp2(m - m_new)
    m_wide = jnp.concatenate([m_new] * (width // LANES), axis=1) if width > LANES else m_new[:, :width]
    p = jnp.exp2(s - m_wide)
    acc_ref[...] = (jnp.concatenate([alpha, alpha], axis=1) * acc_ref[...]
                    + jnp.dot(p.astype(v_ext.dtype), v_ext, preferred_element_type=F32))
    m_ref[...] = m_new


def _init_state(m_ref, acc_ref):
    m_ref[...] = jnp.full(m_ref.shape, -jnp.inf, F32)
    acc_ref[...] = jnp.zeros(acc_ref.shape, F32)


def _first_softmax_step(s, v_ext, m_ref, acc_ref):
    m = jnp.max(s, axis=-1, keepdims=True)
    p = jnp.exp2(s - m)
    m_ref[...] = jnp.broadcast_to(m, m_ref.shape)
    acc_ref[...] = jnp.dot(p.astype(v_ext.dtype), v_ext, preferred_element_type=F32)


def _diff_attn_kernel(lam_init, q_ref, qnext_ref, k_ref, v_ref, qm_ref, km_ref, vm_ref,
                      lq1_ref, lk1_ref, lq2_ref, lk2_ref, sg_ref, yb_ref, ybm_ref,
                      qs_ref, qn_ref, m_ref, acc_ref, sa_ref, sb_ref, sc_ref):
    i = pl.program_id(2)
    t = q_ref.shape[1]
    lam = (jnp.exp(jnp.sum(lq1_ref[...] * lk1_ref[...], axis=-1, keepdims=True))
           - jnp.exp(jnp.sum(lq2_ref[...] * lk2_ref[...], axis=-1, keepdims=True)) + lam_init)

    def finish(r, out_ref):
        o = (acc_ref[0:r, :HEAD_DIM] / acc_ref[0:r, HEAD_DIM:]
             - lam * (acc_ref[t:t + r, :HEAD_DIM] / acc_ref[t:t + r, HEAD_DIM:]))
        out_ref[0] = (_rmsnorm(o, sg_ref[...]) * (1.0 - lam_init)).astype(out_ref.dtype)

    km = km_ref[0]
    vm = vm_ref[0]

    @pl.when(i == 0)
    def _():
        qm = _stack_subheads(qm_ref[0])
        for half in range(2):
            rows = slice(half * t, half * t + N_META)
            _init_state(m_ref.at[rows], acc_ref.at[rows])
        sm = _causal_mask(_scores(qm, km))
        for half in range(2):
            rows = slice(half * t, half * t + N_META)
            _online_softmax_step(sm[half * N_META:(half + 1) * N_META], vm,
                                 m_ref.at[rows], acc_ref.at[rows])
        finish(N_META, ybm_ref)

    def kv_tile(ref, j):
        return ref[0, pl.ds(pl.multiple_of(j * t, t), t), :]

    def step(cur_ref, nxt_ref, j, last=False):
        if nxt_ref is not None:
            nxt_ref[...] = _scores(qs_ref[...], kv_tile(k_ref, j + 1))
        if last:
            qn_ref[...] = _stack_subheads(qnext_ref[0])
            sc_ref[...] = _scores(qn_ref[...], kv_tile(k_ref, 0))
        s = _causal_mask(cur_ref[...]) if last else cur_ref[...]
        _online_softmax_step(s, kv_tile(v_ref, j), m_ref, acc_ref)

    @pl.when(i == 0)
    def _():
        qs_ref[...] = _stack_subheads(q_ref[0])
        sa_ref[...] = _scores(qs_ref[...], kv_tile(k_ref, 0))
        _first_softmax_step(_scores(qs_ref[...], km), vm, m_ref, acc_ref)
        step(sa_ref, None, 0, last=True)

    @pl.when(i > 0)
    def _():
        qs_ref[...] = qn_ref[...]
        _first_softmax_step(_scores(qs_ref[...], km), vm, m_ref, acc_ref)
        step(sc_ref, sa_ref, 0)

        n_quads = (i - 1) // 4

        @pl.loop(0, n_quads)
        def _(jj):
            step(sa_ref, sb_ref, 4 * jj + 1)
            step(sb_ref, sa_ref, 4 * jj + 2)
            step(sa_ref, sb_ref, 4 * jj + 3)
            step(sb_ref, sa_ref, 4 * jj + 4)

        @pl.when((i - 1) % 4 >= 2)
        def _():
            step(sa_ref, sb_ref, 4 * n_quads + 1)
            step(sb_ref, sa_ref, 4 * n_quads + 2)

        @pl.when(i % 2 == 0)
        def _():
            step(sa_ref, sb_ref, i - 1)
            step(sb_ref, None, i, last=True)

        @pl.when(i % 2 == 1)
        def _():
            step(sa_ref, None, i, last=True)

    finish(t, yb_ref)


def _diff_attn(q, k, v, qm, km, vm, lq1, lk1, lq2, lk2, subln_g, lam_init):
    b, seq, width = q.shape
    t = ATTN_TILE
    assert seq % t == 0 and width == N_HEADS * HEAD_DIM
    nq = seq // t
    qtile = pl.BlockSpec((1, t, HEAD_DIM), lambda bi, h, i: (bi, i, h))
    qnext = pl.BlockSpec((1, t, HEAD_DIM), lambda bi, h, i: (bi, jnp.minimum(i + 1, nq - 1), h))
    full = lambda w: pl.BlockSpec((1, seq, w), lambda bi, h, i: (bi, 0, h))
    mtile = lambda w: pl.BlockSpec((1, N_META, w), lambda bi, h, i: (bi, 0, h))
    return pl.pallas_call(
        functools.partial(_diff_attn_kernel, lam_init),
        grid=(b, N_HEADS, nq),
        in_specs=[qtile, qnext, full(HEAD_DIM), full(2 * HEAD_DIM), mtile(HEAD_DIM), mtile(HEAD_DIM), mtile(2 * HEAD_DIM),
                  _const_spec(lq1.shape), _const_spec(lk1.shape), _const_spec(lq2.shape),
                  _const_spec(lk2.shape), _const_spec(subln_g.shape)],
        out_specs=[qtile, mtile(HEAD_DIM)],
        out_shape=[jax.ShapeDtypeStruct((b, seq, width), BF16),
                   jax.ShapeDtypeStruct((b, N_META, width), BF16)],
        scratch_shapes=[pltpu.VMEM((2 * t, HEAD_DIM), BF16), pltpu.VMEM((2 * t, HEAD_DIM), BF16),
                        pltpu.VMEM((2 * t, LANES), F32), pltpu.VMEM((2 * t, 2 * HEAD_DIM), F32),
                        pltpu.VMEM((2 * t, t), F32), pltpu.VMEM((2 * t, t), F32), pltpu.VMEM((2 * t, t), F32)],
        compiler_params=pltpu.CompilerParams(
            dimension_semantics=("arbitrary", "arbitrary", "arbitrary"),
            vmem_limit_bytes=VMEM_LIMIT_BYTES),
        name="diff_attn",
    )(q, q, k, v, qm, km, vm, lq1, lk1, lq2, lk2, subln_g)


def _mix_out_kernel(x_ref, ya_ref, yb_ref, meta_ref, yam_ref, ybm_ref, wo_ref, g_ref,
                    wup_ref, cw_ref, cb_ref, wdown_ref, out_ref, nperm_ref, next_ref, act_ref, operm_ref):
    tm = x_ref.shape[1]
    d = x_ref.shape[2]
    pool_w = ya_ref.shape[-1]
    d_ff = wdown_ref.shape[0]
    n_slabs = d // LANES
    nv = tm // SUBLANES

    def mix_residual(rows, ya, yb):
        h1 = (rows + jnp.dot(ya, wo_ref[:pool_w, :], preferred_element_type=F32)
              + jnp.dot(yb, wo_ref[pool_w:, :], preferred_element_type=F32))
        return h1, _rmsnorm(h1, g_ref[...])

    @pl.when(pl.program_id(1) == 0)
    def _():
        _, nm = mix_residual(meta_ref[...], yam_ref[0], ybm_ref[0])
        halo = jnp.concatenate([nm[N_META - 9:N_META - 1], nm[N_META - 8:]], axis=0)
        for c in range(n_slabs):
            nperm_ref[c, 0:HALO, :] = halo[:, c * LANES:(c + 1) * LANES]

    h1, n2 = mix_residual(x_ref[0], ya_ref[0], yb_ref[0])
    for s in range(SUBLANES):
        for v0 in range(0, nv, SUBLANES):
            t0 = s * nv + v0
            for c in range(n_slabs):
                nperm_ref[c, pl.ds(HALO + v0 * SUBLANES + s, SUBLANES, stride=SUBLANES), :] = (
                    n2[t0:t0 + SUBLANES, c * LANES:(c + 1) * LANES])
    next_ref[...] = jnp.concatenate([nperm_ref[c] for c in range(n_slabs)], axis=1).astype(BF16)

    def conv(u, cols):
        cur = u[HALO:]
        wrap1 = jnp.concatenate([u[HALO - 1:HALO], u[HALO + tm - 8:HALO + tm - 1]], axis=0)
        wrap2 = jnp.concatenate([u[HALO - 9:HALO - 8], u[HALO + tm - 16:HALO + tm - 9]], axis=0)
        back1 = jnp.concatenate([wrap1, u[HALO:HALO + tm - 8]], axis=0)
        back2 = jnp.concatenate([wrap2, wrap1, u[HALO:HALO + tm - 16]], axis=0)
        return (cb_ref[:, cols] + back2 * cw_ref[0:1, cols] + back1 * cw_ref[1:2, cols]
                + cur * cw_ref[2:3, cols])

    for c in range(d_ff // FF_CHUNK):
        gcols = slice(c * FF_CHUNK, (c + 1) * FF_CHUNK)
        vcols = slice(d_ff + c * FF_CHUNK, d_ff + (c + 1) * FF_CHUNK)
        ug = jnp.dot(next_ref[...], wup_ref[:, gcols], preferred_element_type=F32)
        uv = jnp.dot(next_ref[...], wup_ref[:, vcols], preferred_element_type=F32)
        gate = conv(ug, gcols)
        act_ref[:, gcols] = (gate * jax.nn.sigmoid(gate) * conv(uv, vcols)).astype(BF16)
    ffn = jnp.dot(act_ref[...], wdown_ref[...], preferred_element_type=F32)
    for c in range(n_slabs):
        operm_ref[c] = ffn[:, c * LANES:(c + 1) * LANES]
        nperm_ref[c, 0:HALO, :] = nperm_ref[c, tm:tm + HALO, :]
    for s in range(SUBLANES):
        for v0 in range(0, nv, SUBLANES):
            t0 = s * nv + v0
            for c in range(n_slabs):
                lanes = slice(c * LANES, (c + 1) * LANES)
                out_ref[0, t0:t0 + SUBLANES, lanes] = (
                    h1[t0:t0 + SUBLANES, lanes]
                    + operm_ref[c, pl.ds(v0 * SUBLANES + s, SUBLANES, stride=SUBLANES), :])


def _mix_out(x, ya, yb, meta, yam, ybm, w_out, g, w_up, conv_w, conv_b, w_down):
    b, seq, d = x.shape
    pool_w = ya.shape[-1]
    tm = SEQ_TILE
    assert seq % tm == 0 and w_down.shape[0] % FF_CHUNK == 0
    tile = lambda w: pl.BlockSpec((1, tm, w), lambda bi, i: (bi, i, 0))
    mtile = lambda w: pl.BlockSpec((1, N_META, w), lambda bi, i: (bi, 0, 0))
    return pl.pallas_call(
        _mix_out_kernel,
        grid=(b, seq // tm),
        in_specs=[tile(d), tile(pool_w), tile(yb.shape[-1]), _const_spec(meta.shape),
                  mtile(pool_w), mtile(ybm.shape[-1]), _const_spec(w_out.shape), _const_spec(g.shape),
                  _const_spec(w_up.shape), _const_spec(conv_w.shape), _const_spec(conv_b.shape),
                  _const_spec(w_down.shape)],
        out_specs=tile(d),
        out_shape=jax.ShapeDtypeStruct((b, seq, d), x.dtype),
        scratch_shapes=[pltpu.VMEM((d // LANES, HALO + tm, LANES), F32), pltpu.VMEM((HALO + tm, d), BF16),
                        pltpu.VMEM((tm, w_down.shape[0]), BF16), pltpu.VMEM((d // LANES, tm, LANES), F32)],
        compiler_params=pltpu.CompilerParams(
            dimension_semantics=("arbitrary", "arbitrary"), vmem_limit_bytes=VMEM_LIMIT_BYTES),
        name="mix_out",
    )(x, ya, yb, meta, yam, ybm, w_out, g, w_up, conv_w, conv_b, w_down)


def kernel(x, meta_tokens, norm_mix_g, w_in, w_pool, b_pool, pool_scale, q_norm_g, k_norm_g,
           lambda_q1, lambda_k1, lambda_q2, lambda_k2, subln_g, w_out, norm_ffn_g,
           w_up, conv_w, conv_b, w_down):
    depth = w_in.shape[0]
    assert depth == 1 and meta_tokens.shape[0] == N_META
    row = lambda a: a.reshape(1, -1).astype(F32)
    h = x
    meta = meta_tokens.astype(x.dtype)
    for i in range(depth):
        lam_init = 0.8 - 0.6 * math.exp(-0.3 * i)
        qg = jnp.tile(row(q_norm_g[i]), (1, 2)) * (QK_DIM ** -0.5 * math.log2(math.e))
        kg = jnp.tile(row(k_norm_g[i]), (1, 2))
        ya, q, k, v, yam, qm, km, vm = _mix_in(
            h, meta, row(norm_mix_g[i]), w_in[i].astype(BF16), w_pool[i].astype(BF16),
            row(b_pool[i]), row(pool_scale[i]), qg, kg)
        yb, ybm = _diff_attn(q, k, v, qm, km, vm, row(lambda_q1[i]), row(lambda_k1[i]),
                             row(lambda_q2[i]), row(lambda_k2[i]), row(subln_g[i]), lam_init)
        h = _mix_out(h, ya, yb, meta, yam, ybm, w_out[i].astype(BF16), row(norm_ffn_g[i]),
                     w_up[i].astype(BF16), conv_w[i].astype(F32), row(conv_b[i]), w_down[i].astype(BF16))
    return h
```

```python
import functools
import math

import jax
import jax.numpy as jnp
from jax import lax
from jax.experimental import pallas as pl
from jax.experimental.pallas import tpu as pltpu

F32 = jnp.float32
BF16 = jnp.bfloat16

N_META = 16
POOL_WINDOWS = (2, 4, 8, 16)
N_HEADS = 4
HEAD_DIM = 128
QK_DIM = 64
CONV_WIDTH = 3
EPS = 1e-6

LANES = 128
SUBLANES = 8
HALO = 16
VMEM_LIMIT_BYTES = 56 * 1024 * 1024

SEQ_TILE = 512
ATTN_TILE = 512
FF_CHUNK = 256
BF16_SUBLANES = 16
N_LATER_WEIGHTS = 3
WDOWN_VIEW_ROWS = 512


def _rmsnorm(x, g):
    return x * lax.rsqrt(jnp.mean(x * x, axis=-1, keepdims=True) + EPS) * g


def _const_spec(shape):
    nd = len(shape)
    return pl.BlockSpec(shape, lambda *_: (0,) * nd, pipeline_mode=pl.Buffered(1))


def _mix_in_kernel(x_ref, meta_ref, g_ref, win_ref, wpool_ref, bpool_ref, pscale_ref,
                   qg_ref, kg_ref, *refs):
    later_w = refs[:N_LATER_WEIGHTS]
    ya_ref, q_ref, k_ref, v_ref, yam_ref, qm_ref, km_ref, vm_ref = refs[N_LATER_WEIGHTS:N_LATER_WEIGHTS + 8]
    later_w_bf16 = refs[N_LATER_WEIGHTS + 8:2 * N_LATER_WEIGHTS + 8]
    uext_ref = refs[-1]
    pool_w = bpool_ref.shape[-1]
    qk_w = q_ref.shape[-1]

    for src, dst in zip(later_w, later_w_bf16):
        dst[...] = src[...].astype(dst.dtype)

    def heads(rows, is_meta, ya_out, q_out, k_out, v_out):
        r = rows.shape[0]
        n = _rmsnorm(rows, g_ref[...])
        proj = jnp.dot(n.astype(BF16), win_ref[...], preferred_element_type=F32)
        u = proj[:, :pool_w]
        uext_ref[HALO:HALO + r, :] = u
        for gi, w in enumerate(POOL_WINDOWS):
            cols = slice(gi * LANES, (gi + 1) * LANES)
            win_sum = u[:, cols]
            for j in range(1, w):
                win_sum = win_sum + uext_ref[HALO - j:HALO - j + r, cols]
            if is_meta:
                pos = lax.broadcasted_iota(jnp.int32, (r, 1), 0)
                mean = win_sum / jnp.minimum(pos + 1, w).astype(F32)
            else:
                mean = win_sum * (1.0 / w)
            pooled = mean - u[:, cols]
            y = jnp.dot(pooled.astype(BF16), wpool_ref[gi], preferred_element_type=F32)
            y = (y + bpool_ref[:, cols]) * pscale_ref[:, cols]
            ya_out[0, :, cols] = y.astype(ya_out.dtype)
        uext_ref[0:HALO, :] = uext_ref[r:r + HALO, :]

        lane = lax.broadcasted_iota(jnp.int32, (r, LANES), 1)
        lo = lane < QK_DIM
        for src, gain_ref, out in ((proj[:, pool_w:pool_w + qk_w], qg_ref, q_out),
                                   (proj[:, pool_w + qk_w:pool_w + 2 * qk_w], kg_ref, k_out)):
            for h in range(N_HEADS):
                cols = slice(h * LANES, (h + 1) * LANES)
                slab = src[:, cols]
                sq = slab * slab
                s_lo = jnp.sum(jnp.where(lo, sq, 0.0), axis=-1, keepdims=True)
                s_hi = jnp.sum(jnp.where(lo, 0.0, sq), axis=-1, keepdims=True)
                ms = jnp.where(lo, s_lo, s_hi) * (1.0 / QK_DIM)
                out[0, :, cols] = (slab * lax.rsqrt(ms + EPS) * gain_ref[...]).astype(out.dtype)
        for h in range(N_HEADS):
            vcols = slice(pool_w + 2 * qk_w + h * HEAD_DIM, pool_w + 2 * qk_w + (h + 1) * HEAD_DIM)
            v_out[0, :, 2 * h * HEAD_DIM:(2 * h + 1) * HEAD_DIM] = proj[:, vcols].astype(v_out.dtype)
            v_out[0, :, (2 * h + 1) * HEAD_DIM:(2 * h + 2) * HEAD_DIM] = jnp.ones((r, HEAD_DIM), v_out.dtype)

    @pl.when(pl.program_id(1) == 0)
    def _():
        uext_ref[0:HALO, :] = jnp.zeros((HALO, pool_w), F32)
        heads(meta_ref[...], True, yam_ref, qm_ref, km_ref, vm_ref)

    heads(x_ref[0], False, ya_ref, q_ref, k_ref, v_ref)


def _mix_in(x, meta, g, w_in, w_pool, b_pool, pool_scale, qg, kg, later_weights):
    b, seq, d = x.shape
    pool_w = b_pool.shape[-1]
    qk_w = N_HEADS * HEAD_DIM
    tm = SEQ_TILE
    nt = seq // tm
    assert seq % tm == 0 and len(later_weights) == N_LATER_WEIGHTS
    tile = lambda w: pl.BlockSpec((1, tm, w), lambda bi, i: (bi, i, 0))
    mtile = lambda w: pl.BlockSpec((1, N_META, w), lambda bi, i: (bi, 0, 0))
    big = lambda w: jax.ShapeDtypeStruct((b, seq, w), BF16)
    small = lambda w: jax.ShapeDtypeStruct((b, N_META, w), BF16)
    assert all(w.shape[0] % (b * nt * BF16_SUBLANES) == 0 for w in later_weights)
    wblock = lambda w: pl.BlockSpec((w.shape[0] // (b * nt), w.shape[1]), lambda bi, i: (bi * nt + i, 0))
    outs = pl.pallas_call(
        _mix_in_kernel,
        grid=(b, nt),
        in_specs=[tile(d), _const_spec(meta.shape), _const_spec(g.shape), _const_spec(w_in.shape),
                  _const_spec(w_pool.shape), _const_spec(b_pool.shape), _const_spec(pool_scale.shape),
                  _const_spec(qg.shape), _const_spec(kg.shape)] + [wblock(w) for w in later_weights],
        out_specs=[tile(pool_w), tile(qk_w), tile(qk_w), tile(2 * qk_w),
                   mtile(pool_w), mtile(qk_w), mtile(qk_w), mtile(2 * qk_w)]
                  + [wblock(w) for w in later_weights],
        out_shape=[big(pool_w), big(qk_w), big(qk_w), big(2 * qk_w),
                   small(pool_w), small(qk_w), small(qk_w), small(2 * qk_w)]
                  + [jax.ShapeDtypeStruct(w.shape, BF16) for w in later_weights],
        scratch_shapes=[pltpu.VMEM((HALO + tm, pool_w), F32)],
        compiler_params=pltpu.CompilerParams(
            dimension_semantics=("arbitrary", "arbitrary"), vmem_limit_bytes=VMEM_LIMIT_BYTES),
        name="mix_in",
    )(x, meta, g, w_in, w_pool, b_pool, pool_scale, qg, kg, *later_weights)
    return outs[:8], outs[8:]


def _stack_subheads(q):
    lane = lax.broadcasted_iota(jnp.int32, q.shape, 1)
    zero = jnp.zeros_like(q)
    return jnp.concatenate([jnp.where(lane < QK_DIM, q, zero), jnp.where(lane < QK_DIM, zero, q)], axis=0)


def _scores(qs, k):
    return lax.dot_general(qs, k, (((1,), (1,)), ((), ())), preferred_element_type=F32)


def _causal_mask(s):
    r = s.shape[0] // 2
    row = lax.broadcasted_iota(jnp.int32, s.shape, 0)
    row = jnp.where(row >= r, row - r, row)
    col = lax.broadcasted_iota(jnp.int32, s.shape, 1)
    return jnp.where(col <= row, s, -jnp.inf)


def _online_softmax_step(s, v_ext, m_ref, acc_ref):
    width = s.shape[1]
    m = m_ref[...]
    m_new = jnp.maximum(m, jnp.max(s, axis=-1, keepdims=True))
    alpha = jnp.exp2(m - m_new)
    m_wide = jnp.concatenate([m_new] * (width // LANES), axis=1) if width > LANES else m_new[:, :width]
    p = jnp.exp2(s - m_wide)
    acc_ref[...] = (jnp.concatenate([alpha, alpha], axis=1) * acc_ref[...]
                    + jnp.dot(p.astype(v_ext.dtype), v_ext, preferred_element_type=F32))
    m_ref[...] = m_new


def _init_state(m_ref, acc_ref):
    m_ref[...] = jnp.full(m_ref.shape, -jnp.inf, F32)
    acc_ref[...] = jnp.zeros(acc_ref.shape, F32)


def _first_softmax_step(s, v_ext, m_ref, acc_ref):
    m = jnp.max(s, axis=-1, keepdims=True)
    p = jnp.exp2(s - m)
    m_ref[...] = jnp.broadcast_to(m, m_ref.shape)
    acc_ref[...] = jnp.dot(p.astype(v_ext.dtype), v_ext, preferred_element_type=F32)


def _diff_attn_kernel(lam_init, q_ref, qnext_ref, k_ref, v_ref, qm_ref, km_ref, vm_ref,
                      lq1_ref, lk1_ref, lq2_ref, lk2_ref, sg_ref, yb_ref, ybm_ref,
                      qs_ref, qn_ref, m_ref, acc_ref, sa_ref, sb_ref, sc_ref):
    i = pl.program_id(2)
    t = q_ref.shape[1]
    lam = (jnp.exp(jnp.sum(lq1_ref[...] * lk1_ref[...], axis=-1, keepdims=True))
           - jnp.exp(jnp.sum(lq2_ref[...] * lk2_ref[...], axis=-1, keepdims=True)) + lam_init)

    def finish(r, out_ref):
        o = (acc_ref[0:r, :HEAD_DIM] / acc_ref[0:r, HEAD_DIM:]
             - lam * (acc_ref[t:t + r, :HEAD_DIM] / acc_ref[t:t + r, HEAD_DIM:]))
        out_ref[0] = (_rmsnorm(o, sg_ref[...]) * (1.0 - lam_init)).astype(out_ref.dtype)

    km = km_ref[0]
    vm = vm_ref[0]

    @pl.when(i == 0)
    def _():
        qm = _stack_subheads(qm_ref[0])
        for half in range(2):
            rows = slice(half * t, half * t + N_META)
            _init_state(m_ref.at[rows], acc_ref.at[rows])
        sm = _causal_mask(_scores(qm, km))
        for half in range(2):
            rows = slice(half * t, half * t + N_META)
            _online_softmax_step(sm[half * N_META:(half + 1) * N_META], vm,
                                 m_ref.at[rows], acc_ref.at[rows])
        finish(N_META, ybm_ref)

    def kv_tile(ref, j):
        return ref[0, pl.ds(pl.multiple_of(j * t, t), t), :]

    def step(cur_ref, nxt_ref, j, last=False):
        if nxt_ref is not None:
            nxt_ref[...] = _scores(qs_ref[...], kv_tile(k_ref, j + 1))
        if last:
            qn_ref[...] = _stack_subheads(qnext_ref[0])
            sc_ref[...] = _scores(qn_ref[...], kv_tile(k_ref, 0))
        s = _causal_mask(cur_ref[...]) if last else cur_ref[...]
        _online_softmax_step(s, kv_tile(v_ref, j), m_ref, acc_ref)

    @pl.when(i == 0)
    def _():
        qs_ref[...] = _stack_subheads(q_ref[0])
        sa_ref[...] = _scores(qs_ref[...], kv_tile(k_ref, 0))
        _first_softmax_step(_scores(qs_ref[...], km), vm, m_ref, acc_ref)
        step(sa_ref, None, 0, last=True)

    @pl.when(i > 0)
    def _():
        qs_ref[...] = qn_ref[...]
        _first_softmax_step(_scores(qs_ref[...], km), vm, m_ref, acc_ref)
        step(sc_ref, sa_ref, 0)

        n_quads = (i - 1) // 4

        @pl.loop(0, n_quads)
        def _(jj):
            step(sa_ref, sb_ref, 4 * jj + 1)
            step(sb_ref, sa_ref, 4 * jj + 2)
            step(sa_ref, sb_ref, 4 * jj + 3)
            step(sb_ref, sa_ref, 4 * jj + 4)

        @pl.when((i - 1) % 4 >= 2)
        def _():
            step(sa_ref, sb_ref, 4 * n_quads + 1)
            step(sb_ref, sa_ref, 4 * n_quads + 2)

        @pl.when(i % 2 == 0)
        def _():
            step(sa_ref, sb_ref, i - 1)
            step(sb_ref, None, i, last=True)

        @pl.when(i % 2 == 1)
        def _():
            step(sa_ref, None, i, last=True)

    finish(t, yb_ref)


def _diff_attn(q, k, v, qm, km, vm, lq1, lk1, lq2, lk2, subln_g, lam_init):
    b, seq, width = q.shape
    t = ATTN_TILE
    assert seq % t == 0 and width == N_HEADS * HEAD_DIM
    nq = seq // t
    qtile = pl.BlockSpec((1, t, HEAD_DIM), lambda bi, h, i: (bi, i, h))
    qnext = pl.BlockSpec((1, t, HEAD_DIM), lambda bi, h, i: (bi, jnp.minimum(i + 1, nq - 1), h))
    full = lambda w: pl.BlockSpec((1, seq, w), lambda bi, h, i: (bi, 0, h))
    mtile = lambda w: pl.BlockSpec((1, N_META, w), lambda bi, h, i: (bi, 0, h))
    return pl.pallas_call(
        functools.partial(_diff_attn_kernel, lam_init),
        grid=(b, N_HEADS, nq),
        in_specs=[qtile, qnext, full(HEAD_DIM), full(2 * HEAD_DIM), mtile(HEAD_DIM), mtile(HEAD_DIM), mtile(2 * HEAD_DIM),
                  _const_spec(lq1.shape), _const_spec(lk1.shape), _const_spec(lq2.shape),
                  _const_spec(lk2.shape), _const_spec(subln_g.shape)],
        out_specs=[qtile, mtile(HEAD_DIM)],
        out_shape=[jax.ShapeDtypeStruct((b, seq, width), BF16),
                   jax.ShapeDtypeStruct((b, N_META, width), BF16)],
        scratch_shapes=[pltpu.VMEM((2 * t, HEAD_DIM), BF16), pltpu.VMEM((2 * t, HEAD_DIM), BF16),
                        pltpu.VMEM((2 * t, LANES), F32), pltpu.VMEM((2 * t, 2 * HEAD_DIM), F32),
                        pltpu.VMEM((2 * t, t), F32), pltpu.VMEM((2 * t, t), F32), pltpu.VMEM((2 * t, t), F32)],
        compiler_params=pltpu.CompilerParams(
            dimension_semantics=("arbitrary", "arbitrary", "arbitrary"),
            vmem_limit_bytes=VMEM_LIMIT_BYTES),
        name="diff_attn",
    )(q, q, k, v, qm, km, vm, lq1, lk1, lq2, lk2, subln_g)


def _mix_out_kernel(x_ref, ya_ref, yb_ref, meta_ref, yam_ref, ybm_ref, wo_ref, g_ref,
                    wup_ref, cw_ref, cb_ref, wdown_ref, out_ref, nperm_ref, next_ref, act_ref, operm_ref):
    tm = x_ref.shape[1]
    d = x_ref.shape[2]
    pool_w = ya_ref.shape[-1]
    d_ff = wdown_ref.shape[0]
    n_slabs = d // LANES
    nv = tm // SUBLANES

    def mix_residual(rows, ya, yb):
        h1 = (rows + jnp.dot(ya, wo_ref[:pool_w, :], preferred_element_type=F32)
              + jnp.dot(yb, wo_ref[pool_w:, :], preferred_element_type=F32))
        return h1, _rmsnorm(h1, g_ref[...])

    @pl.when(pl.program_id(1) == 0)
    def _():
        _, nm = mix_residual(meta_ref[...], yam_ref[0], ybm_ref[0])
        halo = jnp.concatenate([nm[N_META - 9:N_META - 1], nm[N_META - 8:]], axis=0)
        for c in range(n_slabs):
            nperm_ref[c, 0:HALO, :] = halo[:, c * LANES:(c + 1) * LANES]

    h1, n2 = mix_residual(x_ref[0], ya_ref[0], yb_ref[0])
    for s in range(SUBLANES):
        for v0 in range(0, nv, SUBLANES):
            t0 = s * nv + v0
            for c in range(n_slabs):
                nperm_ref[c, pl.ds(HALO + v0 * SUBLANES + s, SUBLANES, stride=SUBLANES), :] = (
                    n2[t0:t0 + SUBLANES, c * LANES:(c + 1) * LANES])
    next_ref[...] = jnp.concatenate([nperm_ref[c] for c in range(n_slabs)], axis=1).astype(BF16)

    def conv(u, cols):
        cur = u[HALO:]
        wrap1 = jnp.concatenate([u[HALO - 1:HALO], u[HALO + tm - 8:HALO + tm - 1]], axis=0)
        wrap2 = jnp.concatenate([u[HALO - 9:HALO - 8], u[HALO + tm - 16:HALO + tm - 9]], axis=0)
        back1 = jnp.concatenate([wrap1, u[HALO:HALO + tm - 8]], axis=0)
        back2 = jnp.concatenate([wrap2, wrap1, u[HALO:HALO + tm - 16]], axis=0)
        return (cb_ref[:, cols] + back2 * cw_ref[0:1, cols] + back1 * cw_ref[1:2, cols]
                + cur * cw_ref[2:3, cols])

    for c in range(d_ff // FF_CHUNK):
        gcols = slice(c * FF_CHUNK, (c + 1) * FF_CHUNK)
        vcols = slice(d_ff + c * FF_CHUNK, d_ff + (c + 1) * FF_CHUNK)
        ug = jnp.dot(next_ref[...], wup_ref[:, gcols], preferred_element_type=F32)
        uv = jnp.dot(next_ref[...], wup_ref[:, vcols], preferred_element_type=F32)
        gate = conv(ug, gcols)
        act_ref[:, gcols] = (gate * jax.nn.sigmoid(gate) * conv(uv, vcols)).astype(BF16)
    ffn = jnp.dot(act_ref[...], wdown_ref[...], preferred_element_type=F32)
    for c in range(n_slabs):
        operm_ref[c] = ffn[:, c * LANES:(c + 1) * LANES]
        nperm_ref[c, 0:HALO, :] = nperm_ref[c, tm:tm + HALO, :]
    for s in range(SUBLANES):
        for v0 in range(0, nv, SUBLANES):
            t0 = s * nv + v0
            for c in range(n_slabs):
                lanes = slice(c * LANES, (c + 1) * LANES)
                out_ref[0, t0:t0 + SUBLANES, lanes] = (
                    h1[t0:t0 + SUBLANES, lanes]
                    + operm_ref[c, pl.ds(v0 * SUBLANES + s, SUBLANES, stride=SUBLANES), :])


def _mix_out(x, ya, yb, meta, yam, ybm, w_out, g, w_up, conv_w, conv_b, w_down):
    b, seq, d = x.shape
    pool_w = ya.shape[-1]
    tm = SEQ_TILE
    assert seq % tm == 0 and w_down.shape[0] % FF_CHUNK == 0
    tile = lambda w: pl.BlockSpec((1, tm, w), lambda bi, i: (bi, i, 0))
    mtile = lambda w: pl.BlockSpec((1, N_META, w), lambda bi, i: (bi, 0, 0))
    return pl.pallas_call(
        _mix_out_kernel,
        grid=(b, seq // tm),
        in_specs=[tile(d), tile(pool_w), tile(yb.shape[-1]), _const_spec(meta.shape),
                  mtile(pool_w), mtile(ybm.shape[-1]), _const_spec(w_out.shape), _const_spec(g.shape),
                  _const_spec(w_up.shape), _const_spec(conv_w.shape), _const_spec(conv_b.shape),
                  _const_spec(w_down.shape)],
        out_specs=tile(d),
        out_shape=jax.ShapeDtypeStruct((b, seq, d), x.dtype),
        scratch_shapes=[pltpu.VMEM((d // LANES, HALO + tm, LANES), F32), pltpu.VMEM((HALO + tm, d), BF16),
                        pltpu.VMEM((tm, w_down.shape[0]), BF16), pltpu.VMEM((d // LANES, tm, LANES), F32)],
        compiler_params=pltpu.CompilerParams(
            dimension_semantics=("arbitrary", "arbitrary"), vmem_limit_bytes=VMEM_LIMIT_BYTES),
        name="mix_out",
    )(x, ya, yb, meta, yam, ybm, w_out, g, w_up, conv_w, conv_b, w_down)


def kernel(x, meta_tokens, norm_mix_g, w_in, w_pool, b_pool, pool_scale, q_norm_g, k_norm_g,
           lambda_q1, lambda_k1, lambda_q2, lambda_k2, subln_g, w_out, norm_ffn_g,
           w_up, conv_w, conv_b, w_down):
    depth = w_in.shape[0]
    assert depth == 1 and meta_tokens.shape[0] == N_META
    row = lambda a: a.reshape(1, -1).astype(F32)
    h = x
    meta = meta_tokens.astype(x.dtype)
    for i in range(depth):
        lam_init = 0.8 - 0.6 * math.exp(-0.3 * i)
        qg = jnp.tile(row(q_norm_g[i]), (1, 2)) * (QK_DIM ** -0.5 * math.log2(math.e))
        kg = jnp.tile(row(k_norm_g[i]), (1, 2))
        d_ff, d_model = w_down[i].shape
        (ya, q, k, v, yam, qm, km, vm), (w_out_n, w_up_n, w_down_n) = _mix_in(
            h, meta, row(norm_mix_g[i]), w_in[i].astype(BF16), w_pool[i].astype(BF16),
            row(b_pool[i]), row(pool_scale[i]), qg, kg,
            (w_out[i], w_up[i], w_down[i].reshape(WDOWN_VIEW_ROWS, -1)))
        yb, ybm = _diff_attn(q, k, v, qm, km, vm, row(lambda_q1[i]), row(lambda_k1[i]),
                             row(lambda_q2[i]), row(lambda_k2[i]), row(subln_g[i]), lam_init)
        h = _mix_out(h, ya, yb, meta, yam, ybm, w_out_n, row(norm_ffn_g[i]),
                     w_up_n, conv_w[i].astype(F32), row(conv_b[i]), w_down_n.reshape(d_ff, d_model))
    return h
```

```python
import functools
import math

import jax
import jax.numpy as jnp
from jax import lax
from jax.experimental import pallas as pl
from jax.experimental.pallas import tpu as pltpu

F32 = jnp.float32
BF16 = jnp.bfloat16

N_META = 16
POOL_WINDOWS = (2, 4, 8, 16)
N_HEADS = 4
HEAD_DIM = 128
QK_DIM = 64
CONV_WIDTH = 3
EPS = 1e-6

LANES = 128
SUBLANES = 8
HALO = 16
VMEM_LIMIT_BYTES = 56 * 1024 * 1024

SEQ_TILE = 512
ATTN_TILE = 512
FF_CHUNK = 256
BF16_SUBLANES = 16
N_LATER_WEIGHTS = 3


def _rmsnorm(x, g):
    return x * lax.rsqrt(jnp.mean(x * x, axis=-1, keepdims=True) + EPS) * g


def _const_spec(shape):
    nd = len(shape)
    return pl.BlockSpec(shape, lambda *_: (0,) * nd, pipeline_mode=pl.Buffered(1))


def _layer_const_spec(shape):
    return pl.BlockSpec((None,) + tuple(shape[1:]), lambda *_: (0, 0, 0), pipeline_mode=pl.Buffered(1))


def _mix_in_kernel(x_ref, meta_ref, g_ref, win_ref, wpool_ref, bpool_ref, pscale_ref,
                   qg_ref, kg_ref, *refs):
    later_w = refs[:N_LATER_WEIGHTS]
    ya_ref, q_ref, k_ref, v_ref, yam_ref, qm_ref, km_ref, vm_ref = refs[N_LATER_WEIGHTS:N_LATER_WEIGHTS + 8]
    later_w_bf16 = refs[N_LATER_WEIGHTS + 8:2 * N_LATER_WEIGHTS + 8]
    uext_ref = refs[-1]
    pool_w = bpool_ref.shape[-1]
    qk_w = q_ref.shape[-1]

    for src, dst in zip(later_w, later_w_bf16):
        dst[...] = src[...].astype(dst.dtype)

    def heads(rows, is_meta, ya_out, q_out, k_out, v_out):
        r = rows.shape[0]
        n = _rmsnorm(rows, g_ref[...])
        proj = jnp.dot(n.astype(BF16), win_ref[...], preferred_element_type=F32)
        u = proj[:, :pool_w]
        uext_ref[HALO:HALO + r, :] = u
        for gi, w in enumerate(POOL_WINDOWS):
            cols = slice(gi * LANES, (gi + 1) * LANES)
            win_sum = u[:, cols]
            for j in range(1, w):
                win_sum = win_sum + uext_ref[HALO - j:HALO - j + r, cols]
            if is_meta:
                pos = lax.broadcasted_iota(jnp.int32, (r, 1), 0)
                mean = win_sum / jnp.minimum(pos + 1, w).astype(F32)
            else:
                mean = win_sum * (1.0 / w)
            pooled = mean - u[:, cols]
            y = jnp.dot(pooled.astype(BF16), wpool_ref[gi], preferred_element_type=F32)
            y = (y + bpool_ref[:, cols]) * pscale_ref[:, cols]
            ya_out[0, :, cols] = y.astype(ya_out.dtype)
        uext_ref[0:HALO, :] = uext_ref[r:r + HALO, :]

        lane = lax.broadcasted_iota(jnp.int32, (r, LANES), 1)
        lo = lane < QK_DIM
        for src, gain_ref, out in ((proj[:, pool_w:pool_w + qk_w], qg_ref, q_out),
                                   (proj[:, pool_w + qk_w:pool_w + 2 * qk_w], kg_ref, k_out)):
            for h in range(N_HEADS):
                cols = slice(h * LANES, (h + 1) * LANES)
                slab = src[:, cols]
                sq = slab * slab
                s_lo = jnp.sum(jnp.where(lo, sq, 0.0), axis=-1, keepdims=True)
                s_hi = jnp.sum(jnp.where(lo, 0.0, sq), axis=-1, keepdims=True)
                ms = jnp.where(lo, s_lo, s_hi) * (1.0 / QK_DIM)
                out[0, :, cols] = (slab * lax.rsqrt(ms + EPS) * gain_ref[...]).astype(out.dtype)
        for h in range(N_HEADS):
            vcols = slice(pool_w + 2 * qk_w + h * HEAD_DIM, pool_w + 2 * qk_w + (h + 1) * HEAD_DIM)
            v_out[0, :, 2 * h * HEAD_DIM:(2 * h + 1) * HEAD_DIM] = proj[:, vcols].astype(v_out.dtype)
            v_out[0, :, (2 * h + 1) * HEAD_DIM:(2 * h + 2) * HEAD_DIM] = jnp.ones((r, HEAD_DIM), v_out.dtype)

    @pl.when(pl.program_id(1) == 0)
    def _():
        uext_ref[0:HALO, :] = jnp.zeros((HALO, pool_w), F32)
        heads(meta_ref[...], True, yam_ref, qm_ref, km_ref, vm_ref)

    heads(x_ref[0], False, ya_ref, q_ref, k_ref, v_ref)


def _mix_in(x, meta, g, w_in, w_pool, b_pool, pool_scale, qg, kg, later_weights):
    b, seq, d = x.shape
    pool_w = b_pool.shape[-1]
    qk_w = N_HEADS * HEAD_DIM
    tm = SEQ_TILE
    nt = seq // tm
    assert seq % tm == 0 and len(later_weights) == N_LATER_WEIGHTS
    tile = lambda w: pl.BlockSpec((1, tm, w), lambda bi, i: (bi, i, 0))
    mtile = lambda w: pl.BlockSpec((1, N_META, w), lambda bi, i: (bi, 0, 0))
    big = lambda w: jax.ShapeDtypeStruct((b, seq, w), BF16)
    small = lambda w: jax.ShapeDtypeStruct((b, N_META, w), BF16)
    def wblock(w):
        rows, steps = w.shape[1], b * nt
        hold = next(h for h in (1, 2, 4, 8) if (rows * h) % (steps * BF16_SUBLANES) == 0)
        return pl.BlockSpec((None, rows * hold // steps, w.shape[2]), lambda bi, i: (0, (bi * nt + i) // hold, 0))
    outs = pl.pallas_call(
        _mix_in_kernel,
        grid=(b, nt),
        in_specs=[tile(d), _const_spec(meta.shape), _const_spec(g.shape), _const_spec(w_in.shape),
                  _const_spec(w_pool.shape), _const_spec(b_pool.shape), _const_spec(pool_scale.shape),
                  _const_spec(qg.shape), _const_spec(kg.shape)] + [wblock(w) for w in later_weights],
        out_specs=[tile(pool_w), tile(qk_w), tile(qk_w), tile(2 * qk_w),
                   mtile(pool_w), mtile(qk_w), mtile(qk_w), mtile(2 * qk_w)]
                  + [wblock(w) for w in later_weights],
        out_shape=[big(pool_w), big(qk_w), big(qk_w), big(2 * qk_w),
                   small(pool_w), small(qk_w), small(qk_w), small(2 * qk_w)]
                  + [jax.ShapeDtypeStruct(w.shape, BF16) for w in later_weights],
        scratch_shapes=[pltpu.VMEM((HALO + tm, pool_w), F32)],
        compiler_params=pltpu.CompilerParams(
            dimension_semantics=("arbitrary", "arbitrary"), vmem_limit_bytes=VMEM_LIMIT_BYTES),
        name="mix_in",
    )(x, meta, g, w_in, w_pool, b_pool, pool_scale, qg, kg, *later_weights)
    return outs[:8], outs[8:]


def _stack_subheads(q):
    lane = lax.broadcasted_iota(jnp.int32, q.shape, 1)
    zero = jnp.zeros_like(q)
    return jnp.concatenate([jnp.where(lane < QK_DIM, q, zero), jnp.where(lane < QK_DIM, zero, q)], axis=0)


def _scores(qs, k):
    return lax.dot_general(qs, k, (((1,), (1,)), ((), ())), preferred_element_type=F32)


def _causal_mask(s):
    r = s.shape[0] // 2
    row = lax.broadcasted_iota(jnp.int32, s.shape, 0)
    row = jnp.where(row >= r, row - r, row)
    col = lax.broadcasted_iota(jnp.int32, s.shape, 1)
    return jnp.where(col <= row, s, -jnp.inf)


def _online_softmax_step(s, v_ext, m_ref, acc_ref):
    width = s.shape[1]
    m = m_ref[...]
    m_new = jnp.maximum(m, jnp.max(s, axis=-1, keepdims=True))
    alpha = jnp.exp2(m - m_new)
    m_wide = jnp.concatenate([m_new] * (width // LANES), axis=1) if width > LANES else m_new[:, :width]
    p = jnp.exp2(s - m_wide)
    acc_ref[...] = (jnp.concatenate([alpha, alpha], axis=1) * acc_ref[...]
                    + jnp.dot(p.astype(v_ext.dtype), v_ext, preferred_element_type=F32))
    m_ref[...] = m_new


def _init_state(m_ref, acc_ref):
    m_ref[...] = jnp.full(m_ref.shape, -jnp.inf, F32)
    acc_ref[...] = jnp.zeros(acc_ref.shape, F32)


def _first_softmax_step(s, v_ext, m_ref, acc_ref):
    m = jnp.max(s, axis=-1, keepdims=True)
    p = jnp.exp2(s - m)
    m_ref[...] = jnp.broadcast_to(m, m_ref.shape)
    acc_ref[...] = jnp.dot(p.astype(v_ext.dtype), v_ext, preferred_element_type=F32)


def _diff_attn_kernel(lam_init, q_ref, qnext_ref, k_ref, v_ref, qm_ref, km_ref, vm_ref,
                      lq1_ref, lk1_ref, lq2_ref, lk2_ref, sg_ref, yb_ref, ybm_ref,
                      qs_ref, qn_ref, m_ref, acc_ref, sa_ref, sb_ref, sc_ref):
    i = pl.program_id(2)
    t = q_ref.shape[1]
    lam = (jnp.exp(jnp.sum(lq1_ref[...] * lk1_ref[...], axis=-1, keepdims=True))
           - jnp.exp(jnp.sum(lq2_ref[...] * lk2_ref[...], axis=-1, keepdims=True)) + lam_init)

    def finish(r, out_ref):
        o = (acc_ref[0:r, :HEAD_DIM] / acc_ref[0:r, HEAD_DIM:]
             - lam * (acc_ref[t:t + r, :HEAD_DIM] / acc_ref[t:t + r, HEAD_DIM:]))
        out_ref[0] = (_rmsnorm(o, sg_ref[...]) * (1.0 - lam_init)).astype(out_ref.dtype)

    km = km_ref[0]
    vm = vm_ref[0]

    @pl.when(i == 0)
    def _():
        qm = _stack_subheads(qm_ref[0])
        for half in range(2):
            rows = slice(half * t, half * t + N_META)
            _init_state(m_ref.at[rows], acc_ref.at[rows])
        sm = _causal_mask(_scores(qm, km))
        for half in range(2):
            rows = slice(half * t, half * t + N_META)
            _online_softmax_step(sm[half * N_META:(half + 1) * N_META], vm,
                                 m_ref.at[rows], acc_ref.at[rows])
        finish(N_META, ybm_ref)

    def kv_tile(ref, j):
        return ref[0, pl.ds(pl.multiple_of(j * t, t), t), :]

    def step(cur_ref, nxt_ref, j, last=False):
        if nxt_ref is not None:
            nxt_ref[...] = _scores(qs_ref[...], kv_tile(k_ref, j + 1))
        if last:
            qn_ref[...] = _stack_subheads(qnext_ref[0])
            sc_ref[...] = _scores(qn_ref[...], kv_tile(k_ref, 0))
        s = _causal_mask(cur_ref[...]) if last else cur_ref[...]
        _online_softmax_step(s, kv_tile(v_ref, j), m_ref, acc_ref)

    @pl.when(i == 0)
    def _():
        qs_ref[...] = _stack_subheads(q_ref[0])
        sa_ref[...] = _scores(qs_ref[...], kv_tile(k_ref, 0))
        _first_softmax_step(_scores(qs_ref[...], km), vm, m_ref, acc_ref)
        step(sa_ref, None, 0, last=True)

    @pl.when(i > 0)
    def _():
        qs_ref[...] = qn_ref[...]
        _first_softmax_step(_scores(qs_ref[...], km), vm, m_ref, acc_ref)
        step(sc_ref, sa_ref, 0)

        n_quads = (i - 1) // 4

        @pl.loop(0, n_quads)
        def _(jj):
            step(sa_ref, sb_ref, 4 * jj + 1)
            step(sb_ref, sa_ref, 4 * jj + 2)
            step(sa_ref, sb_ref, 4 * jj + 3)
            step(sb_ref, sa_ref, 4 * jj + 4)

        @pl.when((i - 1) % 4 >= 2)
        def _():
            step(sa_ref, sb_ref, 4 * n_quads + 1)
            step(sb_ref, sa_ref, 4 * n_quads + 2)

        @pl.when(i % 2 == 0)
        def _():
            step(sa_ref, sb_ref, i - 1)
            step(sb_ref, None, i, last=True)

        @pl.when(i % 2 == 1)
        def _():
            step(sa_ref, None, i, last=True)

    finish(t, yb_ref)


def _diff_attn(q, k, v, qm, km, vm, lq1, lk1, lq2, lk2, subln_g, lam_init):
    b, seq, width = q.shape
    t = ATTN_TILE
    assert seq % t == 0 and width == N_HEADS * HEAD_DIM
    nq = seq // t
    qtile = pl.BlockSpec((1, t, HEAD_DIM), lambda bi, h, i: (bi, i, h))
    qnext = pl.BlockSpec((1, t, HEAD_DIM), lambda bi, h, i: (bi, jnp.minimum(i + 1, nq - 1), h))
    full = lambda w: pl.BlockSpec((1, seq, w), lambda bi, h, i: (bi, 0, h))
    mtile = lambda w: pl.BlockSpec((1, N_META, w), lambda bi, h, i: (bi, 0, h))
    return pl.pallas_call(
        functools.partial(_diff_attn_kernel, lam_init),
        grid=(b, N_HEADS, nq),
        in_specs=[qtile, qnext, full(HEAD_DIM), full(2 * HEAD_DIM), mtile(HEAD_DIM), mtile(HEAD_DIM), mtile(2 * HEAD_DIM),
                  _const_spec(lq1.shape), _const_spec(lk1.shape), _const_spec(lq2.shape),
                  _const_spec(lk2.shape), _const_spec(subln_g.shape)],
        out_specs=[qtile, mtile(HEAD_DIM)],
        out_shape=[jax.ShapeDtypeStruct((b, seq, width), BF16),
                   jax.ShapeDtypeStruct((b, N_META, width), BF16)],
        scratch_shapes=[pltpu.VMEM((2 * t, HEAD_DIM), BF16), pltpu.VMEM((2 * t, HEAD_DIM), BF16),
                        pltpu.VMEM((2 * t, LANES), F32), pltpu.VMEM((2 * t, 2 * HEAD_DIM), F32),
                        pltpu.VMEM((2 * t, t), F32), pltpu.VMEM((2 * t, t), F32), pltpu.VMEM((2 * t, t), F32)],
        compiler_params=pltpu.CompilerParams(
            dimension_semantics=("arbitrary", "arbitrary", "arbitrary"),
            vmem_limit_bytes=VMEM_LIMIT_BYTES),
        name="diff_attn",
    )(q, q, k, v, qm, km, vm, lq1, lk1, lq2, lk2, subln_g)


def _mix_out_kernel(x_ref, ya_ref, yb_ref, meta_ref, yam_ref, ybm_ref, wo_ref, g_ref,
                    wup_ref, cw_ref, cb_ref, wdown_ref, out_ref, nperm_ref, next_ref, act_ref, operm_ref):
    tm = x_ref.shape[1]
    d = x_ref.shape[2]
    pool_w = ya_ref.shape[-1]
    d_ff = wdown_ref.shape[0]
    n_slabs = d // LANES
    nv = tm // SUBLANES

    def mix_residual(rows, ya, yb):
        h1 = (rows + jnp.dot(ya, wo_ref[:pool_w, :], preferred_element_type=F32)
              + jnp.dot(yb, wo_ref[pool_w:, :], preferred_element_type=F32))
        return h1, _rmsnorm(h1, g_ref[...])

    @pl.when(pl.program_id(1) == 0)
    def _():
        _, nm = mix_residual(meta_ref[...], yam_ref[0], ybm_ref[0])
        halo = jnp.concatenate([nm[N_META - 9:N_META - 1], nm[N_META - 8:]], axis=0)
        for c in range(n_slabs):
            nperm_ref[c, 0:HALO, :] = halo[:, c * LANES:(c + 1) * LANES]

    h1, n2 = mix_residual(x_ref[0], ya_ref[0], yb_ref[0])
    for s in range(SUBLANES):
        for v0 in range(0, nv, SUBLANES):
            t0 = s * nv + v0
            for c in range(n_slabs):
                nperm_ref[c, pl.ds(HALO + v0 * SUBLANES + s, SUBLANES, stride=SUBLANES), :] = (
                    n2[t0:t0 + SUBLANES, c * LANES:(c + 1) * LANES])
    next_ref[...] = jnp.concatenate([nperm_ref[c] for c in range(n_slabs)], axis=1).astype(BF16)

    def conv(u, cols):
        cur = u[HALO:]
        wrap1 = jnp.concatenate([u[HALO - 1:HALO], u[HALO + tm - 8:HALO + tm - 1]], axis=0)
        wrap2 = jnp.concatenate([u[HALO - 9:HALO - 8], u[HALO + tm - 16:HALO + tm - 9]], axis=0)
        back1 = jnp.concatenate([wrap1, u[HALO:HALO + tm - 8]], axis=0)
        back2 = jnp.concatenate([wrap2, wrap1, u[HALO:HALO + tm - 16]], axis=0)
        return (cb_ref[:, cols] + back2 * cw_ref[0:1, cols] + back1 * cw_ref[1:2, cols]
                + cur * cw_ref[2:3, cols])

    for c in range(d_ff // FF_CHUNK):
        gcols = slice(c * FF_CHUNK, (c + 1) * FF_CHUNK)
        vcols = slice(d_ff + c * FF_CHUNK, d_ff + (c + 1) * FF_CHUNK)
        ug = jnp.dot(next_ref[...], wup_ref[:, gcols], preferred_element_type=F32)
        uv = jnp.dot(next_ref[...], wup_ref[:, vcols], preferred_element_type=F32)
        gate = conv(ug, gcols)
        act_ref[:, gcols] = (gate * jax.nn.sigmoid(gate) * conv(uv, vcols)).astype(BF16)
    ffn = jnp.dot(act_ref[...], wdown_ref[...], preferred_element_type=F32)
    for c in range(n_slabs):
        operm_ref[c] = ffn[:, c * LANES:(c + 1) * LANES]
        nperm_ref[c, 0:HALO, :] = nperm_ref[c, tm:tm + HALO, :]
    for s in range(SUBLANES):
        for v0 in range(0, nv, SUBLANES):
            t0 = s * nv + v0
            for c in range(n_slabs):
                lanes = slice(c * LANES, (c + 1) * LANES)
                out_ref[0, t0:t0 + SUBLANES, lanes] = (
                    h1[t0:t0 + SUBLANES, lanes]
                    + operm_ref[c, pl.ds(v0 * SUBLANES + s, SUBLANES, stride=SUBLANES), :])


def _mix_out(x, ya, yb, meta, yam, ybm, w_out, g, w_up, conv_w, conv_b, w_down):
    b, seq, d = x.shape
    pool_w = ya.shape[-1]
    tm = SEQ_TILE
    d_ff = w_down.shape[1]
    assert seq % tm == 0 and d_ff % FF_CHUNK == 0
    tile = lambda w: pl.BlockSpec((1, tm, w), lambda bi, i: (bi, i, 0))
    mtile = lambda w: pl.BlockSpec((1, N_META, w), lambda bi, i: (bi, 0, 0))
    return pl.pallas_call(
        _mix_out_kernel,
        grid=(b, seq // tm),
        in_specs=[tile(d), tile(pool_w), tile(yb.shape[-1]), _const_spec(meta.shape),
                  mtile(pool_w), mtile(ybm.shape[-1]), _layer_const_spec(w_out.shape), _const_spec(g.shape),
                  _layer_const_spec(w_up.shape), _const_spec(conv_w.shape), _const_spec(conv_b.shape),
                  _layer_const_spec(w_down.shape)],
        out_specs=tile(d),
        out_shape=jax.ShapeDtypeStruct((b, seq, d), x.dtype),
        scratch_shapes=[pltpu.VMEM((d // LANES, HALO + tm, LANES), F32), pltpu.VMEM((HALO + tm, d), BF16),
                        pltpu.VMEM((tm, d_ff), BF16), pltpu.VMEM((d // LANES, tm, LANES), F32)],
        compiler_params=pltpu.CompilerParams(
            dimension_semantics=("arbitrary", "arbitrary"), vmem_limit_bytes=VMEM_LIMIT_BYTES),
        name="mix_out",
    )(x, ya, yb, meta, yam, ybm, w_out, g, w_up, conv_w, conv_b, w_down)


def kernel(x, meta_tokens, norm_mix_g, w_in, w_pool, b_pool, pool_scale, q_norm_g, k_norm_g,
           lambda_q1, lambda_k1, lambda_q2, lambda_k2, subln_g, w_out, norm_ffn_g,
           w_up, conv_w, conv_b, w_down):
    depth = w_in.shape[0]
    assert depth == 1 and meta_tokens.shape[0] == N_META
    row = lambda a: a.reshape(1, -1).astype(F32)
    h = x
    meta = meta_tokens.astype(x.dtype)
    for i in range(depth):
        lam_init = 0.8 - 0.6 * math.exp(-0.3 * i)
        qg = jnp.tile(row(q_norm_g[i]), (1, 2)) * (QK_DIM ** -0.5 * math.log2(math.e))
        kg = jnp.tile(row(k_norm_g[i]), (1, 2))
        layer = slice(i, i + 1)
        (ya, q, k, v, yam, qm, km, vm), (w_out_n, w_up_n, w_down_n) = _mix_in(
            h, meta, row(norm_mix_g[i]), w_in[i].astype(BF16), w_pool[i].astype(BF16),
            row(b_pool[i]), row(pool_scale[i]), qg, kg, (w_out[layer], w_up[layer], w_down[layer]))
        yb, ybm = _diff_attn(q, k, v, qm, km, vm, row(lambda_q1[i]), row(lambda_k1[i]),
                             row(lambda_q2[i]), row(lambda_k2[i]), row(subln_g[i]), lam_init)
        h = _mix_out(h, ya, yb, meta, yam, ybm, w_out_n, row(norm_ffn_g[i]),
                     w_up_n, conv_w[i].astype(F32), row(conv_b[i]), w_down_n)
    return h
```

```python
import functools
import math

import jax
import jax.numpy as jnp
from jax import lax
from jax.experimental import pallas as pl
from jax.experimental.pallas import tpu as pltpu

F32 = jnp.float32
BF16 = jnp.bfloat16

N_META = 16
POOL_WINDOWS = (2, 4, 8, 16)
N_HEADS = 4
HEAD_DIM = 128
QK_DIM = 64
CONV_WIDTH = 3
EPS = 1e-6

LANES = 128
SUBLANES = 8
HALO = 16
VMEM_LIMIT_BYTES = 56 * 1024 * 1024

SEQ_TILE = 512
ATTN_TILE = 512
FF_CHUNK = 256
BF16_SUBLANES = 16
N_LATER_WEIGHTS = 3


def _rmsnorm(x, g):
    return x * lax.rsqrt(jnp.mean(x * x, axis=-1, keepdims=True) + EPS) * g


def _const_spec(shape):
    nd = len(shape)
    return pl.BlockSpec(shape, lambda *_: (0,) * nd, pipeline_mode=pl.Buffered(1))


def _layer_const_spec(shape):
    nd = len(shape)
    return pl.BlockSpec((None,) + tuple(shape[1:]), lambda *_: (0,) * nd, pipeline_mode=pl.Buffered(1))


def _mix_in_kernel(x_ref, meta_ref, g_ref, win_ref, wpool_ref, bpool_ref, pscale_ref,
                   qg_ref, kg_ref, *refs):
    later_w = refs[:N_LATER_WEIGHTS]
    ya_ref, q_ref, k_ref, v_ref, yam_ref, qm_ref, km_ref, vm_ref = refs[N_LATER_WEIGHTS:N_LATER_WEIGHTS + 8]
    later_w_bf16 = refs[N_LATER_WEIGHTS + 8:2 * N_LATER_WEIGHTS + 8]
    uext_ref, win_bf16_ref = refs[-2:]
    pool_w = bpool_ref.shape[-1]
    qk_w = q_ref.shape[-1]

    for src, dst in zip(later_w, later_w_bf16):
        dst[...] = src[...].astype(dst.dtype)

    @pl.when((pl.program_id(0) == 0) & (pl.program_id(1) == 0))
    def _():
        win_bf16_ref[...] = win_ref[...].astype(BF16)

    def heads(rows, is_meta, ya_out, q_out, k_out, v_out):
        r = rows.shape[0]
        n = _rmsnorm(rows, g_ref[...])
        proj = jnp.dot(n.astype(BF16), win_bf16_ref[...], preferred_element_type=F32)
        u = proj[:, :pool_w]
        uext_ref[HALO:HALO + r, :] = u
        for gi, w in enumerate(POOL_WINDOWS):
            cols = slice(gi * LANES, (gi + 1) * LANES)
            win_sum = u[:, cols]
            for j in range(1, w):
                win_sum = win_sum + uext_ref[HALO - j:HALO - j + r, cols]
            if is_meta:
                pos = lax.broadcasted_iota(jnp.int32, (r, 1), 0)
                mean = win_sum / jnp.minimum(pos + 1, w).astype(F32)
            else:
                mean = win_sum * (1.0 / w)
            pooled = mean - u[:, cols]
            y = jnp.dot(pooled.astype(BF16), wpool_ref[gi].astype(BF16), preferred_element_type=F32)
            y = (y + bpool_ref[:, cols]) * pscale_ref[:, cols]
            ya_out[0, :, cols] = y.astype(ya_out.dtype)
        uext_ref[0:HALO, :] = uext_ref[r:r + HALO, :]

        lane = lax.broadcasted_iota(jnp.int32, (r, LANES), 1)
        lo = lane < QK_DIM
        for src, gain_ref, out in ((proj[:, pool_w:pool_w + qk_w], qg_ref, q_out),
                                   (proj[:, pool_w + qk_w:pool_w + 2 * qk_w], kg_ref, k_out)):
            for h in range(N_HEADS):
                cols = slice(h * LANES, (h + 1) * LANES)
                slab = src[:, cols]
                sq = slab * slab
                s_lo = jnp.sum(jnp.where(lo, sq, 0.0), axis=-1, keepdims=True)
                s_hi = jnp.sum(jnp.where(lo, 0.0, sq), axis=-1, keepdims=True)
                ms = jnp.where(lo, s_lo, s_hi) * (1.0 / QK_DIM)
                out[0, :, cols] = (slab * lax.rsqrt(ms + EPS) * gain_ref[...]).astype(out.dtype)
        for h in range(N_HEADS):
            vcols = slice(pool_w + 2 * qk_w + h * HEAD_DIM, pool_w + 2 * qk_w + (h + 1) * HEAD_DIM)
            v_out[0, :, 2 * h * HEAD_DIM:(2 * h + 1) * HEAD_DIM] = proj[:, vcols].astype(v_out.dtype)
            v_out[0, :, (2 * h + 1) * HEAD_DIM:(2 * h + 2) * HEAD_DIM] = jnp.ones((r, HEAD_DIM), v_out.dtype)

    @pl.when(pl.program_id(1) == 0)
    def _():
        uext_ref[0:HALO, :] = jnp.zeros((HALO, pool_w), F32)
        heads(meta_ref[...], True, yam_ref, qm_ref, km_ref, vm_ref)

    heads(x_ref[0], False, ya_ref, q_ref, k_ref, v_ref)


def _mix_in(x, meta, g, w_in, w_pool, b_pool, pool_scale, qg, kg, later_weights):
    b, seq, d = x.shape
    pool_w = b_pool.shape[-1]
    qk_w = N_HEADS * HEAD_DIM
    tm = SEQ_TILE
    nt = seq // tm
    assert seq % tm == 0 and len(later_weights) == N_LATER_WEIGHTS
    tile = lambda w: pl.BlockSpec((1, tm, w), lambda bi, i: (bi, i, 0))
    mtile = lambda w: pl.BlockSpec((1, N_META, w), lambda bi, i: (bi, 0, 0))
    big = lambda w: jax.ShapeDtypeStruct((b, seq, w), BF16)
    small = lambda w: jax.ShapeDtypeStruct((b, N_META, w), BF16)
    def wblock(w):
        rows, steps = w.shape[1], b * nt
        hold = next(h for h in (1, 2, 4, 8) if (rows * h) % (steps * BF16_SUBLANES) == 0)
        return pl.BlockSpec((None, rows * hold // steps, w.shape[2]), lambda bi, i: (0, (bi * nt + i) // hold, 0))
    outs = pl.pallas_call(
        _mix_in_kernel,
        grid=(b, nt),
        in_specs=[tile(d), _const_spec(meta.shape), _const_spec(g.shape), _layer_const_spec(w_in.shape),
                  _layer_const_spec(w_pool.shape), _const_spec(b_pool.shape), _const_spec(pool_scale.shape),
                  _const_spec(qg.shape), _const_spec(kg.shape)] + [wblock(w) for w in later_weights],
        out_specs=[tile(pool_w), tile(qk_w), tile(qk_w), tile(2 * qk_w),
                   mtile(pool_w), mtile(qk_w), mtile(qk_w), mtile(2 * qk_w)]
                  + [wblock(w) for w in later_weights],
        out_shape=[big(pool_w), big(qk_w), big(qk_w), big(2 * qk_w),
                   small(pool_w), small(qk_w), small(qk_w), small(2 * qk_w)]
                  + [jax.ShapeDtypeStruct(w.shape, BF16) for w in later_weights],
        scratch_shapes=[pltpu.VMEM((HALO + tm, pool_w), F32), pltpu.VMEM(w_in.shape[1:], BF16)],
        compiler_params=pltpu.CompilerParams(
            dimension_semantics=("arbitrary", "arbitrary"), vmem_limit_bytes=VMEM_LIMIT_BYTES),
        name="mix_in",
    )(x, meta, g, w_in, w_pool, b_pool, pool_scale, qg, kg, *later_weights)
    return outs[:8], outs[8:]


def _stack_subheads(q):
    lane = lax.broadcasted_iota(jnp.int32, q.shape, 1)
    zero = jnp.zeros_like(q)
    return jnp.concatenate([jnp.where(lane < QK_DIM, q, zero), jnp.where(lane < QK_DIM, zero, q)], axis=0)


def _scores(qs, k):
    return lax.dot_general(qs, k, (((1,), (1,)), ((), ())), preferred_element_type=F32)


def _causal_mask(s):
    r = s.shape[0] // 2
    row = lax.broadcasted_iota(jnp.int32, s.shape, 0)
    row = jnp.where(row >= r, row - r, row)
    col = lax.broadcasted_iota(jnp.int32, s.shape, 1)
    return jnp.where(col <= row, s, -jnp.inf)


def _online_softmax_step(s, v_ext, m_ref, acc_ref):
    width = s.shape[1]
    m = m_ref[...]
    m_new = jnp.maximum(m, jnp.max(s, axis=-1, keepdims=True))
    alpha = jnp.exp2(m - m_new)
    m_wide = jnp.concatenate([m_new] * (width // LANES), axis=1) if width > LANES else m_new[:, :width]
    p = jnp.exp2(s - m_wide)
    acc_ref[...] = (jnp.concatenate([alpha, alpha], axis=1) * acc_ref[...]
                    + jnp.dot(p.astype(v_ext.dtype), v_ext, preferred_element_type=F32))
    m_ref[...] = m_new


def _init_state(m_ref, acc_ref):
    m_ref[...] = jnp.full(m_ref.shape, -jnp.inf, F32)
    acc_ref[...] = jnp.zeros(acc_ref.shape, F32)


def _first_softmax_step(s, v_ext, m_ref, acc_ref):
    m = jnp.max(s, axis=-1, keepdims=True)
    p = jnp.exp2(s - m)
    m_ref[...] = jnp.broadcast_to(m, m_ref.shape)
    acc_ref[...] = jnp.dot(p.astype(v_ext.dtype), v_ext, preferred_element_type=F32)


def _diff_attn_kernel(lam_init, q_ref, qnext_ref, k_ref, v_ref, qm_ref, km_ref, vm_ref,
                      lq1_ref, lk1_ref, lq2_ref, lk2_ref, sg_ref, yb_ref, ybm_ref,
                      qs_ref, qn_ref, m_ref, acc_ref, sa_ref, sb_ref, sc_ref):
    i = pl.program_id(2)
    t = q_ref.shape[1]
    lam = (jnp.exp(jnp.sum(lq1_ref[...] * lk1_ref[...], axis=-1, keepdims=True))
           - jnp.exp(jnp.sum(lq2_ref[...] * lk2_ref[...], axis=-1, keepdims=True)) + lam_init)

    def finish(r, out_ref):
        o = (acc_ref[0:r, :HEAD_DIM] / acc_ref[0:r, HEAD_DIM:]
             - lam * (acc_ref[t:t + r, :HEAD_DIM] / acc_ref[t:t + r, HEAD_DIM:]))
        out_ref[0] = (_rmsnorm(o, sg_ref[...]) * (1.0 - lam_init)).astype(out_ref.dtype)

    km = km_ref[0]
    vm = vm_ref[0]

    @pl.when(i == 0)
    def _():
        qm = _stack_subheads(qm_ref[0])
        for half in range(2):
            rows = slice(half * t, half * t + N_META)
            _init_state(m_ref.at[rows], acc_ref.at[rows])
        sm = _causal_mask(_scores(qm, km))
        for half in range(2):
            rows = slice(half * t, half * t + N_META)
            _online_softmax_step(sm[half * N_META:(half + 1) * N_META], vm,
                                 m_ref.at[rows], acc_ref.at[rows])
        finish(N_META, ybm_ref)

    def kv_tile(ref, j):
        return ref[0, pl.ds(pl.multiple_of(j * t, t), t), :]

    def step(cur_ref, nxt_ref, j, last=False):
        if nxt_ref is not None:
            nxt_ref[...] = _scores(qs_ref[...], kv_tile(k_ref, j + 1))
        if last:
            qn_ref[...] = _stack_subheads(qnext_ref[0])
            sc_ref[...] = _scores(qn_ref[...], kv_tile(k_ref, 0))
        s = _causal_mask(cur_ref[...]) if last else cur_ref[...]
        _online_softmax_step(s, kv_tile(v_ref, j), m_ref, acc_ref)

    @pl.when(i == 0)
    def _():
        qs_ref[...] = _stack_subheads(q_ref[0])
        sa_ref[...] = _scores(qs_ref[...], kv_tile(k_ref, 0))
        _first_softmax_step(_scores(qs_ref[...], km), vm, m_ref, acc_ref)
        step(sa_ref, None, 0, last=True)

    @pl.when(i > 0)
    def _():
        qs_ref[...] = qn_ref[...]
        _first_softmax_step(_scores(qs_ref[...], km), vm, m_ref, acc_ref)
        step(sc_ref, sa_ref, 0)

        n_quads = (i - 1) // 4

        @pl.loop(0, n_quads)
        def _(jj):
            step(sa_ref, sb_ref, 4 * jj + 1)
            step(sb_ref, sa_ref, 4 * jj + 2)
            step(sa_ref, sb_ref, 4 * jj + 3)
            step(sb_ref, sa_ref, 4 * jj + 4)

        @pl.when((i - 1) % 4 >= 2)
        def _():
            step(sa_ref, sb_ref, 4 * n_quads + 1)
            step(sb_ref, sa_ref, 4 * n_quads + 2)

        @pl.when(i % 2 == 0)
        def _():
            step(sa_ref, sb_ref, i - 1)
            step(sb_ref, None, i, last=True)

        @pl.when(i % 2 == 1)
        def _():
            step(sa_ref, None, i, last=True)

    finish(t, yb_ref)


def _diff_attn(q, k, v, qm, km, vm, lq1, lk1, lq2, lk2, subln_g, lam_init):
    b, seq, width = q.shape
    t = ATTN_TILE
    assert seq % t == 0 and width == N_HEADS * HEAD_DIM
    nq = seq // t
    qtile = pl.BlockSpec((1, t, HEAD_DIM), lambda bi, h, i: (bi, i, h))
    qnext = pl.BlockSpec((1, t, HEAD_DIM), lambda bi, h, i: (bi, jnp.minimum(i + 1, nq - 1), h))
    full = lambda w: pl.BlockSpec((1, seq, w), lambda bi, h, i: (bi, 0, h))
    mtile = lambda w: pl.BlockSpec((1, N_META, w), lambda bi, h, i: (bi, 0, h))
    return pl.pallas_call(
        functools.partial(_diff_attn_kernel, lam_init),
        grid=(b, N_HEADS, nq),
        in_specs=[qtile, qnext, full(HEAD_DIM), full(2 * HEAD_DIM), mtile(HEAD_DIM), mtile(HEAD_DIM), mtile(2 * HEAD_DIM),
                  _const_spec(lq1.shape), _const_spec(lk1.shape), _const_spec(lq2.shape),
                  _const_spec(lk2.shape), _const_spec(subln_g.shape)],
        out_specs=[qtile, mtile(HEAD_DIM)],
        out_shape=[jax.ShapeDtypeStruct((b, seq, width), BF16),
                   jax.ShapeDtypeStruct((b, N_META, width), BF16)],
        scratch_shapes=[pltpu.VMEM((2 * t, HEAD_DIM), BF16), pltpu.VMEM((2 * t, HEAD_DIM), BF16),
                        pltpu.VMEM((2 * t, LANES), F32), pltpu.VMEM((2 * t, 2 * HEAD_DIM), F32),
                        pltpu.VMEM((2 * t, t), F32), pltpu.VMEM((2 * t, t), F32), pltpu.VMEM((2 * t, t), F32)],
        compiler_params=pltpu.CompilerParams(
            dimension_semantics=("arbitrary", "arbitrary", "arbitrary"),
            vmem_limit_bytes=VMEM_LIMIT_BYTES),
        name="diff_attn",
    )(q, q, k, v, qm, km, vm, lq1, lk1, lq2, lk2, subln_g)


def _mix_out_kernel(x_ref, ya_ref, yb_ref, meta_ref, yam_ref, ybm_ref, wo_ref, g_ref,
                    wup_ref, cw_ref, cb_ref, wdown_ref, out_ref, nperm_ref, next_ref, act_ref, operm_ref):
    tm = x_ref.shape[1]
    d = x_ref.shape[2]
    pool_w = ya_ref.shape[-1]
    d_ff = wdown_ref.shape[0]
    n_slabs = d // LANES
    nv = tm // SUBLANES

    def mix_residual(rows, ya, yb):
        h1 = (rows + jnp.dot(ya, wo_ref[:pool_w, :], preferred_element_type=F32)
              + jnp.dot(yb, wo_ref[pool_w:, :], preferred_element_type=F32))
        return h1, _rmsnorm(h1, g_ref[...])

    @pl.when(pl.program_id(1) == 0)
    def _():
        _, nm = mix_residual(meta_ref[...], yam_ref[0], ybm_ref[0])
        halo = jnp.concatenate([nm[N_META - 9:N_META - 1], nm[N_META - 8:]], axis=0)
        for c in range(n_slabs):
            nperm_ref[c, 0:HALO, :] = halo[:, c * LANES:(c + 1) * LANES]

    h1, n2 = mix_residual(x_ref[0], ya_ref[0], yb_ref[0])
    for s in range(SUBLANES):
        for v0 in range(0, nv, SUBLANES):
            t0 = s * nv + v0
            for c in range(n_slabs):
                nperm_ref[c, pl.ds(HALO + v0 * SUBLANES + s, SUBLANES, stride=SUBLANES), :] = (
                    n2[t0:t0 + SUBLANES, c * LANES:(c + 1) * LANES])
    next_ref[...] = jnp.concatenate([nperm_ref[c] for c in range(n_slabs)], axis=1).astype(BF16)

    def conv(u, cols):
        cur = u[HALO:]
        wrap1 = jnp.concatenate([u[HALO - 1:HALO], u[HALO + tm - 8:HALO + tm - 1]], axis=0)
        wrap2 = jnp.concatenate([u[HALO - 9:HALO - 8], u[HALO + tm - 16:HALO + tm - 9]], axis=0)
        back1 = jnp.concatenate([wrap1, u[HALO:HALO + tm - 8]], axis=0)
        back2 = jnp.concatenate([wrap2, wrap1, u[HALO:HALO + tm - 16]], axis=0)
        return (cb_ref[:, cols] + back2 * cw_ref[0:1, cols] + back1 * cw_ref[1:2, cols]
                + cur * cw_ref[2:3, cols])

    for c in range(d_ff // FF_CHUNK):
        gcols = slice(c * FF_CHUNK, (c + 1) * FF_CHUNK)
        vcols = slice(d_ff + c * FF_CHUNK, d_ff + (c + 1) * FF_CHUNK)
        ug = jnp.dot(next_ref[...], wup_ref[:, gcols], preferred_element_type=F32)
        uv = jnp.dot(next_ref[...], wup_ref[:, vcols], preferred_element_type=F32)
        gate = conv(ug, gcols)
        act_ref[:, gcols] = (gate * jax.nn.sigmoid(gate) * conv(uv, vcols)).astype(BF16)
    ffn = jnp.dot(act_ref[...], wdown_ref[...], preferred_element_type=F32)
    for c in range(n_slabs):
        operm_ref[c] = ffn[:, c * LANES:(c + 1) * LANES]
        nperm_ref[c, 0:HALO, :] = nperm_ref[c, tm:tm + HALO, :]
    for s in range(SUBLANES):
        for v0 in range(0, nv, SUBLANES):
            t0 = s * nv + v0
            for c in range(n_slabs):
                lanes = slice(c * LANES, (c + 1) * LANES)
                out_ref[0, t0:t0 + SUBLANES, lanes] = (
                    h1[t0:t0 + SUBLANES, lanes]
                    + operm_ref[c, pl.ds(v0 * SUBLANES + s, SUBLANES, stride=SUBLANES), :])


def _mix_out(x, ya, yb, meta, yam, ybm, w_out, g, w_up, conv_w, conv_b, w_down):
    b, seq, d = x.shape
    pool_w = ya.shape[-1]
    tm = SEQ_TILE
    d_ff = w_down.shape[1]
    assert seq % tm == 0 and d_ff % FF_CHUNK == 0
    tile = lambda w: pl.BlockSpec((1, tm, w), lambda bi, i: (bi, i, 0))
    mtile = lambda w: pl.BlockSpec((1, N_META, w), lambda bi, i: (bi, 0, 0))
    return pl.pallas_call(
        _mix_out_kernel,
        grid=(b, seq // tm),
        in_specs=[tile(d), tile(pool_w), tile(yb.shape[-1]), _const_spec(meta.shape),
                  mtile(pool_w), mtile(ybm.shape[-1]), _layer_const_spec(w_out.shape), _const_spec(g.shape),
                  _layer_const_spec(w_up.shape), _const_spec(conv_w.shape), _const_spec(conv_b.shape),
                  _layer_const_spec(w_down.shape)],
        out_specs=tile(d),
        out_shape=jax.ShapeDtypeStruct((b, seq, d), x.dtype),
        scratch_shapes=[pltpu.VMEM((d // LANES, HALO + tm, LANES), F32), pltpu.VMEM((HALO + tm, d), BF16),
                        pltpu.VMEM((tm, d_ff), BF16), pltpu.VMEM((d // LANES, tm, LANES), F32)],
        compiler_params=pltpu.CompilerParams(
            dimension_semantics=("arbitrary", "arbitrary"), vmem_limit_bytes=VMEM_LIMIT_BYTES),
        name="mix_out",
    )(x, ya, yb, meta, yam, ybm, w_out, g, w_up, conv_w, conv_b, w_down)


def kernel(x, meta_tokens, norm_mix_g, w_in, w_pool, b_pool, pool_scale, q_norm_g, k_norm_g,
           lambda_q1, lambda_k1, lambda_q2, lambda_k2, subln_g, w_out, norm_ffn_g,
           w_up, conv_w, conv_b, w_down):
    depth = w_in.shape[0]
    assert depth == 1 and meta_tokens.shape[0] == N_META
    row = lambda a: a.reshape(1, -1).astype(F32)
    h = x
    meta = meta_tokens.astype(x.dtype)
    for i in range(depth):
        lam_init = 0.8 - 0.6 * math.exp(-0.3 * i)
        qg = jnp.tile(row(q_norm_g[i]), (1, 2)) * (QK_DIM ** -0.5 * math.log2(math.e))
        kg = jnp.tile(row(k_norm_g[i]), (1, 2))
        layer = slice(i, i + 1)
        (ya, q, k, v, yam, qm, km, vm), (w_out_n, w_up_n, w_down_n) = _mix_in(
            h, meta, row(norm_mix_g[i]), w_in[layer], w_pool[layer],
            row(b_pool[i]), row(pool_scale[i]), qg, kg, (w_out[layer], w_up[layer], w_down[layer]))
        yb, ybm = _diff_attn(q, k, v, qm, km, vm, row(lambda_q1[i]), row(lambda_k1[i]),
                             row(lambda_q2[i]), row(lambda_k2[i]), row(subln_g[i]), lam_init)
        h = _mix_out(h, ya, yb, meta, yam, ybm, w_out_n, row(norm_ffn_g[i]),
                     w_up_n, conv_w[i].astype(F32), row(conv_b[i]), w_down_n)
    return h
```

```python
import functools
import math

import jax
import jax.numpy as jnp
from jax import lax
from jax.experimental import pallas as pl
from jax.experimental.pallas import tpu as pltpu

F32 = jnp.float32
BF16 = jnp.bfloat16

N_META = 16
POOL_WINDOWS = (2, 4, 8, 16)
N_HEADS = 4
HEAD_DIM = 128
QK_DIM = 64
CONV_WIDTH = 3
EPS = 1e-6

LANES = 128
SUBLANES = 8
HALO = 16
VMEM_LIMIT_BYTES = 56 * 1024 * 1024

SEQ_TILE = 512
ATTN_TILE = 512
FF_CHUNK = 256
BF16_SUBLANES = 16
N_LATER_WEIGHTS = 3


def _rmsnorm(x, g):
    return x * lax.rsqrt(jnp.mean(x * x, axis=-1, keepdims=True) + EPS) * g


def _const_spec(shape):
    nd = len(shape)
    return pl.BlockSpec(shape, lambda *_: (0,) * nd, pipeline_mode=pl.Buffered(1))


def _layer_const_spec(shape):
    nd = len(shape)
    return pl.BlockSpec((None,) + tuple(shape[1:]), lambda *_: (0,) * nd, pipeline_mode=pl.Buffered(1))


def _mix_in_kernel(x_ref, meta_ref, g_ref, win_ref, wpool_ref, bpool_ref, pscale_ref,
                   qg_ref, kg_ref, *refs):
    later_w = refs[:N_LATER_WEIGHTS]
    ya_ref, q_ref, k_ref, v_ref, yam_ref, qm_ref, km_ref, vm_ref = refs[N_LATER_WEIGHTS:N_LATER_WEIGHTS + 8]
    later_w_bf16 = refs[N_LATER_WEIGHTS + 8:2 * N_LATER_WEIGHTS + 8]
    uext_ref, win_bf16_ref = refs[-2:]
    pool_w = bpool_ref.shape[-1]
    qk_w = q_ref.shape[-1]

    for src, dst in zip(later_w, later_w_bf16):
        dst[...] = src[...].astype(dst.dtype)

    @pl.when((pl.program_id(0) == 0) & (pl.program_id(1) == 0))
    def _():
        win_bf16_ref[...] = win_ref[...].astype(BF16)

    def heads(rows, is_meta, ya_out, q_out, k_out, v_out):
        r = rows.shape[0]
        n = _rmsnorm(rows, g_ref[...])
        proj = jnp.dot(n.astype(BF16), win_bf16_ref[...], preferred_element_type=F32)
        u = proj[:, :pool_w]
        uext_ref[HALO:HALO + r, :] = u
        for gi, w in enumerate(POOL_WINDOWS):
            cols = slice(gi * LANES, (gi + 1) * LANES)
            win_sum = u[:, cols]
            for j in range(1, w):
                win_sum = win_sum + uext_ref[HALO - j:HALO - j + r, cols]
            if is_meta:
                pos = lax.broadcasted_iota(jnp.int32, (r, 1), 0)
                mean = win_sum / jnp.minimum(pos + 1, w).astype(F32)
            else:
                mean = win_sum * (1.0 / w)
            pooled = mean - u[:, cols]
            y = jnp.dot(pooled.astype(BF16), wpool_ref[gi].astype(BF16), preferred_element_type=F32)
            y = (y + bpool_ref[:, cols]) * pscale_ref[:, cols]
            ya_out[0, :, cols] = y.astype(ya_out.dtype)
        uext_ref[0:HALO, :] = uext_ref[r:r + HALO, :]

        lane = lax.broadcasted_iota(jnp.int32, (r, LANES), 1)
        lo = lane < QK_DIM
        for src, gain_ref, out in ((proj[:, pool_w:pool_w + qk_w], qg_ref, q_out),
                                   (proj[:, pool_w + qk_w:pool_w + 2 * qk_w], kg_ref, k_out)):
            for h in range(N_HEADS):
                cols = slice(h * LANES, (h + 1) * LANES)
                slab = src[:, cols]
                sq = slab * slab
                s_lo = jnp.sum(jnp.where(lo, sq, 0.0), axis=-1, keepdims=True)
                s_hi = jnp.sum(jnp.where(lo, 0.0, sq), axis=-1, keepdims=True)
                ms = jnp.where(lo, s_lo, s_hi) * (1.0 / QK_DIM)
                out[0, :, cols] = (slab * lax.rsqrt(ms + EPS) * gain_ref[...]).astype(out.dtype)
        for h in range(N_HEADS):
            vcols = slice(pool_w + 2 * qk_w + h * HEAD_DIM, pool_w + 2 * qk_w + (h + 1) * HEAD_DIM)
            v_out[0, :, 2 * h * HEAD_DIM:(2 * h + 1) * HEAD_DIM] = proj[:, vcols].astype(v_out.dtype)
            v_out[0, :, (2 * h + 1) * HEAD_DIM:(2 * h + 2) * HEAD_DIM] = jnp.ones((r, HEAD_DIM), v_out.dtype)

    @pl.when(pl.program_id(1) == 0)
    def _():
        uext_ref[0:HALO, :] = jnp.zeros((HALO, pool_w), F32)
        heads(meta_ref[...], True, yam_ref, qm_ref, km_ref, vm_ref)

    heads(x_ref[0], False, ya_ref, q_ref, k_ref, v_ref)


def _mix_in(x, meta, g, w_in, w_pool, b_pool, pool_scale, qg, kg, later_weights):
    b, seq, d = x.shape
    pool_w = b_pool.shape[-1]
    qk_w = N_HEADS * HEAD_DIM
    tm = SEQ_TILE
    nt = seq // tm
    assert seq % tm == 0 and len(later_weights) == N_LATER_WEIGHTS
    tile = lambda w: pl.BlockSpec((1, tm, w), lambda bi, i: (bi, i, 0))
    mtile = lambda w: pl.BlockSpec((1, N_META, w), lambda bi, i: (bi, 0, 0))
    big = lambda w: jax.ShapeDtypeStruct((b, seq, w), BF16)
    small = lambda w: jax.ShapeDtypeStruct((b, N_META, w), BF16)
    def wblock(w):
        rows, steps = w.shape[1], b * nt
        hold = next(h for h in (1, 2, 4, 8) if (rows * h) % (steps * BF16_SUBLANES) == 0)
        return pl.BlockSpec((None, rows * hold // steps, w.shape[2]), lambda bi, i: (0, (bi * nt + i) // hold, 0))
    outs = pl.pallas_call(
        _mix_in_kernel,
        grid=(b, nt),
        in_specs=[tile(d), _const_spec(meta.shape), _const_spec(g.shape), _layer_const_spec(w_in.shape),
                  _layer_const_spec(w_pool.shape), _const_spec(b_pool.shape), _const_spec(pool_scale.shape),
                  _const_spec(qg.shape), _const_spec(kg.shape)] + [wblock(w) for w in later_weights],
        out_specs=[tile(pool_w), tile(qk_w), tile(qk_w), tile(2 * qk_w),
                   mtile(pool_w), mtile(qk_w), mtile(qk_w), mtile(2 * qk_w)]
                  + [wblock(w) for w in later_weights],
        out_shape=[big(pool_w), big(qk_w), big(qk_w), big(2 * qk_w),
                   small(pool_w), small(qk_w), small(qk_w), small(2 * qk_w)]
                  + [jax.ShapeDtypeStruct(w.shape, BF16) for w in later_weights],
        scratch_shapes=[pltpu.VMEM((HALO + tm, pool_w), F32), pltpu.VMEM(w_in.shape[1:], BF16)],
        compiler_params=pltpu.CompilerParams(
            dimension_semantics=("arbitrary", "arbitrary"), vmem_limit_bytes=VMEM_LIMIT_BYTES),
        name="mix_in",
    )(x, meta, g, w_in, w_pool, b_pool, pool_scale, qg, kg, *later_weights)
    return outs[:8], outs[8:]


def _stack_subheads(q, sub):
    lane = lax.broadcasted_iota(jnp.int32, (sub, q.shape[1]), 1)
    zero = jnp.zeros((sub, q.shape[1]), q.dtype)
    parts = []
    for r0 in range(0, q.shape[0], sub):
        part = q[r0:r0 + sub]
        parts += [jnp.where(lane < QK_DIM, part, zero), jnp.where(lane < QK_DIM, zero, part)]
    return jnp.concatenate(parts, axis=0)


def _scores(qs, k):
    return lax.dot_general(qs, k, (((1,), (1,)), ((), ())), preferred_element_type=F32)


def _causal_mask(s, masked_rows=None):
    sub = s.shape[1]
    row = lax.broadcasted_iota(jnp.int32, s.shape, 0)
    col = lax.broadcasted_iota(jnp.int32, s.shape, 1)
    visible = col <= (row & (sub - 1))
    if masked_rows is not None and masked_rows < s.shape[0]:
        visible = visible | (row >= masked_rows)
    return jnp.where(visible, s, -jnp.inf)


def _online_softmax_step(s, v_ext, m_ref, acc_ref):
    width = s.shape[1]
    m = m_ref[...]
    m_new = jnp.maximum(m, jnp.max(s, axis=-1, keepdims=True))
    alpha = jnp.exp2(m - m_new)
    m_wide = jnp.concatenate([m_new] * (width // LANES), axis=1) if width > LANES else m_new[:, :width]
    p = jnp.exp2(s - m_wide)
    acc_ref[...] = (jnp.concatenate([alpha, alpha], axis=1) * acc_ref[...]
                    + jnp.dot(p.astype(v_ext.dtype), v_ext, preferred_element_type=F32))
    m_ref[...] = m_new


def _init_state(m_ref, acc_ref):
    m_ref[...] = jnp.full(m_ref.shape, -jnp.inf, F32)
    acc_ref[...] = jnp.zeros(acc_ref.shape, F32)


def _first_softmax_step(s, v_ext, m_ref, acc_ref):
    m = jnp.max(s, axis=-1, keepdims=True)
    p = jnp.exp2(s - m)
    m_ref[...] = jnp.broadcast_to(m, m_ref.shape)
    acc_ref[...] = jnp.dot(p.astype(v_ext.dtype), v_ext, preferred_element_type=F32)


def _diff_attn_kernel(lam_init, q_ref, qnext_ref, k_ref, v_ref, qm_ref, km_ref, vm_ref,
                      lq1_ref, lk1_ref, lq2_ref, lk2_ref, sg_ref, yb_ref, ybm_ref,
                      qs_ref, qn_ref, m_ref, acc_ref, sa_ref, sb_ref, sc_ref):
    g = pl.program_id(2)
    t = q_ref.shape[1] // 2
    lam = (jnp.exp(jnp.sum(lq1_ref[...] * lk1_ref[...], axis=-1, keepdims=True))
           - jnp.exp(jnp.sum(lq2_ref[...] * lk2_ref[...], axis=-1, keepdims=True)) + lam_init)

    def finish(row0, r, out_ref, out_rows):
        o = (acc_ref[row0:row0 + r, :HEAD_DIM] / acc_ref[row0:row0 + r, HEAD_DIM:]
             - lam * (acc_ref[row0 + t:row0 + t + r, :HEAD_DIM] / acc_ref[row0 + t:row0 + t + r, HEAD_DIM:]))
        out_ref[0, out_rows, :] = (_rmsnorm(o, sg_ref[...]) * (1.0 - lam_init)).astype(out_ref.dtype)

    km = km_ref[0]
    vm = vm_ref[0]

    @pl.when(g == 0)
    def _():
        qm = _stack_subheads(qm_ref[0], N_META)
        for half in range(2):
            rows = slice(half * t, half * t + N_META)
            _init_state(m_ref.at[rows], acc_ref.at[rows])
        sm = _causal_mask(_scores(qm, km))
        for half in range(2):
            rows = slice(half * t, half * t + N_META)
            _online_softmax_step(sm[half * N_META:(half + 1) * N_META], vm,
                                 m_ref.at[rows], acc_ref.at[rows])
        finish(0, N_META, ybm_ref, slice(0, N_META))

    def kv_tile(ref, j):
        return ref[0, pl.ds(pl.multiple_of(j * t, t), t), :]

    b_rows = slice(2 * t, 4 * t)

    def step(cur_ref, nxt_ref, j):
        nxt_ref[...] = _scores(qs_ref[...], kv_tile(k_ref, j + 1))
        _online_softmax_step(cur_ref[...], kv_tile(v_ref, j), m_ref, acc_ref)

    def diagonal_steps(cur_ref, nxt_ref):
        nxt_ref[0:2 * t, :] = _scores(qs_ref[b_rows, :], kv_tile(k_ref, 2 * g + 1))
        _online_softmax_step(_causal_mask(cur_ref[...], masked_rows=2 * t), kv_tile(v_ref, 2 * g), m_ref, acc_ref)
        qn_ref[...] = _stack_subheads(qnext_ref[0], t)
        sc_ref[...] = _scores(qn_ref[...], kv_tile(k_ref, 0))
        _online_softmax_step(_causal_mask(nxt_ref[0:2 * t, :]), kv_tile(v_ref, 2 * g + 1),
                             m_ref.at[b_rows], acc_ref.at[b_rows])

    @pl.when(g == 0)
    def _():
        qs_ref[...] = _stack_subheads(q_ref[0], t)
        sa_ref[...] = _scores(qs_ref[...], kv_tile(k_ref, 0))
        _first_softmax_step(_scores(qs_ref[...], km), vm, m_ref, acc_ref)
        diagonal_steps(sa_ref, sb_ref)

    @pl.when(g > 0)
    def _():
        qs_ref[...] = qn_ref[...]
        _first_softmax_step(_scores(qs_ref[...], km), vm, m_ref, acc_ref)
        step(sc_ref, sa_ref, 0)

        n_quads = (2 * g - 1) // 4

        @pl.loop(0, n_quads)
        def _(jj):
            step(sa_ref, sb_ref, 4 * jj + 1)
            step(sb_ref, sa_ref, 4 * jj + 2)
            step(sa_ref, sb_ref, 4 * jj + 3)
            step(sb_ref, sa_ref, 4 * jj + 4)

        @pl.when((2 * g - 1) % 4 >= 2)
        def _():
            step(sa_ref, sb_ref, 4 * n_quads + 1)
            step(sb_ref, sa_ref, 4 * n_quads + 2)

        step(sa_ref, sb_ref, 2 * g - 1)
        diagonal_steps(sb_ref, sa_ref)

    finish(0, t, yb_ref, slice(0, t))
    finish(2 * t, t, yb_ref, slice(t, 2 * t))


def _diff_attn(q, k, v, qm, km, vm, lq1, lk1, lq2, lk2, subln_g, lam_init):
    b, seq, width = q.shape
    t = ATTN_TILE
    assert seq % (2 * t) == 0 and width == N_HEADS * HEAD_DIM
    ng = seq // (2 * t)
    qpair = pl.BlockSpec((1, 2 * t, HEAD_DIM), lambda bi, h, g: (bi, g, h))
    qnext = pl.BlockSpec((1, 2 * t, HEAD_DIM), lambda bi, h, g: (bi, jnp.minimum(g + 1, ng - 1), h))
    full = lambda w: pl.BlockSpec((1, seq, w), lambda bi, h, g: (bi, 0, h))
    mtile = lambda w: pl.BlockSpec((1, N_META, w), lambda bi, h, g: (bi, 0, h))
    return pl.pallas_call(
        functools.partial(_diff_attn_kernel, lam_init),
        grid=(b, N_HEADS, ng),
        in_specs=[qpair, qnext, full(HEAD_DIM), full(2 * HEAD_DIM), mtile(HEAD_DIM), mtile(HEAD_DIM), mtile(2 * HEAD_DIM),
                  _const_spec(lq1.shape), _const_spec(lk1.shape), _const_spec(lq2.shape),
                  _const_spec(lk2.shape), _const_spec(subln_g.shape)],
        out_specs=[qpair, mtile(HEAD_DIM)],
        out_shape=[jax.ShapeDtypeStruct((b, seq, width), BF16),
                   jax.ShapeDtypeStruct((b, N_META, width), BF16)],
        scratch_shapes=[pltpu.VMEM((4 * t, HEAD_DIM), BF16), pltpu.VMEM((4 * t, HEAD_DIM), BF16),
                        pltpu.VMEM((4 * t, LANES), F32), pltpu.VMEM((4 * t, 2 * HEAD_DIM), F32),
                        pltpu.VMEM((4 * t, t), F32), pltpu.VMEM((4 * t, t), F32), pltpu.VMEM((4 * t, t), F32)],
        compiler_params=pltpu.CompilerParams(
            dimension_semantics=("arbitrary", "arbitrary", "arbitrary"),
            vmem_limit_bytes=VMEM_LIMIT_BYTES),
        name="diff_attn",
    )(q, q, k, v, qm, km, vm, lq1, lk1, lq2, lk2, subln_g)


def _mix_out_kernel(x_ref, ya_ref, yb_ref, meta_ref, yam_ref, ybm_ref, wo_ref, g_ref,
                    wup_ref, cw_ref, cb_ref, wdown_ref, out_ref, nperm_ref, next_ref, act_ref, operm_ref):
    tm = x_ref.shape[1]
    d = x_ref.shape[2]
    pool_w = ya_ref.shape[-1]
    d_ff = wdown_ref.shape[0]
    n_slabs = d // LANES
    nv = tm // SUBLANES

    def mix_residual(rows, ya, yb):
        h1 = (rows + jnp.dot(ya, wo_ref[:pool_w, :], preferred_element_type=F32)
              + jnp.dot(yb, wo_ref[pool_w:, :], preferred_element_type=F32))
        return h1, _rmsnorm(h1, g_ref[...])

    @pl.when(pl.program_id(1) == 0)
    def _():
        _, nm = mix_residual(meta_ref[...], yam_ref[0], ybm_ref[0])
        halo = jnp.concatenate([nm[N_META - 9:N_META - 1], nm[N_META - 8:]], axis=0)
        for c in range(n_slabs):
            nperm_ref[c, 0:HALO, :] = halo[:, c * LANES:(c + 1) * LANES]

    h1, n2 = mix_residual(x_ref[0], ya_ref[0], yb_ref[0])
    for s in range(SUBLANES):
        for v0 in range(0, nv, SUBLANES):
            t0 = s * nv + v0
            for c in range(n_slabs):
                nperm_ref[c, pl.ds(HALO + v0 * SUBLANES + s, SUBLANES, stride=SUBLANES), :] = (
                    n2[t0:t0 + SUBLANES, c * LANES:(c + 1) * LANES])
    next_ref[...] = jnp.concatenate([nperm_ref[c] for c in range(n_slabs)], axis=1).astype(BF16)

    def conv(u, cols):
        cur = u[HALO:]
        wrap1 = jnp.concatenate([u[HALO - 1:HALO], u[HALO + tm - 8:HALO + tm - 1]], axis=0)
        wrap2 = jnp.concatenate([u[HALO - 9:HALO - 8], u[HALO + tm - 16:HALO + tm - 9]], axis=0)
        back1 = jnp.concatenate([wrap1, u[HALO:HALO + tm - 8]], axis=0)
        back2 = jnp.concatenate([wrap2, wrap1, u[HALO:HALO + tm - 16]], axis=0)
        return (cb_ref[:, cols] + back2 * cw_ref[0:1, cols] + back1 * cw_ref[1:2, cols]
                + cur * cw_ref[2:3, cols])

    for c in range(d_ff // FF_CHUNK):
        gcols = slice(c * FF_CHUNK, (c + 1) * FF_CHUNK)
        vcols = slice(d_ff + c * FF_CHUNK, d_ff + (c + 1) * FF_CHUNK)
        ug = jnp.dot(next_ref[...], wup_ref[:, gcols], preferred_element_type=F32)
        uv = jnp.dot(next_ref[...], wup_ref[:, vcols], preferred_element_type=F32)
        gate = conv(ug, gcols)
        act_ref[:, gcols] = (gate * jax.nn.sigmoid(gate) * conv(uv, vcols)).astype(BF16)
    ffn = jnp.dot(act_ref[...], wdown_ref[...], preferred_element_type=F32)
    for c in range(n_slabs):
        operm_ref[c] = ffn[:, c * LANES:(c + 1) * LANES]
        nperm_ref[c, 0:HALO, :] = nperm_ref[c, tm:tm + HALO, :]
    for s in range(SUBLANES):
        for v0 in range(0, nv, SUBLANES):
            t0 = s * nv + v0
            for c in range(n_slabs):
                lanes = slice(c * LANES, (c + 1) * LANES)
                out_ref[0, t0:t0 + SUBLANES, lanes] = (
                    h1[t0:t0 + SUBLANES, lanes]
                    + operm_ref[c, pl.ds(v0 * SUBLANES + s, SUBLANES, stride=SUBLANES), :])


def _mix_out(x, ya, yb, meta, yam, ybm, w_out, g, w_up, conv_w, conv_b, w_down):
    b, seq, d = x.shape
    pool_w = ya.shape[-1]
    tm = SEQ_TILE
    d_ff = w_down.shape[1]
    assert seq % tm == 0 and d_ff % FF_CHUNK == 0
    tile = lambda w: pl.BlockSpec((1, tm, w), lambda bi, i: (bi, i, 0))
    mtile = lambda w: pl.BlockSpec((1, N_META, w), lambda bi, i: (bi, 0, 0))
    return pl.pallas_call(
        _mix_out_kernel,
        grid=(b, seq // tm),
        in_specs=[tile(d), tile(pool_w), tile(yb.shape[-1]), _const_spec(meta.shape),
                  mtile(pool_w), mtile(ybm.shape[-1]), _layer_const_spec(w_out.shape), _const_spec(g.shape),
                  _layer_const_spec(w_up.shape), _const_spec(conv_w.shape), _const_spec(conv_b.shape),
                  _layer_const_spec(w_down.shape)],
        out_specs=tile(d),
        out_shape=jax.ShapeDtypeStruct((b, seq, d), x.dtype),
        scratch_shapes=[pltpu.VMEM((d // LANES, HALO + tm, LANES), F32), pltpu.VMEM((HALO + tm, d), BF16),
                        pltpu.VMEM((tm, d_ff), BF16), pltpu.VMEM((d // LANES, tm, LANES), F32)],
        compiler_params=pltpu.CompilerParams(
            dimension_semantics=("arbitrary", "arbitrary"), vmem_limit_bytes=VMEM_LIMIT_BYTES),
        name="mix_out",
    )(x, ya, yb, meta, yam, ybm, w_out, g, w_up, conv_w, conv_b, w_down)


def kernel(x, meta_tokens, norm_mix_g, w_in, w_pool, b_pool, pool_scale, q_norm_g, k_norm_g,
           lambda_q1, lambda_k1, lambda_q2, lambda_k2, subln_g, w_out, norm_ffn_g,
           w_up, conv_w, conv_b, w_down):
    depth = w_in.shape[0]
    assert depth == 1 and meta_tokens.shape[0] == N_META
    row = lambda a: a.reshape(1, -1).astype(F32)
    h = x
    meta = meta_tokens.astype(x.dtype)
    for i in range(depth):
        lam_init = 0.8 - 0.6 * math.exp(-0.3 * i)
        qg = jnp.tile(row(q_norm_g[i]), (1, 2)) * (QK_DIM ** -0.5 * math.log2(math.e))
        kg = jnp.tile(row(k_norm_g[i]), (1, 2))
        layer = slice(i, i + 1)
        (ya, q, k, v, yam, qm, km, vm), (w_out_n, w_up_n, w_down_n) = _mix_in(
            h, meta, row(norm_mix_g[i]), w_in[layer], w_pool[layer],
            row(b_pool[i]), row(pool_scale[i]), qg, kg, (w_out[layer], w_up[layer], w_down[layer]))
        yb, ybm = _diff_attn(q, k, v, qm, km, vm, row(lambda_q1[i]), row(lambda_k1[i]),
                             row(lambda_q2[i]), row(lambda_k2[i]), row(subln_g[i]), lam_init)
        h = _mix_out(h, ya, yb, meta, yam, ybm, w_out_n, row(norm_ffn_g[i]),
                     w_up_n, conv_w[i].astype(F32), row(conv_b[i]), w_down_n)
    return h
```

```python
import functools
import math

import jax
import jax.numpy as jnp
from jax import lax
from jax.experimental import pallas as pl
from jax.experimental.pallas import tpu as pltpu

F32 = jnp.float32
BF16 = jnp.bfloat16

N_META = 16
POOL_WINDOWS = (2, 4, 8, 16)
N_HEADS = 4
HEAD_DIM = 128
QK_DIM = 64
CONV_WIDTH = 3
EPS = 1e-6

LANES = 128
SUBLANES = 8
HALO = 16
VMEM_LIMIT_BYTES = 56 * 1024 * 1024

SEQ_TILE = 512
ATTN_TILE = 512
ATTN_SUBTILES = 2
FF_CHUNK = 256
BF16_SUBLANES = 16
N_LATER_WEIGHTS = 3


def _rmsnorm(x, g):
    return x * lax.rsqrt(jnp.mean(x * x, axis=-1, keepdims=True) + EPS) * g


def _const_spec(shape):
    nd = len(shape)
    return pl.BlockSpec(shape, lambda *_: (0,) * nd, pipeline_mode=pl.Buffered(1))


def _layer_const_spec(shape):
    nd = len(shape)
    return pl.BlockSpec((None,) + tuple(shape[1:]), lambda *_: (0,) * nd, pipeline_mode=pl.Buffered(1))


def _mix_in_kernel(x_ref, meta_ref, g_ref, win_ref, wpool_ref, bpool_ref, pscale_ref,
                   qg_ref, kg_ref, *refs):
    later_w = refs[:N_LATER_WEIGHTS]
    ya_ref, q_ref, k_ref, v_ref, yam_ref, qm_ref, km_ref, vm_ref = refs[N_LATER_WEIGHTS:N_LATER_WEIGHTS + 8]
    later_w_bf16 = refs[N_LATER_WEIGHTS + 8:2 * N_LATER_WEIGHTS + 8]
    uext_ref, win_bf16_ref = refs[-2:]
    pool_w = bpool_ref.shape[-1]
    qk_w = q_ref.shape[-1]

    for src, dst in zip(later_w, later_w_bf16):
        dst[...] = src[...].astype(dst.dtype)

    @pl.when((pl.program_id(0) == 0) & (pl.program_id(1) == 0))
    def _():
        win_bf16_ref[...] = win_ref[...].astype(BF16)

    def heads(rows, is_meta, ya_out, q_out, k_out, v_out):
        r = rows.shape[0]
        n = _rmsnorm(rows, g_ref[...])
        proj = jnp.dot(n.astype(BF16), win_bf16_ref[...], preferred_element_type=F32)
        u = proj[:, :pool_w]
        uext_ref[HALO:HALO + r, :] = u
        for gi, w in enumerate(POOL_WINDOWS):
            cols = slice(gi * LANES, (gi + 1) * LANES)
            win_sum = u[:, cols]
            for j in range(1, w):
                win_sum = win_sum + uext_ref[HALO - j:HALO - j + r, cols]
            if is_meta:
                pos = lax.broadcasted_iota(jnp.int32, (r, 1), 0)
                mean = win_sum / jnp.minimum(pos + 1, w).astype(F32)
            else:
                mean = win_sum * (1.0 / w)
            pooled = mean - u[:, cols]
            y = jnp.dot(pooled.astype(BF16), wpool_ref[gi].astype(BF16), preferred_element_type=F32)
            y = (y + bpool_ref[:, cols]) * pscale_ref[:, cols]
            ya_out[0, :, cols] = y.astype(ya_out.dtype)
        uext_ref[0:HALO, :] = uext_ref[r:r + HALO, :]

        lane = lax.broadcasted_iota(jnp.int32, (r, LANES), 1)
        lo = lane < QK_DIM
        for src, gain_ref, out in ((proj[:, pool_w:pool_w + qk_w], qg_ref, q_out),
                                   (proj[:, pool_w + qk_w:pool_w + 2 * qk_w], kg_ref, k_out)):
            for h in range(N_HEADS):
                cols = slice(h * LANES, (h + 1) * LANES)
                slab = src[:, cols]
                sq = slab * slab
                s_lo = jnp.sum(jnp.where(lo, sq, 0.0), axis=-1, keepdims=True)
                s_hi = jnp.sum(jnp.where(lo, 0.0, sq), axis=-1, keepdims=True)
                ms = jnp.where(lo, s_lo, s_hi) * (1.0 / QK_DIM)
                out[0, :, cols] = (slab * lax.rsqrt(ms + EPS) * gain_ref[...]).astype(out.dtype)
        for h in range(N_HEADS):
            vcols = slice(pool_w + 2 * qk_w + h * HEAD_DIM, pool_w + 2 * qk_w + (h + 1) * HEAD_DIM)
            v_out[0, :, 2 * h * HEAD_DIM:(2 * h + 1) * HEAD_DIM] = proj[:, vcols].astype(v_out.dtype)
            v_out[0, :, (2 * h + 1) * HEAD_DIM:(2 * h + 2) * HEAD_DIM] = jnp.ones((r, HEAD_DIM), v_out.dtype)

    @pl.when(pl.program_id(1) == 0)
    def _():
        uext_ref[0:HALO, :] = jnp.zeros((HALO, pool_w), F32)
        heads(meta_ref[...], True, yam_ref, qm_ref, km_ref, vm_ref)

    heads(x_ref[0], False, ya_ref, q_ref, k_ref, v_ref)


def _mix_in(x, meta, g, w_in, w_pool, b_pool, pool_scale, qg, kg, later_weights):
    b, seq, d = x.shape
    pool_w = b_pool.shape[-1]
    qk_w = N_HEADS * HEAD_DIM
    tm = SEQ_TILE
    nt = seq // tm
    assert seq % tm == 0 and len(later_weights) == N_LATER_WEIGHTS
    tile = lambda w: pl.BlockSpec((1, tm, w), lambda bi, i: (bi, i, 0))
    mtile = lambda w: pl.BlockSpec((1, N_META, w), lambda bi, i: (bi, 0, 0))
    big = lambda w: jax.ShapeDtypeStruct((b, seq, w), BF16)
    small = lambda w: jax.ShapeDtypeStruct((b, N_META, w), BF16)
    def wblock(w):
        rows, steps = w.shape[1], b * nt
        hold = next(h for h in (1, 2, 4, 8) if (rows * h) % (steps * BF16_SUBLANES) == 0)
        return pl.BlockSpec((None, rows * hold // steps, w.shape[2]), lambda bi, i: (0, (bi * nt + i) // hold, 0))
    outs = pl.pallas_call(
        _mix_in_kernel,
        grid=(b, nt),
        in_specs=[tile(d), _const_spec(meta.shape), _const_spec(g.shape), _layer_const_spec(w_in.shape),
                  _layer_const_spec(w_pool.shape), _const_spec(b_pool.shape), _const_spec(pool_scale.shape),
                  _const_spec(qg.shape), _const_spec(kg.shape)] + [wblock(w) for w in later_weights],
        out_specs=[tile(pool_w), tile(qk_w), tile(qk_w), tile(2 * qk_w),
                   mtile(pool_w), mtile(qk_w), mtile(qk_w), mtile(2 * qk_w)]
                  + [wblock(w) for w in later_weights],
        out_shape=[big(pool_w), big(qk_w), big(qk_w), big(2 * qk_w),
                   small(pool_w), small(qk_w), small(qk_w), small(2 * qk_w)]
                  + [jax.ShapeDtypeStruct(w.shape, BF16) for w in later_weights],
        scratch_shapes=[pltpu.VMEM((HALO + tm, pool_w), F32), pltpu.VMEM(w_in.shape[1:], BF16)],
        compiler_params=pltpu.CompilerParams(
            dimension_semantics=("arbitrary", "arbitrary"), vmem_limit_bytes=VMEM_LIMIT_BYTES),
        name="mix_in",
    )(x, meta, g, w_in, w_pool, b_pool, pool_scale, qg, kg, *later_weights)
    return outs[:8], outs[8:]


def _stack_subheads(q, sub):
    lane = lax.broadcasted_iota(jnp.int32, (sub, q.shape[1]), 1)
    zero = jnp.zeros((sub, q.shape[1]), q.dtype)
    parts = []
    for r0 in range(0, q.shape[0], sub):
        part = q[r0:r0 + sub]
        parts += [jnp.where(lane < QK_DIM, part, zero), jnp.where(lane < QK_DIM, zero, part)]
    return jnp.concatenate(parts, axis=0)


def _scores(qs, k):
    return lax.dot_general(qs, k, (((1,), (1,)), ((), ())), preferred_element_type=F32)


def _causal_mask(s, masked_rows=None):
    sub = s.shape[1]
    row = lax.broadcasted_iota(jnp.int32, s.shape, 0)
    col = lax.broadcasted_iota(jnp.int32, s.shape, 1)
    visible = col <= (row & (sub - 1))
    if masked_rows is not None and masked_rows < s.shape[0]:
        visible = visible | (row >= masked_rows)
    return jnp.where(visible, s, -jnp.inf)


def _online_softmax_step(s, v_ext, m_ref, acc_ref):
    width = s.shape[1]
    m = m_ref[...]
    m_new = jnp.maximum(m, jnp.max(s, axis=-1, keepdims=True))
    alpha = jnp.exp2(m - m_new)
    m_wide = jnp.concatenate([m_new] * (width // LANES), axis=1) if width > LANES else m_new[:, :width]
    p = jnp.exp2(s - m_wide)
    acc_ref[...] = (jnp.concatenate([alpha, alpha], axis=1) * acc_ref[...]
                    + jnp.dot(p.astype(v_ext.dtype), v_ext, preferred_element_type=F32))
    m_ref[...] = m_new


def _init_state(m_ref, acc_ref):
    m_ref[...] = jnp.full(m_ref.shape, -jnp.inf, F32)
    acc_ref[...] = jnp.zeros(acc_ref.shape, F32)


def _first_softmax_step(s, v_ext, m_ref, acc_ref):
    m = jnp.max(s, axis=-1, keepdims=True)
    p = jnp.exp2(s - m)
    m_ref[...] = jnp.broadcast_to(m, m_ref.shape)
    acc_ref[...] = jnp.dot(p.astype(v_ext.dtype), v_ext, preferred_element_type=F32)


def _diff_attn_kernel(lam_init, q_ref, qnext_ref, k_ref, v_ref, qm_ref, km_ref, vm_ref,
                      lq1_ref, lk1_ref, lq2_ref, lk2_ref, sg_ref, yb_ref, ybm_ref,
                      qs_ref, qn_ref, m_ref, acc_ref, sa_ref, sb_ref, sc_ref):
    g = pl.program_id(2)
    n = ATTN_SUBTILES
    t = q_ref.shape[1] // n
    lam = (jnp.exp(jnp.sum(lq1_ref[...] * lk1_ref[...], axis=-1, keepdims=True))
           - jnp.exp(jnp.sum(lq2_ref[...] * lk2_ref[...], axis=-1, keepdims=True)) + lam_init)

    def finish(row0, r, out_ref, out_rows):
        o = (acc_ref[row0:row0 + r, :HEAD_DIM] / acc_ref[row0:row0 + r, HEAD_DIM:]
             - lam * (acc_ref[row0 + t:row0 + t + r, :HEAD_DIM] / acc_ref[row0 + t:row0 + t + r, HEAD_DIM:]))
        out_ref[0, out_rows, :] = (_rmsnorm(o, sg_ref[...]) * (1.0 - lam_init)).astype(out_ref.dtype)

    km = km_ref[0]
    vm = vm_ref[0]

    @pl.when(g == 0)
    def _():
        qm = _stack_subheads(qm_ref[0], N_META)
        for half in range(2):
            rows = slice(half * t, half * t + N_META)
            _init_state(m_ref.at[rows], acc_ref.at[rows])
        sm = _causal_mask(_scores(qm, km))
        for half in range(2):
            rows = slice(half * t, half * t + N_META)
            _online_softmax_step(sm[half * N_META:(half + 1) * N_META], vm,
                                 m_ref.at[rows], acc_ref.at[rows])
        finish(0, N_META, ybm_ref, slice(0, N_META))

    def kv_tile(ref, j):
        return ref[0, pl.ds(pl.multiple_of(j * t, t), t), :]

    def step(cur_ref, nxt_ref, j):
        nxt_ref[...] = _scores(qs_ref[...], kv_tile(k_ref, j + 1))
        _online_softmax_step(cur_ref[...], kv_tile(v_ref, j), m_ref, acc_ref)

    def diagonal_steps(cur_ref, nxt_ref):
        for c in range(n):
            rows = slice(2 * t * c, 2 * t * n)
            n_rows = 2 * t * (n - c)
            if c + 1 < n:
                nxt_ref[0:n_rows - 2 * t, :] = _scores(qs_ref[2 * t * (c + 1):2 * t * n, :],
                                                       kv_tile(k_ref, n * g + c + 1))
            else:
                qn_ref[...] = _stack_subheads(qnext_ref[0], t)
                sc_ref[...] = _scores(qn_ref[...], kv_tile(k_ref, 0))
            _online_softmax_step(_causal_mask(cur_ref[0:n_rows, :], masked_rows=2 * t), kv_tile(v_ref, n * g + c),
                                 m_ref.at[rows], acc_ref.at[rows])
            cur_ref, nxt_ref = nxt_ref, cur_ref

    @pl.when(g == 0)
    def _():
        qs_ref[...] = _stack_subheads(q_ref[0], t)
        sa_ref[...] = _scores(qs_ref[...], kv_tile(k_ref, 0))
        _first_softmax_step(_scores(qs_ref[...], km), vm, m_ref, acc_ref)
        diagonal_steps(sa_ref, sb_ref)

    @pl.when(g > 0)
    def _():
        qs_ref[...] = qn_ref[...]
        _first_softmax_step(_scores(qs_ref[...], km), vm, m_ref, acc_ref)
        step(sc_ref, sa_ref, 0)

        n_quads = (n * g - 1) // 4

        @pl.loop(0, n_quads)
        def _(jj):
            step(sa_ref, sb_ref, 4 * jj + 1)
            step(sb_ref, sa_ref, 4 * jj + 2)
            step(sa_ref, sb_ref, 4 * jj + 3)
            step(sb_ref, sa_ref, 4 * jj + 4)

        @pl.when((n * g - 1) % 4 >= 2)
        def _():
            step(sa_ref, sb_ref, 4 * n_quads + 1)
            step(sb_ref, sa_ref, 4 * n_quads + 2)

        step(sa_ref, sb_ref, n * g - 1)
        diagonal_steps(sb_ref, sa_ref)

    for c in range(n):
        finish(2 * t * c, t, yb_ref, slice(t * c, t * (c + 1)))


def _diff_attn(q, k, v, qm, km, vm, lq1, lk1, lq2, lk2, subln_g, lam_init):
    b, seq, width = q.shape
    t, n = ATTN_TILE, ATTN_SUBTILES
    assert seq % (n * t) == 0 and n % 2 == 0 and width == N_HEADS * HEAD_DIM
    ng = seq // (n * t)
    qgroup = pl.BlockSpec((1, n * t, HEAD_DIM), lambda bi, h, g: (bi, g, h))
    qnext = pl.BlockSpec((1, n * t, HEAD_DIM), lambda bi, h, g: (bi, jnp.minimum(g + 1, ng - 1), h))
    full = lambda w: pl.BlockSpec((1, seq, w), lambda bi, h, g: (bi, 0, h))
    mtile = lambda w: pl.BlockSpec((1, N_META, w), lambda bi, h, g: (bi, 0, h))
    rows = 2 * n * t
    return pl.pallas_call(
        functools.partial(_diff_attn_kernel, lam_init),
        grid=(b, N_HEADS, ng),
        in_specs=[qgroup, qnext, full(HEAD_DIM), full(2 * HEAD_DIM), mtile(HEAD_DIM), mtile(HEAD_DIM), mtile(2 * HEAD_DIM),
                  _const_spec(lq1.shape), _const_spec(lk1.shape), _const_spec(lq2.shape),
                  _const_spec(lk2.shape), _const_spec(subln_g.shape)],
        out_specs=[qgroup, mtile(HEAD_DIM)],
        out_shape=[jax.ShapeDtypeStruct((b, seq, width), BF16),
                   jax.ShapeDtypeStruct((b, N_META, width), BF16)],
        scratch_shapes=[pltpu.VMEM((rows, HEAD_DIM), BF16), pltpu.VMEM((rows, HEAD_DIM), BF16),
                        pltpu.VMEM((rows, LANES), F32), pltpu.VMEM((rows, 2 * HEAD_DIM), F32),
                        pltpu.VMEM((rows, t), F32), pltpu.VMEM((rows, t), F32), pltpu.VMEM((rows, t), F32)],
        compiler_params=pltpu.CompilerParams(
            dimension_semantics=("arbitrary", "arbitrary", "arbitrary"),
            vmem_limit_bytes=VMEM_LIMIT_BYTES),
        name="diff_attn",
    )(q, q, k, v, qm, km, vm, lq1, lk1, lq2, lk2, subln_g)


def _mix_out_kernel(x_ref, ya_ref, yb_ref, meta_ref, yam_ref, ybm_ref, wo_ref, g_ref,
                    wup_ref, cw_ref, cb_ref, wdown_ref, out_ref, nperm_ref, next_ref, act_ref, operm_ref):
    tm = x_ref.shape[1]
    d = x_ref.shape[2]
    pool_w = ya_ref.shape[-1]
    d_ff = wdown_ref.shape[0]
    n_slabs = d // LANES
    nv = tm // SUBLANES

    def mix_residual(rows, ya, yb):
        h1 = (rows + jnp.dot(ya, wo_ref[:pool_w, :], preferred_element_type=F32)
              + jnp.dot(yb, wo_ref[pool_w:, :], preferred_element_type=F32))
        return h1, _rmsnorm(h1, g_ref[...])

    @pl.when(pl.program_id(1) == 0)
    def _():
        _, nm = mix_residual(meta_ref[...], yam_ref[0], ybm_ref[0])
        halo = jnp.concatenate([nm[N_META - 9:N_META - 1], nm[N_META - 8:]], axis=0)
        for c in range(n_slabs):
            nperm_ref[c, 0:HALO, :] = halo[:, c * LANES:(c + 1) * LANES]

    h1, n2 = mix_residual(x_ref[0], ya_ref[0], yb_ref[0])
    for s in range(SUBLANES):
        for v0 in range(0, nv, SUBLANES):
            t0 = s * nv + v0
            for c in range(n_slabs):
                nperm_ref[c, pl.ds(HALO + v0 * SUBLANES + s, SUBLANES, stride=SUBLANES), :] = (
                    n2[t0:t0 + SUBLANES, c * LANES:(c + 1) * LANES])
    next_ref[...] = jnp.concatenate([nperm_ref[c] for c in range(n_slabs)], axis=1).astype(BF16)

    def conv(u, cols):
        cur = u[HALO:]
        wrap1 = jnp.concatenate([u[HALO - 1:HALO], u[HALO + tm - 8:HALO + tm - 1]], axis=0)
        wrap2 = jnp.concatenate([u[HALO - 9:HALO - 8], u[HALO + tm - 16:HALO + tm - 9]], axis=0)
        back1 = jnp.concatenate([wrap1, u[HALO:HALO + tm - 8]], axis=0)
        back2 = jnp.concatenate([wrap2, wrap1, u[HALO:HALO + tm - 16]], axis=0)
        return (cb_ref[:, cols] + back2 * cw_ref[0:1, cols] + back1 * cw_ref[1:2, cols]
                + cur * cw_ref[2:3, cols])

    for c in range(d_ff // FF_CHUNK):
        gcols = slice(c * FF_CHUNK, (c + 1) * FF_CHUNK)
        vcols = slice(d_ff + c * FF_CHUNK, d_ff + (c + 1) * FF_CHUNK)
        ug = jnp.dot(next_ref[...], wup_ref[:, gcols], preferred_element_type=F32)
        uv = jnp.dot(next_ref[...], wup_ref[:, vcols], preferred_element_type=F32)
        gate = conv(ug, gcols)
        act_ref[:, gcols] = (gate * jax.nn.sigmoid(gate) * conv(uv, vcols)).astype(BF16)
    ffn = jnp.dot(act_ref[...], wdown_ref[...], preferred_element_type=F32)
    for c in range(n_slabs):
        operm_ref[c] = ffn[:, c * LANES:(c + 1) * LANES]
        nperm_ref[c, 0:HALO, :] = nperm_ref[c, tm:tm + HALO, :]
    for s in range(SUBLANES):
        for v0 in range(0, nv, SUBLANES):
            t0 = s * nv + v0
            for c in range(n_slabs):
                lanes = slice(c * LANES, (c + 1) * LANES)
                out_ref[0, t0:t0 + SUBLANES, lanes] = (
                    h1[t0:t0 + SUBLANES, lanes]
                    + operm_ref[c, pl.ds(v0 * SUBLANES + s, SUBLANES, stride=SUBLANES), :])


def _mix_out(x, ya, yb, meta, yam, ybm, w_out, g, w_up, conv_w, conv_b, w_down):
    b, seq, d = x.shape
    pool_w = ya.shape[-1]
    tm = SEQ_TILE
    d_ff = w_down.shape[1]
    assert seq % tm == 0 and d_ff % FF_CHUNK == 0
    tile = lambda w: pl.BlockSpec((1, tm, w), lambda bi, i: (bi, i, 0))
    mtile = lambda w: pl.BlockSpec((1, N_META, w), lambda bi, i: (bi, 0, 0))
    return pl.pallas_call(
        _mix_out_kernel,
        grid=(b, seq // tm),
        in_specs=[tile(d), tile(pool_w), tile(yb.shape[-1]), _const_spec(meta.shape),
                  mtile(pool_w), mtile(ybm.shape[-1]), _layer_const_spec(w_out.shape), _const_spec(g.shape),
                  _layer_const_spec(w_up.shape), _const_spec(conv_w.shape), _const_spec(conv_b.shape),
                  _layer_const_spec(w_down.shape)],
        out_specs=tile(d),
        out_shape=jax.ShapeDtypeStruct((b, seq, d), x.dtype),
        scratch_shapes=[pltpu.VMEM((d // LANES, HALO + tm, LANES), F32), pltpu.VMEM((HALO + tm, d), BF16),
                        pltpu.VMEM((tm, d_ff), BF16), pltpu.VMEM((d // LANES, tm, LANES), F32)],
        compiler_params=pltpu.CompilerParams(
            dimension_semantics=("arbitrary", "arbitrary"), vmem_limit_bytes=VMEM_LIMIT_BYTES),
        name="mix_out",
    )(x, ya, yb, meta, yam, ybm, w_out, g, w_up, conv_w, conv_b, w_down)


def kernel(x, meta_tokens, norm_mix_g, w_in, w_pool, b_pool, pool_scale, q_norm_g, k_norm_g,
           lambda_q1, lambda_k1, lambda_q2, lambda_k2, subln_g, w_out, norm_ffn_g,
           w_up, conv_w, conv_b, w_down):
    depth = w_in.shape[0]
    assert depth == 1 and meta_tokens.shape[0] == N_META
    row = lambda a: a.reshape(1, -1).astype(F32)
    h = x
    meta = meta_tokens.astype(x.dtype)
    for i in range(depth):
        lam_init = 0.8 - 0.6 * math.exp(-0.3 * i)
        qg = jnp.tile(row(q_norm_g[i]), (1, 2)) * (QK_DIM ** -0.5 * math.log2(math.e))
        kg = jnp.tile(row(k_norm_g[i]), (1, 2))
        layer = slice(i, i + 1)
        (ya, q, k, v, yam, qm, km, vm), (w_out_n, w_up_n, w_down_n) = _mix_in(
            h, meta, row(norm_mix_g[i]), w_in[layer], w_pool[layer],
            row(b_pool[i]), row(pool_scale[i]), qg, kg, (w_out[layer], w_up[layer], w_down[layer]))
        yb, ybm = _diff_attn(q, k, v, qm, km, vm, row(lambda_q1[i]), row(lambda_k1[i]),
                             row(lambda_q2[i]), row(lambda_k2[i]), row(subln_g[i]), lam_init)
        h = _mix_out(h, ya, yb, meta, yam, ybm, w_out_n, row(norm_ffn_g[i]),
                     w_up_n, conv_w[i].astype(F32), row(conv_b[i]), w_down_n)
    return h
```

```python
import functools
import math

import jax
import jax.numpy as jnp
from jax import lax
from jax.experimental import pallas as pl
from jax.experimental.pallas import tpu as pltpu

F32 = jnp.float32
BF16 = jnp.bfloat16

N_META = 16
POOL_WINDOWS = (2, 4, 8, 16)
N_HEADS = 4
HEAD_DIM = 128
QK_DIM = 64
CONV_WIDTH = 3
EPS = 1e-6

LANES = 128
SUBLANES = 8
HALO = 16
VMEM_LIMIT_BYTES = 56 * 1024 * 1024

SEQ_TILE = 512
ATTN_TILE = 512
ATTN_SUBTILES = 2
FF_CHUNK = 256
BF16_SUBLANES = 16
N_LATER_WEIGHTS = 3


def _rmsnorm(x, g):
    return x * lax.rsqrt(jnp.mean(x * x, axis=-1, keepdims=True) + EPS) * g


def _const_spec(shape):
    nd = len(shape)
    return pl.BlockSpec(shape, lambda *_: (0,) * nd, pipeline_mode=pl.Buffered(1))


def _layer_const_spec(shape):
    nd = len(shape)
    return pl.BlockSpec((None,) + tuple(shape[1:]), lambda *_: (0,) * nd, pipeline_mode=pl.Buffered(1))


def _mix_in_kernel(x_ref, meta_ref, g_ref, win_ref, wpool_ref, bpool_ref, pscale_ref,
                   qg_ref, kg_ref, *refs):
    later_w = refs[:N_LATER_WEIGHTS]
    ya_ref, q_ref, k_ref, v_ref, yam_ref, qm_ref, km_ref, vm_ref = refs[N_LATER_WEIGHTS:N_LATER_WEIGHTS + 8]
    later_w_bf16 = refs[N_LATER_WEIGHTS + 8:2 * N_LATER_WEIGHTS + 8]
    uext_ref, win_bf16_ref = refs[-2:]
    pool_w = bpool_ref.shape[-1]
    qk_w = q_ref.shape[-1]

    for src, dst in zip(later_w, later_w_bf16):
        dst[...] = src[...].astype(dst.dtype)

    @pl.when((pl.program_id(0) == 0) & (pl.program_id(1) == 0))
    def _():
        win_bf16_ref[...] = win_ref[...].astype(BF16)

    def heads(rows, is_meta, ya_out, q_out, k_out, v_out):
        r = rows.shape[0]
        inv_rms = lax.rsqrt(jnp.mean(rows * rows, axis=-1, keepdims=True) + EPS)
        proj = inv_rms * jnp.dot((rows * g_ref[...]).astype(BF16), win_bf16_ref[...],
                                 preferred_element_type=F32)
        u = proj[:, :pool_w]
        uext_ref[HALO:HALO + r, :] = u
        for gi, w in enumerate(POOL_WINDOWS):
            cols = slice(gi * LANES, (gi + 1) * LANES)
            win_sum = u[:, cols]
            for j in range(1, w):
                win_sum = win_sum + uext_ref[HALO - j:HALO - j + r, cols]
            if is_meta:
                pos = lax.broadcasted_iota(jnp.int32, (r, 1), 0)
                mean = win_sum / jnp.minimum(pos + 1, w).astype(F32)
            else:
                mean = win_sum * (1.0 / w)
            pooled = mean - u[:, cols]
            y = jnp.dot(pooled.astype(BF16), wpool_ref[gi].astype(BF16), preferred_element_type=F32)
            y = (y + bpool_ref[:, cols]) * pscale_ref[:, cols]
            ya_out[0, :, cols] = y.astype(ya_out.dtype)
        uext_ref[0:HALO, :] = uext_ref[r:r + HALO, :]

        lane = lax.broadcasted_iota(jnp.int32, (r, LANES), 1)
        lo = lane < QK_DIM
        for src, gain_ref, out in ((proj[:, pool_w:pool_w + qk_w], qg_ref, q_out),
                                   (proj[:, pool_w + qk_w:pool_w + 2 * qk_w], kg_ref, k_out)):
            for h in range(N_HEADS):
                cols = slice(h * LANES, (h + 1) * LANES)
                slab = src[:, cols]
                sq = slab * slab
                s_lo = jnp.sum(jnp.where(lo, sq, 0.0), axis=-1, keepdims=True)
                s_hi = jnp.sum(jnp.where(lo, 0.0, sq), axis=-1, keepdims=True)
                ms = jnp.where(lo, s_lo, s_hi) * (1.0 / QK_DIM)
                out[0, :, cols] = (slab * lax.rsqrt(ms + EPS) * gain_ref[...]).astype(out.dtype)
        for h in range(N_HEADS):
            vcols = slice(pool_w + 2 * qk_w + h * HEAD_DIM, pool_w + 2 * qk_w + (h + 1) * HEAD_DIM)
            v_out[0, :, 2 * h * HEAD_DIM:(2 * h + 1) * HEAD_DIM] = proj[:, vcols].astype(v_out.dtype)
            v_out[0, :, (2 * h + 1) * HEAD_DIM:(2 * h + 2) * HEAD_DIM] = jnp.ones((r, HEAD_DIM), v_out.dtype)

    @pl.when(pl.program_id(1) == 0)
    def _():
        uext_ref[0:HALO, :] = jnp.zeros((HALO, pool_w), F32)
        heads(meta_ref[...], True, yam_ref, qm_ref, km_ref, vm_ref)

    heads(x_ref[0], False, ya_ref, q_ref, k_ref, v_ref)


def _mix_in(x, meta, g, w_in, w_pool, b_pool, pool_scale, qg, kg, later_weights):
    b, seq, d = x.shape
    pool_w = b_pool.shape[-1]
    qk_w = N_HEADS * HEAD_DIM
    tm = SEQ_TILE
    nt = seq // tm
    assert seq % tm == 0 and len(later_weights) == N_LATER_WEIGHTS
    tile = lambda w: pl.BlockSpec((1, tm, w), lambda bi, i: (bi, i, 0))
    mtile = lambda w: pl.BlockSpec((1, N_META, w), lambda bi, i: (bi, 0, 0))
    big = lambda w: jax.ShapeDtypeStruct((b, seq, w), BF16)
    small = lambda w: jax.ShapeDtypeStruct((b, N_META, w), BF16)
    def wblock(w):
        rows, steps = w.shape[1], b * nt
        hold = next(h for h in (1, 2, 4, 8) if (rows * h) % (steps * BF16_SUBLANES) == 0)
        return pl.BlockSpec((None, rows * hold // steps, w.shape[2]), lambda bi, i: (0, (bi * nt + i) // hold, 0))
    outs = pl.pallas_call(
        _mix_in_kernel,
        grid=(b, nt),
        in_specs=[tile(d), _const_spec(meta.shape), _const_spec(g.shape), _layer_const_spec(w_in.shape),
                  _layer_const_spec(w_pool.shape), _const_spec(b_pool.shape), _const_spec(pool_scale.shape),
                  _const_spec(qg.shape), _const_spec(kg.shape)] + [wblock(w) for w in later_weights],
        out_specs=[tile(pool_w), tile(qk_w), tile(qk_w), tile(2 * qk_w),
                   mtile(pool_w), mtile(qk_w), mtile(qk_w), mtile(2 * qk_w)]
                  + [wblock(w) for w in later_weights],
        out_shape=[big(pool_w), big(qk_w), big(qk_w), big(2 * qk_w),
                   small(pool_w), small(qk_w), small(qk_w), small(2 * qk_w)]
                  + [jax.ShapeDtypeStruct(w.shape, BF16) for w in later_weights],
        scratch_shapes=[pltpu.VMEM((HALO + tm, pool_w), F32), pltpu.VMEM(w_in.shape[1:], BF16)],
        compiler_params=pltpu.CompilerParams(
            dimension_semantics=("arbitrary", "arbitrary"), vmem_limit_bytes=VMEM_LIMIT_BYTES),
        name="mix_in",
    )(x, meta, g, w_in, w_pool, b_pool, pool_scale, qg, kg, *later_weights)
    return outs[:8], outs[8:]


def _stack_subheads(q, sub):
    lane = lax.broadcasted_iota(jnp.int32, (sub, q.shape[1]), 1)
    zero = jnp.zeros((sub, q.shape[1]), q.dtype)
    parts = []
    for r0 in range(0, q.shape[0], sub):
        part = q[r0:r0 + sub]
        parts += [jnp.where(lane < QK_DIM, part, zero), jnp.where(lane < QK_DIM, zero, part)]
    return jnp.concatenate(parts, axis=0)


def _scores(qs, k):
    return lax.dot_general(qs, k, (((1,), (1,)), ((), ())), preferred_element_type=F32)


def _causal_mask(s, masked_rows=None):
    sub = s.shape[1]
    row = lax.broadcasted_iota(jnp.int32, s.shape, 0)
    col = lax.broadcasted_iota(jnp.int32, s.shape, 1)
    visible = col <= (row & (sub - 1))
    if masked_rows is not None and masked_rows < s.shape[0]:
        visible = visible | (row >= masked_rows)
    return jnp.where(visible, s, -jnp.inf)


def _online_softmax_step(s, v_ext, m_ref, acc_ref):
    width = s.shape[1]
    m = m_ref[...]
    m_new = jnp.maximum(m, jnp.max(s, axis=-1, keepdims=True))
    alpha = jnp.exp2(m - m_new)
    m_wide = jnp.concatenate([m_new] * (width // LANES), axis=1) if width > LANES else m_new[:, :width]
    p = jnp.exp2(s - m_wide)
    acc_ref[...] = (jnp.concatenate([alpha, alpha], axis=1) * acc_ref[...]
                    + jnp.dot(p.astype(v_ext.dtype), v_ext, preferred_element_type=F32))
    m_ref[...] = m_new


def _init_state(m_ref, acc_ref):
    m_ref[...] = jnp.full(m_ref.shape, -jnp.inf, F32)
    acc_ref[...] = jnp.zeros(acc_ref.shape, F32)


def _first_softmax_step(s, v_ext, m_ref, acc_ref):
    m = jnp.max(s, axis=-1, keepdims=True)
    p = jnp.exp2(s - m)
    m_ref[...] = jnp.broadcast_to(m, m_ref.shape)
    acc_ref[...] = jnp.dot(p.astype(v_ext.dtype), v_ext, preferred_element_type=F32)


def _diff_attn_kernel(lam_init, q_ref, qnext_ref, k_ref, v_ref, qm_ref, km_ref, vm_ref,
                      lq1_ref, lk1_ref, lq2_ref, lk2_ref, sg_ref, yb_ref, ybm_ref,
                      qs_ref, qn_ref, m_ref, acc_ref, sa_ref, sb_ref, sc_ref):
    g = pl.program_id(2)
    n = ATTN_SUBTILES
    t = q_ref.shape[1] // n
    lam = (jnp.exp(jnp.sum(lq1_ref[...] * lk1_ref[...], axis=-1, keepdims=True))
           - jnp.exp(jnp.sum(lq2_ref[...] * lk2_ref[...], axis=-1, keepdims=True)) + lam_init)

    def finish(row0, r, out_ref, out_rows):
        o = (acc_ref[row0:row0 + r, :HEAD_DIM] / acc_ref[row0:row0 + r, HEAD_DIM:]
             - lam * (acc_ref[row0 + t:row0 + t + r, :HEAD_DIM] / acc_ref[row0 + t:row0 + t + r, HEAD_DIM:]))
        out_ref[0, out_rows, :] = (_rmsnorm(o, sg_ref[...]) * (1.0 - lam_init)).astype(out_ref.dtype)

    km = km_ref[0]
    vm = vm_ref[0]

    @pl.when(g == 0)
    def _():
        qm = _stack_subheads(qm_ref[0], N_META)
        for half in range(2):
            rows = slice(half * t, half * t + N_META)
            _init_state(m_ref.at[rows], acc_ref.at[rows])
        sm = _causal_mask(_scores(qm, km))
        for half in range(2):
            rows = slice(half * t, half * t + N_META)
            _online_softmax_step(sm[half * N_META:(half + 1) * N_META], vm,
                                 m_ref.at[rows], acc_ref.at[rows])
        finish(0, N_META, ybm_ref, slice(0, N_META))

    def kv_tile(ref, j):
        return ref[0, pl.ds(pl.multiple_of(j * t, t), t), :]

    def step(cur_ref, nxt_ref, j):
        nxt_ref[...] = _scores(qs_ref[...], kv_tile(k_ref, j + 1))
        _online_softmax_step(cur_ref[...], kv_tile(v_ref, j), m_ref, acc_ref)

    def diagonal_steps(cur_ref, nxt_ref):
        for c in range(n):
            rows = slice(2 * t * c, 2 * t * n)
            n_rows = 2 * t * (n - c)
            if c + 1 < n:
                nxt_ref[0:n_rows - 2 * t, :] = _scores(qs_ref[2 * t * (c + 1):2 * t * n, :],
                                                       kv_tile(k_ref, n * g + c + 1))
            else:
                qn_ref[...] = _stack_subheads(qnext_ref[0], t)
                sc_ref[...] = _scores(qn_ref[...], kv_tile(k_ref, 0))
            _online_softmax_step(_causal_mask(cur_ref[0:n_rows, :], masked_rows=2 * t), kv_tile(v_ref, n * g + c),
                                 m_ref.at[rows], acc_ref.at[rows])
            finish(2 * t * c, t, yb_ref, slice(t * c, t * (c + 1)))
            cur_ref, nxt_ref = nxt_ref, cur_ref

    @pl.when(g == 0)
    def _():
        qs_ref[...] = _stack_subheads(q_ref[0], t)
        sa_ref[...] = _scores(qs_ref[...], kv_tile(k_ref, 0))
        _first_softmax_step(_scores(qs_ref[...], km), vm, m_ref, acc_ref)
        diagonal_steps(sa_ref, sb_ref)

    @pl.when(g > 0)
    def _():
        qs_ref[...] = qn_ref[...]
        _first_softmax_step(_scores(qs_ref[...], km), vm, m_ref, acc_ref)
        step(sc_ref, sa_ref, 0)

        n_quads = (n * g - 1) // 4

        @pl.loop(0, n_quads)
        def _(jj):
            step(sa_ref, sb_ref, 4 * jj + 1)
            step(sb_ref, sa_ref, 4 * jj + 2)
            step(sa_ref, sb_ref, 4 * jj + 3)
            step(sb_ref, sa_ref, 4 * jj + 4)

        @pl.when((n * g - 1) % 4 >= 2)
        def _():
            step(sa_ref, sb_ref, 4 * n_quads + 1)
            step(sb_ref, sa_ref, 4 * n_quads + 2)

        step(sa_ref, sb_ref, n * g - 1)
        diagonal_steps(sb_ref, sa_ref)


def _diff_attn(q, k, v, qm, km, vm, lq1, lk1, lq2, lk2, subln_g, lam_init):
    b, seq, width = q.shape
    t, n = ATTN_TILE, ATTN_SUBTILES
    assert seq % (n * t) == 0 and n % 2 == 0 and width == N_HEADS * HEAD_DIM
    ng = seq // (n * t)
    qgroup = pl.BlockSpec((1, n * t, HEAD_DIM), lambda bi, h, g: (bi, g, h))
    qnext = pl.BlockSpec((1, n * t, HEAD_DIM), lambda bi, h, g: (bi, jnp.minimum(g + 1, ng - 1), h))
    full = lambda w: pl.BlockSpec((1, seq, w), lambda bi, h, g: (bi, 0, h))
    mtile = lambda w: pl.BlockSpec((1, N_META, w), lambda bi, h, g: (bi, 0, h))
    rows = 2 * n * t
    return pl.pallas_call(
        functools.partial(_diff_attn_kernel, lam_init),
        grid=(b, N_HEADS, ng),
        in_specs=[qgroup, qnext, full(HEAD_DIM), full(2 * HEAD_DIM), mtile(HEAD_DIM), mtile(HEAD_DIM), mtile(2 * HEAD_DIM),
                  _const_spec(lq1.shape), _const_spec(lk1.shape), _const_spec(lq2.shape),
                  _const_spec(lk2.shape), _const_spec(subln_g.shape)],
        out_specs=[qgroup, mtile(HEAD_DIM)],
        out_shape=[jax.ShapeDtypeStruct((b, seq, width), BF16),
                   jax.ShapeDtypeStruct((b, N_META, width), BF16)],
        scratch_shapes=[pltpu.VMEM((rows, HEAD_DIM), BF16), pltpu.VMEM((rows, HEAD_DIM), BF16),
                        pltpu.VMEM((rows, LANES), F32), pltpu.VMEM((rows, 2 * HEAD_DIM), F32),
                        pltpu.VMEM((rows, t), F32), pltpu.VMEM((rows, t), F32), pltpu.VMEM((rows, t), F32)],
        compiler_params=pltpu.CompilerParams(
            dimension_semantics=("arbitrary", "arbitrary", "arbitrary"),
            vmem_limit_bytes=VMEM_LIMIT_BYTES),
        name="diff_attn",
    )(q, q, k, v, qm, km, vm, lq1, lk1, lq2, lk2, subln_g)


def _mix_out_kernel(x_ref, ya_ref, yb_ref, meta_ref, yam_ref, ybm_ref, wo_ref, g_ref,
                    wup_ref, cw_ref, cb_ref, wdown_ref, out_ref, nperm_ref, next_ref, act_ref, operm_ref):
    tm = x_ref.shape[1]
    d = x_ref.shape[2]
    pool_w = ya_ref.shape[-1]
    d_ff = wdown_ref.shape[0]
    n_slabs = d // LANES
    nv = tm // SUBLANES

    def mix_residual(rows, ya, yb):
        h1 = (rows + jnp.dot(ya, wo_ref[:pool_w, :], preferred_element_type=F32)
              + jnp.dot(yb, wo_ref[pool_w:, :], preferred_element_type=F32))
        return h1, _rmsnorm(h1, g_ref[...])

    @pl.when(pl.program_id(1) == 0)
    def _():
        _, nm = mix_residual(meta_ref[...], yam_ref[0], ybm_ref[0])
        halo = jnp.concatenate([nm[N_META - 9:N_META - 1], nm[N_META - 8:]], axis=0)
        for c in range(n_slabs):
            nperm_ref[c, 0:HALO, :] = halo[:, c * LANES:(c + 1) * LANES]

    h1, n2 = mix_residual(x_ref[0], ya_ref[0], yb_ref[0])
    for s in range(SUBLANES):
        for v0 in range(0, nv, SUBLANES):
            t0 = s * nv + v0
            for c in range(n_slabs):
                nperm_ref[c, pl.ds(HALO + v0 * SUBLANES + s, SUBLANES, stride=SUBLANES), :] = (
                    n2[t0:t0 + SUBLANES, c * LANES:(c + 1) * LANES])
    next_ref[...] = jnp.concatenate([nperm_ref[c] for c in range(n_slabs)], axis=1).astype(BF16)

    def conv(u, cols):
        cur = u[HALO:]
        wrap1 = jnp.concatenate([u[HALO - 1:HALO], u[HALO + tm - 8:HALO + tm - 1]], axis=0)
        wrap2 = jnp.concatenate([u[HALO - 9:HALO - 8], u[HALO + tm - 16:HALO + tm - 9]], axis=0)
        back1 = jnp.concatenate([wrap1, u[HALO:HALO + tm - 8]], axis=0)
        back2 = jnp.concatenate([wrap2, wrap1, u[HALO:HALO + tm - 16]], axis=0)
        return (cb_ref[:, cols] + back2 * cw_ref[0:1, cols] + back1 * cw_ref[1:2, cols]
                + cur * cw_ref[2:3, cols])

    for c in range(d_ff // FF_CHUNK):
        gcols = slice(c * FF_CHUNK, (c + 1) * FF_CHUNK)
        vcols = slice(d_ff + c * FF_CHUNK, d_ff + (c + 1) * FF_CHUNK)
        ug = jnp.dot(next_ref[...], wup_ref[:, gcols], preferred_element_type=F32)
        uv = jnp.dot(next_ref[...], wup_ref[:, vcols], preferred_element_type=F32)
        gate = conv(ug, gcols)
        act_ref[:, gcols] = (gate * jax.nn.sigmoid(gate) * conv(uv, vcols)).astype(BF16)
    ffn = jnp.dot(act_ref[...], wdown_ref[...], preferred_element_type=F32)
    for c in range(n_slabs):
        operm_ref[c] = ffn[:, c * LANES:(c + 1) * LANES]
        nperm_ref[c, 0:HALO, :] = nperm_ref[c, tm:tm + HALO, :]
    for s in range(SUBLANES):
        for v0 in range(0, nv, SUBLANES):
            t0 = s * nv + v0
            for c in range(n_slabs):
                lanes = slice(c * LANES, (c + 1) * LANES)
                out_ref[0, t0:t0 + SUBLANES, lanes] = (
                    h1[t0:t0 + SUBLANES, lanes]
                    + operm_ref[c, pl.ds(v0 * SUBLANES + s, SUBLANES, stride=SUBLANES), :])


def _mix_out(x, ya, yb, meta, yam, ybm, w_out, g, w_up, conv_w, conv_b, w_down):
    b, seq, d = x.shape
    pool_w = ya.shape[-1]
    tm = SEQ_TILE
    d_ff = w_down.shape[1]
    assert seq % tm == 0 and d_ff % FF_CHUNK == 0
    tile = lambda w: pl.BlockSpec((1, tm, w), lambda bi, i: (bi, i, 0))
    mtile = lambda w: pl.BlockSpec((1, N_META, w), lambda bi, i: (bi, 0, 0))
    return pl.pallas_call(
        _mix_out_kernel,
        grid=(b, seq // tm),
        in_specs=[tile(d), tile(pool_w), tile(yb.shape[-1]), _const_spec(meta.shape),
                  mtile(pool_w), mtile(ybm.shape[-1]), _layer_const_spec(w_out.shape), _const_spec(g.shape),
                  _layer_const_spec(w_up.shape), _const_spec(conv_w.shape), _const_spec(conv_b.shape),
                  _layer_const_spec(w_down.shape)],
        out_specs=tile(d),
        out_shape=jax.ShapeDtypeStruct((b, seq, d), x.dtype),
        scratch_shapes=[pltpu.VMEM((d // LANES, HALO + tm, LANES), F32), pltpu.VMEM((HALO + tm, d), BF16),
                        pltpu.VMEM((tm, d_ff), BF16), pltpu.VMEM((d // LANES, tm, LANES), F32)],
        compiler_params=pltpu.CompilerParams(
            dimension_semantics=("arbitrary", "arbitrary"), vmem_limit_bytes=VMEM_LIMIT_BYTES),
        name="mix_out",
    )(x, ya, yb, meta, yam, ybm, w_out, g, w_up, conv_w, conv_b, w_down)


def kernel(x, meta_tokens, norm_mix_g, w_in, w_pool, b_pool, pool_scale, q_norm_g, k_norm_g,
           lambda_q1, lambda_k1, lambda_q2, lambda_k2, subln_g, w_out, norm_ffn_g,
           w_up, conv_w, conv_b, w_down):
    depth = w_in.shape[0]
    assert depth == 1 and meta_tokens.shape[0] == N_META
    row = lambda a: a.reshape(1, -1).astype(F32)
    h = x
    meta = meta_tokens.astype(x.dtype)
    for i in range(depth):
        lam_init = 0.8 - 0.6 * math.exp(-0.3 * i)
        qg = jnp.tile(row(q_norm_g[i]), (1, 2)) * (QK_DIM ** -0.5 * math.log2(math.e))
        kg = jnp.tile(row(k_norm_g[i]), (1, 2))
        layer = slice(i, i + 1)
        (ya, q, k, v, yam, qm, km, vm), (w_out_n, w_up_n, w_down_n) = _mix_in(
            h, meta, row(norm_mix_g[i]), w_in[layer], w_pool[layer],
            row(b_pool[i]), row(pool_scale[i]), qg, kg, (w_out[layer], w_up[layer], w_down[layer]))
        yb, ybm = _diff_attn(q, k, v, qm, km, vm, row(lambda_q1[i]), row(lambda_k1[i]),
                             row(lambda_q2[i]), row(lambda_k2[i]), row(subln_g[i]), lam_init)
        h = _mix_out(h, ya, yb, meta, yam, ybm, w_out_n, row(norm_ffn_g[i]),
                     w_up_n, conv_w[i].astype(F32), row(conv_b[i]), w_down_n)
    return h
```

```python
import functools
import math

import jax
import jax.numpy as jnp
from jax import lax
from jax.experimental import pallas as pl
from jax.experimental.pallas import tpu as pltpu

F32 = jnp.float32
BF16 = jnp.bfloat16

N_META = 16
POOL_WINDOWS = (2, 4, 8, 16)
N_HEADS = 4
HEAD_DIM = 128
QK_DIM = 64
CONV_WIDTH = 3
EPS = 1e-6

LANES = 128
SUBLANES = 8
HALO = 16
VMEM_LIMIT_BYTES = 56 * 1024 * 1024

IN_TILE = 1024
SEQ_TILE = 512
ATTN_TILE = 512
ATTN_SUBTILES = 2
FF_CHUNK = 256
BF16_SUBLANES = 16
N_LATER_WEIGHTS = 3


def _rmsnorm(x, g):
    return x * lax.rsqrt(jnp.mean(x * x, axis=-1, keepdims=True) + EPS) * g


def _const_spec(shape):
    nd = len(shape)
    return pl.BlockSpec(shape, lambda *_: (0,) * nd, pipeline_mode=pl.Buffered(1))


def _layer_const_spec(shape):
    nd = len(shape)
    return pl.BlockSpec((None,) + tuple(shape[1:]), lambda *_: (0,) * nd, pipeline_mode=pl.Buffered(1))


def _mix_in_kernel(x_ref, meta_ref, g_ref, win_ref, wpool_ref, bpool_ref, pscale_ref,
                   qg_ref, kg_ref, *refs):
    later_w = refs[:N_LATER_WEIGHTS]
    ya_ref, q_ref, k_ref, v_ref, yam_ref, qm_ref, km_ref, vm_ref = refs[N_LATER_WEIGHTS:N_LATER_WEIGHTS + 8]
    later_w_bf16 = refs[N_LATER_WEIGHTS + 8:2 * N_LATER_WEIGHTS + 8]
    uext_ref, win_bf16_ref = refs[-2:]
    pool_w = bpool_ref.shape[-1]
    qk_w = q_ref.shape[-1]

    for src, dst in zip(later_w, later_w_bf16):
        dst[...] = src[...].astype(dst.dtype)

    @pl.when((pl.program_id(0) == 0) & (pl.program_id(1) == 0))
    def _():
        win_bf16_ref[...] = win_ref[...].astype(BF16)

    def heads(rows, is_meta, ya_out, q_out, k_out, v_out):
        r = rows.shape[0]
        inv_rms = lax.rsqrt(jnp.mean(rows * rows, axis=-1, keepdims=True) + EPS)
        proj = inv_rms * jnp.dot((rows * g_ref[...]).astype(BF16), win_bf16_ref[...],
                                 preferred_element_type=F32)
        u = proj[:, :pool_w]
        uext_ref[HALO:HALO + r, :] = u
        for gi, w in enumerate(POOL_WINDOWS):
            cols = slice(gi * LANES, (gi + 1) * LANES)
            win_sum = u[:, cols]
            for j in range(1, w):
                win_sum = win_sum + uext_ref[HALO - j:HALO - j + r, cols]
            if is_meta:
                pos = lax.broadcasted_iota(jnp.int32, (r, 1), 0)
                mean = win_sum / jnp.minimum(pos + 1, w).astype(F32)
            else:
                mean = win_sum * (1.0 / w)
            pooled = mean - u[:, cols]
            y = jnp.dot(pooled.astype(BF16), wpool_ref[gi].astype(BF16), preferred_element_type=F32)
            y = (y + bpool_ref[:, cols]) * pscale_ref[:, cols]
            ya_out[0, :, cols] = y.astype(ya_out.dtype)
        uext_ref[0:HALO, :] = uext_ref[r:r + HALO, :]

        lane = lax.broadcasted_iota(jnp.int32, (r, LANES), 1)
        lo = lane < QK_DIM
        for src, gain_ref, out in ((proj[:, pool_w:pool_w + qk_w], qg_ref, q_out),
                                   (proj[:, pool_w + qk_w:pool_w + 2 * qk_w], kg_ref, k_out)):
            for h in range(N_HEADS):
                cols = slice(h * LANES, (h + 1) * LANES)
                slab = src[:, cols]
                sq = slab * slab
                s_lo = jnp.sum(jnp.where(lo, sq, 0.0), axis=-1, keepdims=True)
                s_hi = jnp.sum(jnp.where(lo, 0.0, sq), axis=-1, keepdims=True)
                ms = jnp.where(lo, s_lo, s_hi) * (1.0 / QK_DIM)
                out[0, :, cols] = (slab * lax.rsqrt(ms + EPS) * gain_ref[...]).astype(out.dtype)
        for h in range(N_HEADS):
            vcols = slice(pool_w + 2 * qk_w + h * HEAD_DIM, pool_w + 2 * qk_w + (h + 1) * HEAD_DIM)
            v_out[0, :, 2 * h * HEAD_DIM:(2 * h + 1) * HEAD_DIM] = proj[:, vcols].astype(v_out.dtype)
            v_out[0, :, (2 * h + 1) * HEAD_DIM:(2 * h + 2) * HEAD_DIM] = jnp.ones((r, HEAD_DIM), v_out.dtype)

    @pl.when(pl.program_id(1) == 0)
    def _():
        uext_ref[0:HALO, :] = jnp.zeros((HALO, pool_w), F32)
        heads(meta_ref[...], True, yam_ref, qm_ref, km_ref, vm_ref)

    heads(x_ref[0], False, ya_ref, q_ref, k_ref, v_ref)


def _mix_in(x, meta, g, w_in, w_pool, b_pool, pool_scale, qg, kg, later_weights):
    b, seq, d = x.shape
    pool_w = b_pool.shape[-1]
    qk_w = N_HEADS * HEAD_DIM
    tm = IN_TILE
    nt = seq // tm
    assert seq % tm == 0 and len(later_weights) == N_LATER_WEIGHTS
    tile = lambda w: pl.BlockSpec((1, tm, w), lambda bi, i: (bi, i, 0))
    mtile = lambda w: pl.BlockSpec((1, N_META, w), lambda bi, i: (bi, 0, 0))
    big = lambda w: jax.ShapeDtypeStruct((b, seq, w), BF16)
    small = lambda w: jax.ShapeDtypeStruct((b, N_META, w), BF16)
    def wblock(w):
        rows, steps = w.shape[1], b * nt
        hold = next(h for h in (1, 2, 4, 8) if (rows * h) % (steps * BF16_SUBLANES) == 0)
        return pl.BlockSpec((None, rows * hold // steps, w.shape[2]), lambda bi, i: (0, (bi * nt + i) // hold, 0))
    outs = pl.pallas_call(
        _mix_in_kernel,
        grid=(b, nt),
        in_specs=[tile(d), _const_spec(meta.shape), _const_spec(g.shape), _layer_const_spec(w_in.shape),
                  _layer_const_spec(w_pool.shape), _const_spec(b_pool.shape), _const_spec(pool_scale.shape),
                  _const_spec(qg.shape), _const_spec(kg.shape)] + [wblock(w) for w in later_weights],
        out_specs=[tile(pool_w), tile(qk_w), tile(qk_w), tile(2 * qk_w),
                   mtile(pool_w), mtile(qk_w), mtile(qk_w), mtile(2 * qk_w)]
                  + [wblock(w) for w in later_weights],
        out_shape=[big(pool_w), big(qk_w), big(qk_w), big(2 * qk_w),
                   small(pool_w), small(qk_w), small(qk_w), small(2 * qk_w)]
                  + [jax.ShapeDtypeStruct(w.shape, BF16) for w in later_weights],
        scratch_shapes=[pltpu.VMEM((HALO + tm, pool_w), F32), pltpu.VMEM(w_in.shape[1:], BF16)],
        compiler_params=pltpu.CompilerParams(
            dimension_semantics=("arbitrary", "arbitrary"), vmem_limit_bytes=VMEM_LIMIT_BYTES),
        name="mix_in",
    )(x, meta, g, w_in, w_pool, b_pool, pool_scale, qg, kg, *later_weights)
    return outs[:8], outs[8:]


def _stack_subheads(q, sub):
    lane = lax.broadcasted_iota(jnp.int32, (sub, q.shape[1]), 1)
    zero = jnp.zeros((sub, q.shape[1]), q.dtype)
    parts = []
    for r0 in range(0, q.shape[0], sub):
        part = q[r0:r0 + sub]
        parts += [jnp.where(lane < QK_DIM, part, zero), jnp.where(lane < QK_DIM, zero, part)]
    return jnp.concatenate(parts, axis=0)


def _scores(qs, k):
    return lax.dot_general(qs, k, (((1,), (1,)), ((), ())), preferred_element_type=F32)


def _causal_mask(s, masked_rows=None):
    sub = s.shape[1]
    row = lax.broadcasted_iota(jnp.int32, s.shape, 0)
    col = lax.broadcasted_iota(jnp.int32, s.shape, 1)
    visible = col <= (row & (sub - 1))
    if masked_rows is not None and masked_rows < s.shape[0]:
        visible = visible | (row >= masked_rows)
    return jnp.where(visible, s, -jnp.inf)


def _online_softmax_step(s, v_ext, m_ref, acc_ref):
    width = s.shape[1]
    m = m_ref[...]
    m_new = jnp.maximum(m, jnp.max(s, axis=-1, keepdims=True))
    alpha = jnp.exp2(m - m_new)
    m_wide = jnp.concatenate([m_new] * (width // LANES), axis=1) if width > LANES else m_new[:, :width]
    p = jnp.exp2(s - m_wide)
    acc_ref[...] = (jnp.concatenate([alpha, alpha], axis=1) * acc_ref[...]
                    + jnp.dot(p.astype(v_ext.dtype), v_ext, preferred_element_type=F32))
    m_ref[...] = m_new


def _init_state(m_ref, acc_ref):
    m_ref[...] = jnp.full(m_ref.shape, -jnp.inf, F32)
    acc_ref[...] = jnp.zeros(acc_ref.shape, F32)


def _first_softmax_step(s, v_ext, m_ref, acc_ref):
    m = jnp.max(s, axis=-1, keepdims=True)
    p = jnp.exp2(s - m)
    m_ref[...] = jnp.broadcast_to(m, m_ref.shape)
    acc_ref[...] = jnp.dot(p.astype(v_ext.dtype), v_ext, preferred_element_type=F32)


def _diff_attn_kernel(lam_init, q_ref, qnext_ref, k_ref, v_ref, qm_ref, km_ref, vm_ref,
                      lq1_ref, lk1_ref, lq2_ref, lk2_ref, sg_ref, yb_ref, ybm_ref,
                      qs_ref, qn_ref, m_ref, acc_ref, sa_ref, sb_ref, sc_ref):
    g = pl.program_id(2)
    n = ATTN_SUBTILES
    t = q_ref.shape[1] // n
    lam = (jnp.exp(jnp.sum(lq1_ref[...] * lk1_ref[...], axis=-1, keepdims=True))
           - jnp.exp(jnp.sum(lq2_ref[...] * lk2_ref[...], axis=-1, keepdims=True)) + lam_init)

    def finish(row0, r, out_ref, out_rows):
        o = (acc_ref[row0:row0 + r, :HEAD_DIM] / acc_ref[row0:row0 + r, HEAD_DIM:]
             - lam * (acc_ref[row0 + t:row0 + t + r, :HEAD_DIM] / acc_ref[row0 + t:row0 + t + r, HEAD_DIM:]))
        out_ref[0, out_rows, :] = (_rmsnorm(o, sg_ref[...]) * (1.0 - lam_init)).astype(out_ref.dtype)

    km = km_ref[0]
    vm = vm_ref[0]

    @pl.when(g == 0)
    def _():
        qm = _stack_subheads(qm_ref[0], N_META)
        for half in range(2):
            rows = slice(half * t, half * t + N_META)
            _init_state(m_ref.at[rows], acc_ref.at[rows])
        sm = _causal_mask(_scores(qm, km))
        for half in range(2):
            rows = slice(half * t, half * t + N_META)
            _online_softmax_step(sm[half * N_META:(half + 1) * N_META], vm,
                                 m_ref.at[rows], acc_ref.at[rows])
        finish(0, N_META, ybm_ref, slice(0, N_META))

    def kv_tile(ref, j):
        return ref[0, pl.ds(pl.multiple_of(j * t, t), t), :]

    def step(cur_ref, nxt_ref, j):
        nxt_ref[...] = _scores(qs_ref[...], kv_tile(k_ref, j + 1))
        _online_softmax_step(cur_ref[...], kv_tile(v_ref, j), m_ref, acc_ref)

    def diagonal_steps(cur_ref, nxt_ref):
        for c in range(n):
            rows = slice(2 * t * c, 2 * t * n)
            n_rows = 2 * t * (n - c)
            if c + 1 < n:
                nxt_ref[0:n_rows - 2 * t, :] = _scores(qs_ref[2 * t * (c + 1):2 * t * n, :],
                                                       kv_tile(k_ref, n * g + c + 1))
            else:
                qn_ref[...] = _stack_subheads(qnext_ref[0], t)
                sc_ref[...] = _scores(qn_ref[...], kv_tile(k_ref, 0))
            _online_softmax_step(_causal_mask(cur_ref[0:n_rows, :], masked_rows=2 * t), kv_tile(v_ref, n * g + c),
                                 m_ref.at[rows], acc_ref.at[rows])
            finish(2 * t * c, t, yb_ref, slice(t * c, t * (c + 1)))
            cur_ref, nxt_ref = nxt_ref, cur_ref

    @pl.when(g == 0)
    def _():
        qs_ref[...] = _stack_subheads(q_ref[0], t)
        sa_ref[...] = _scores(qs_ref[...], kv_tile(k_ref, 0))
        _first_softmax_step(_scores(qs_ref[...], km), vm, m_ref, acc_ref)
        diagonal_steps(sa_ref, sb_ref)

    @pl.when(g > 0)
    def _():
        qs_ref[...] = qn_ref[...]
        _first_softmax_step(_scores(qs_ref[...], km), vm, m_ref, acc_ref)
        step(sc_ref, sa_ref, 0)

        n_quads = (n * g - 1) // 4

        @pl.loop(0, n_quads)
        def _(jj):
            step(sa_ref, sb_ref, 4 * jj + 1)
            step(sb_ref, sa_ref, 4 * jj + 2)
            step(sa_ref, sb_ref, 4 * jj + 3)
            step(sb_ref, sa_ref, 4 * jj + 4)

        @pl.when((n * g - 1) % 4 >= 2)
        def _():
            step(sa_ref, sb_ref, 4 * n_quads + 1)
            step(sb_ref, sa_ref, 4 * n_quads + 2)

        step(sa_ref, sb_ref, n * g - 1)
        diagonal_steps(sb_ref, sa_ref)


def _diff_attn(q, k, v, qm, km, vm, lq1, lk1, lq2, lk2, subln_g, lam_init):
    b, seq, width = q.shape
    t, n = ATTN_TILE, ATTN_SUBTILES
    assert seq % (n * t) == 0 and n % 2 == 0 and width == N_HEADS * HEAD_DIM
    ng = seq // (n * t)
    qgroup = pl.BlockSpec((1, n * t, HEAD_DIM), lambda bi, h, g: (bi, g, h))
    qnext = pl.BlockSpec((1, n * t, HEAD_DIM), lambda bi, h, g: (bi, jnp.minimum(g + 1, ng - 1), h))
    full = lambda w: pl.BlockSpec((1, seq, w), lambda bi, h, g: (bi, 0, h))
    mtile = lambda w: pl.BlockSpec((1, N_META, w), lambda bi, h, g: (bi, 0, h))
    rows = 2 * n * t
    return pl.pallas_call(
        functools.partial(_diff_attn_kernel, lam_init),
        grid=(b, N_HEADS, ng),
        in_specs=[qgroup, qnext, full(HEAD_DIM), full(2 * HEAD_DIM), mtile(HEAD_DIM), mtile(HEAD_DIM), mtile(2 * HEAD_DIM),
                  _const_spec(lq1.shape), _const_spec(lk1.shape), _const_spec(lq2.shape),
                  _const_spec(lk2.shape), _const_spec(subln_g.shape)],
        out_specs=[qgroup, mtile(HEAD_DIM)],
        out_shape=[jax.ShapeDtypeStruct((b, seq, width), BF16),
                   jax.ShapeDtypeStruct((b, N_META, width), BF16)],
        scratch_shapes=[pltpu.VMEM((rows, HEAD_DIM), BF16), pltpu.VMEM((rows, HEAD_DIM), BF16),
                        pltpu.VMEM((rows, LANES), F32), pltpu.VMEM((rows, 2 * HEAD_DIM), F32),
                        pltpu.VMEM((rows, t), F32), pltpu.VMEM((rows, t), F32), pltpu.VMEM((rows, t), F32)],
        compiler_params=pltpu.CompilerParams(
            dimension_semantics=("arbitrary", "arbitrary", "arbitrary"),
            vmem_limit_bytes=VMEM_LIMIT_BYTES),
        name="diff_attn",
    )(q, q, k, v, qm, km, vm, lq1, lk1, lq2, lk2, subln_g)


def _mix_out_kernel(x_ref, ya_ref, yb_ref, meta_ref, yam_ref, ybm_ref, wo_ref, g_ref,
                    wup_ref, cw_ref, cb_ref, wdown_ref, out_ref, nperm_ref, next_ref, act_ref, operm_ref):
    tm = x_ref.shape[1]
    d = x_ref.shape[2]
    pool_w = ya_ref.shape[-1]
    d_ff = wdown_ref.shape[0]
    n_slabs = d // LANES
    nv = tm // SUBLANES

    def mix_residual(rows, ya, yb):
        h1 = (rows + jnp.dot(ya, wo_ref[:pool_w, :], preferred_element_type=F32)
              + jnp.dot(yb, wo_ref[pool_w:, :], preferred_element_type=F32))
        return h1, _rmsnorm(h1, g_ref[...])

    @pl.when(pl.program_id(1) == 0)
    def _():
        _, nm = mix_residual(meta_ref[...], yam_ref[0], ybm_ref[0])
        halo = jnp.concatenate([nm[N_META - 9:N_META - 1], nm[N_META - 8:]], axis=0)
        for c in range(n_slabs):
            nperm_ref[c, 0:HALO, :] = halo[:, c * LANES:(c + 1) * LANES]

    h1, n2 = mix_residual(x_ref[0], ya_ref[0], yb_ref[0])
    for s in range(SUBLANES):
        for v0 in range(0, nv, SUBLANES):
            t0 = s * nv + v0
            for c in range(n_slabs):
                nperm_ref[c, pl.ds(HALO + v0 * SUBLANES + s, SUBLANES, stride=SUBLANES), :] = (
                    n2[t0:t0 + SUBLANES, c * LANES:(c + 1) * LANES])
    next_ref[...] = jnp.concatenate([nperm_ref[c] for c in range(n_slabs)], axis=1).astype(BF16)

    def conv(u, cols):
        cur = u[HALO:]
        wrap1 = jnp.concatenate([u[HALO - 1:HALO], u[HALO + tm - 8:HALO + tm - 1]], axis=0)
        wrap2 = jnp.concatenate([u[HALO - 9:HALO - 8], u[HALO + tm - 16:HALO + tm - 9]], axis=0)
        back1 = jnp.concatenate([wrap1, u[HALO:HALO + tm - 8]], axis=0)
        back2 = jnp.concatenate([wrap2, wrap1, u[HALO:HALO + tm - 16]], axis=0)
        return (cb_ref[:, cols] + back2 * cw_ref[0:1, cols] + back1 * cw_ref[1:2, cols]
                + cur * cw_ref[2:3, cols])

    for c in range(d_ff // FF_CHUNK):
        gcols = slice(c * FF_CHUNK, (c + 1) * FF_CHUNK)
        vcols = slice(d_ff + c * FF_CHUNK, d_ff + (c + 1) * FF_CHUNK)
        ug = jnp.dot(next_ref[...], wup_ref[:, gcols], preferred_element_type=F32)
        uv = jnp.dot(next_ref[...], wup_ref[:, vcols], preferred_element_type=F32)
        gate = conv(ug, gcols)
        act_ref[:, gcols] = (gate * jax.nn.sigmoid(gate) * conv(uv, vcols)).astype(BF16)
    ffn = jnp.dot(act_ref[...], wdown_ref[...], preferred_element_type=F32)
    for c in range(n_slabs):
        operm_ref[c] = ffn[:, c * LANES:(c + 1) * LANES]
        nperm_ref[c, 0:HALO, :] = nperm_ref[c, tm:tm + HALO, :]
    for s in range(SUBLANES):
        for v0 in range(0, nv, SUBLANES):
            t0 = s * nv + v0
            for c in range(n_slabs):
                lanes = slice(c * LANES, (c + 1) * LANES)
                out_ref[0, t0:t0 + SUBLANES, lanes] = (
                    h1[t0:t0 + SUBLANES, lanes]
                    + operm_ref[c, pl.ds(v0 * SUBLANES + s, SUBLANES, stride=SUBLANES), :])


def _mix_out(x, ya, yb, meta, yam, ybm, w_out, g, w_up, conv_w, conv_b, w_down):
    b, seq, d = x.shape
    pool_w = ya.shape[-1]
    tm = SEQ_TILE
    d_ff = w_down.shape[1]
    assert seq % tm == 0 and d_ff % FF_CHUNK == 0
    tile = lambda w: pl.BlockSpec((1, tm, w), lambda bi, i: (bi, i, 0))
    mtile = lambda w: pl.BlockSpec((1, N_META, w), lambda bi, i: (bi, 0, 0))
    return pl.pallas_call(
        _mix_out_kernel,
        grid=(b, seq // tm),
        in_specs=[tile(d), tile(pool_w), tile(yb.shape[-1]), _const_spec(meta.shape),
                  mtile(pool_w), mtile(ybm.shape[-1]), _layer_const_spec(w_out.shape), _const_spec(g.shape),
                  _layer_const_spec(w_up.shape), _const_spec(conv_w.shape), _const_spec(conv_b.shape),
                  _layer_const_spec(w_down.shape)],
        out_specs=tile(d),
        out_shape=jax.ShapeDtypeStruct((b, seq, d), x.dtype),
        scratch_shapes=[pltpu.VMEM((d // LANES, HALO + tm, LANES), F32), pltpu.VMEM((HALO + tm, d), BF16),
                        pltpu.VMEM((tm, d_ff), BF16), pltpu.VMEM((d // LANES, tm, LANES), F32)],
        compiler_params=pltpu.CompilerParams(
            dimension_semantics=("arbitrary", "arbitrary"), vmem_limit_bytes=VMEM_LIMIT_BYTES),
        name="mix_out",
    )(x, ya, yb, meta, yam, ybm, w_out, g, w_up, conv_w, conv_b, w_down)


def kernel(x, meta_tokens, norm_mix_g, w_in, w_pool, b_pool, pool_scale, q_norm_g, k_norm_g,
           lambda_q1, lambda_k1, lambda_q2, lambda_k2, subln_g, w_out, norm_ffn_g,
           w_up, conv_w, conv_b, w_down):
    depth = w_in.shape[0]
    assert depth == 1 and meta_tokens.shape[0] == N_META
    row = lambda a: a.reshape(1, -1).astype(F32)
    h = x
    meta = meta_tokens.astype(x.dtype)
    for i in range(depth):
        lam_init = 0.8 - 0.6 * math.exp(-0.3 * i)
        qg = jnp.tile(row(q_norm_g[i]), (1, 2)) * (QK_DIM ** -0.5 * math.log2(math.e))
        kg = jnp.tile(row(k_norm_g[i]), (1, 2))
        layer = slice(i, i + 1)
        (ya, q, k, v, yam, qm, km, vm), (w_out_n, w_up_n, w_down_n) = _mix_in(
            h, meta, row(norm_mix_g[i]), w_in[layer], w_pool[layer],
            row(b_pool[i]), row(pool_scale[i]), qg, kg, (w_out[layer], w_up[layer], w_down[layer]))
        yb, ybm = _diff_attn(q, k, v, qm, km, vm, row(lambda_q1[i]), row(lambda_k1[i]),
                             row(lambda_q2[i]), row(lambda_k2[i]), row(subln_g[i]), lam_init)
        h = _mix_out(h, ya, yb, meta, yam, ybm, w_out_n, row(norm_ffn_g[i]),
                     w_up_n, conv_w[i].astype(F32), row(conv_b[i]), w_down_n)
    return h
```

```python
import functools
import math

import jax
import jax.numpy as jnp
from jax import lax
from jax.experimental import pallas as pl
from jax.experimental.pallas import tpu as pltpu

F32 = jnp.float32
BF16 = jnp.bfloat16

N_META = 16
POOL_WINDOWS = (2, 4, 8, 16)
N_HEADS = 4
HEAD_DIM = 128
QK_DIM = 64
CONV_WIDTH = 3
EPS = 1e-6

LANES = 128
SUBLANES = 8
HALO = 16
VMEM_LIMIT_BYTES = 56 * 1024 * 1024

IN_TILE = 1024
OUT_TILE = 1024
FFN_SUBTILE = 512
ATTN_TILE = 512
ATTN_SUBTILES = 2
FF_CHUNK = 256
BF16_SUBLANES = 16
N_LATER_WEIGHTS = 3


def _rmsnorm(x, g):
    return x * lax.rsqrt(jnp.mean(x * x, axis=-1, keepdims=True) + EPS) * g


def _const_spec(shape):
    nd = len(shape)
    return pl.BlockSpec(shape, lambda *_: (0,) * nd, pipeline_mode=pl.Buffered(1))


def _layer_const_spec(shape):
    nd = len(shape)
    return pl.BlockSpec((None,) + tuple(shape[1:]), lambda *_: (0,) * nd, pipeline_mode=pl.Buffered(1))


def _mix_in_kernel(x_ref, meta_ref, g_ref, win_ref, wpool_ref, bpool_ref, pscale_ref,
                   qg_ref, kg_ref, *refs):
    later_w = refs[:N_LATER_WEIGHTS]
    ya_ref, q_ref, k_ref, v_ref, yam_ref, qm_ref, km_ref, vm_ref = refs[N_LATER_WEIGHTS:N_LATER_WEIGHTS + 8]
    later_w_bf16 = refs[N_LATER_WEIGHTS + 8:2 * N_LATER_WEIGHTS + 8]
    uext_ref, win_bf16_ref = refs[-2:]
    pool_w = bpool_ref.shape[-1]
    qk_w = q_ref.shape[-1]

    for src, dst in zip(later_w, later_w_bf16):
        dst[...] = src[...].astype(dst.dtype)

    @pl.when((pl.program_id(0) == 0) & (pl.program_id(1) == 0))
    def _():
        win_bf16_ref[...] = win_ref[...].astype(BF16)

    def heads(rows, is_meta, ya_out, q_out, k_out, v_out):
        r = rows.shape[0]
        inv_rms = lax.rsqrt(jnp.mean(rows * rows, axis=-1, keepdims=True) + EPS)
        proj = inv_rms * jnp.dot((rows * g_ref[...]).astype(BF16), win_bf16_ref[...],
                                 preferred_element_type=F32)
        u = proj[:, :pool_w]
        uext_ref[HALO:HALO + r, :] = u
        for gi, w in enumerate(POOL_WINDOWS):
            cols = slice(gi * LANES, (gi + 1) * LANES)
            win_sum = u[:, cols]
            for j in range(1, w):
                win_sum = win_sum + uext_ref[HALO - j:HALO - j + r, cols]
            if is_meta:
                pos = lax.broadcasted_iota(jnp.int32, (r, 1), 0)
                mean = win_sum / jnp.minimum(pos + 1, w).astype(F32)
            else:
                mean = win_sum * (1.0 / w)
            pooled = mean - u[:, cols]
            y = jnp.dot(pooled.astype(BF16), wpool_ref[gi].astype(BF16), preferred_element_type=F32)
            y = (y + bpool_ref[:, cols]) * pscale_ref[:, cols]
            ya_out[0, :, cols] = y.astype(ya_out.dtype)
        uext_ref[0:HALO, :] = uext_ref[r:r + HALO, :]

        lane = lax.broadcasted_iota(jnp.int32, (r, LANES), 1)
        lo = lane < QK_DIM
        for src, gain_ref, out in ((proj[:, pool_w:pool_w + qk_w], qg_ref, q_out),
                                   (proj[:, pool_w + qk_w:pool_w + 2 * qk_w], kg_ref, k_out)):
            for h in range(N_HEADS):
                cols = slice(h * LANES, (h + 1) * LANES)
                slab = src[:, cols]
                sq = slab * slab
                s_lo = jnp.sum(jnp.where(lo, sq, 0.0), axis=-1, keepdims=True)
                s_hi = jnp.sum(jnp.where(lo, 0.0, sq), axis=-1, keepdims=True)
                ms = jnp.where(lo, s_lo, s_hi) * (1.0 / QK_DIM)
                out[0, :, cols] = (slab * lax.rsqrt(ms + EPS) * gain_ref[...]).astype(out.dtype)
        for h in range(N_HEADS):
            vcols = slice(pool_w + 2 * qk_w + h * HEAD_DIM, pool_w + 2 * qk_w + (h + 1) * HEAD_DIM)
            v_out[0, :, 2 * h * HEAD_DIM:(2 * h + 1) * HEAD_DIM] = proj[:, vcols].astype(v_out.dtype)
            v_out[0, :, (2 * h + 1) * HEAD_DIM:(2 * h + 2) * HEAD_DIM] = jnp.ones((r, HEAD_DIM), v_out.dtype)

    @pl.when(pl.program_id(1) == 0)
    def _():
        uext_ref[0:HALO, :] = jnp.zeros((HALO, pool_w), F32)
        heads(meta_ref[...], True, yam_ref, qm_ref, km_ref, vm_ref)

    heads(x_ref[0], False, ya_ref, q_ref, k_ref, v_ref)


def _mix_in(x, meta, g, w_in, w_pool, b_pool, pool_scale, qg, kg, later_weights):
    b, seq, d = x.shape
    pool_w = b_pool.shape[-1]
    qk_w = N_HEADS * HEAD_DIM
    tm = IN_TILE
    nt = seq // tm
    assert seq % tm == 0 and len(later_weights) == N_LATER_WEIGHTS
    tile = lambda w: pl.BlockSpec((1, tm, w), lambda bi, i: (bi, i, 0))
    mtile = lambda w: pl.BlockSpec((1, N_META, w), lambda bi, i: (bi, 0, 0))
    big = lambda w: jax.ShapeDtypeStruct((b, seq, w), BF16)
    small = lambda w: jax.ShapeDtypeStruct((b, N_META, w), BF16)
    def wblock(w):
        rows, steps = w.shape[1], b * nt
        hold = next(h for h in (1, 2, 4, 8) if (rows * h) % (steps * BF16_SUBLANES) == 0)
        return pl.BlockSpec((None, rows * hold // steps, w.shape[2]), lambda bi, i: (0, (bi * nt + i) // hold, 0))
    outs = pl.pallas_call(
        _mix_in_kernel,
        grid=(b, nt),
        in_specs=[tile(d), _const_spec(meta.shape), _const_spec(g.shape), _layer_const_spec(w_in.shape),
                  _layer_const_spec(w_pool.shape), _const_spec(b_pool.shape), _const_spec(pool_scale.shape),
                  _const_spec(qg.shape), _const_spec(kg.shape)] + [wblock(w) for w in later_weights],
        out_specs=[tile(pool_w), tile(qk_w), tile(qk_w), tile(2 * qk_w),
                   mtile(pool_w), mtile(qk_w), mtile(qk_w), mtile(2 * qk_w)]
                  + [wblock(w) for w in later_weights],
        out_shape=[big(pool_w), big(qk_w), big(qk_w), big(2 * qk_w),
                   small(pool_w), small(qk_w), small(qk_w), small(2 * qk_w)]
                  + [jax.ShapeDtypeStruct(w.shape, BF16) for w in later_weights],
        scratch_shapes=[pltpu.VMEM((HALO + tm, pool_w), F32), pltpu.VMEM(w_in.shape[1:], BF16)],
        compiler_params=pltpu.CompilerParams(
            dimension_semantics=("arbitrary", "arbitrary"), vmem_limit_bytes=VMEM_LIMIT_BYTES),
        name="mix_in",
    )(x, meta, g, w_in, w_pool, b_pool, pool_scale, qg, kg, *later_weights)
    return outs[:8], outs[8:]


def _stack_subheads(q, sub):
    lane = lax.broadcasted_iota(jnp.int32, (sub, q.shape[1]), 1)
    zero = jnp.zeros((sub, q.shape[1]), q.dtype)
    parts = []
    for r0 in range(0, q.shape[0], sub):
        part = q[r0:r0 + sub]
        parts += [jnp.where(lane < QK_DIM, part, zero), jnp.where(lane < QK_DIM, zero, part)]
    return jnp.concatenate(parts, axis=0)


def _scores(qs, k):
    return lax.dot_general(qs, k, (((1,), (1,)), ((), ())), preferred_element_type=F32)


def _causal_mask(s, masked_rows=None):
    sub = s.shape[1]
    row = lax.broadcasted_iota(jnp.int32, s.shape, 0)
    col = lax.broadcasted_iota(jnp.int32, s.shape, 1)
    visible = col <= (row & (sub - 1))
    if masked_rows is not None and masked_rows < s.shape[0]:
        visible = visible | (row >= masked_rows)
    return jnp.where(visible, s, -jnp.inf)


def _online_softmax_step(s, v_ext, m_ref, acc_ref):
    width = s.shape[1]
    m = m_ref[...]
    m_new = jnp.maximum(m, jnp.max(s, axis=-1, keepdims=True))
    alpha = jnp.exp2(m - m_new)
    m_wide = jnp.concatenate([m_new] * (width // LANES), axis=1) if width > LANES else m_new[:, :width]
    p = jnp.exp2(s - m_wide)
    acc_ref[...] = (jnp.concatenate([alpha, alpha], axis=1) * acc_ref[...]
                    + jnp.dot(p.astype(v_ext.dtype), v_ext, preferred_element_type=F32))
    m_ref[...] = m_new


def _init_state(m_ref, acc_ref):
    m_ref[...] = jnp.full(m_ref.shape, -jnp.inf, F32)
    acc_ref[...] = jnp.zeros(acc_ref.shape, F32)


def _first_softmax_step(s, v_ext, m_ref, acc_ref):
    m = jnp.max(s, axis=-1, keepdims=True)
    p = jnp.exp2(s - m)
    m_ref[...] = jnp.broadcast_to(m, m_ref.shape)
    acc_ref[...] = jnp.dot(p.astype(v_ext.dtype), v_ext, preferred_element_type=F32)


def _diff_attn_kernel(lam_init, q_ref, qnext_ref, k_ref, v_ref, qm_ref, km_ref, vm_ref,
                      lq1_ref, lk1_ref, lq2_ref, lk2_ref, sg_ref, yb_ref, ybm_ref,
                      qs_ref, qn_ref, m_ref, acc_ref, sa_ref, sb_ref, sc_ref):
    g = pl.program_id(2)
    n = ATTN_SUBTILES
    t = q_ref.shape[1] // n
    lam = (jnp.exp(jnp.sum(lq1_ref[...] * lk1_ref[...], axis=-1, keepdims=True))
           - jnp.exp(jnp.sum(lq2_ref[...] * lk2_ref[...], axis=-1, keepdims=True)) + lam_init)

    def finish(row0, r, out_ref, out_rows):
        o = (acc_ref[row0:row0 + r, :HEAD_DIM] / acc_ref[row0:row0 + r, HEAD_DIM:]
             - lam * (acc_ref[row0 + t:row0 + t + r, :HEAD_DIM] / acc_ref[row0 + t:row0 + t + r, HEAD_DIM:]))
        out_ref[0, out_rows, :] = (_rmsnorm(o, sg_ref[...]) * (1.0 - lam_init)).astype(out_ref.dtype)

    km = km_ref[0]
    vm = vm_ref[0]

    @pl.when(g == 0)
    def _():
        qm = _stack_subheads(qm_ref[0], N_META)
        for half in range(2):
            rows = slice(half * t, half * t + N_META)
            _init_state(m_ref.at[rows], acc_ref.at[rows])
        sm = _causal_mask(_scores(qm, km))
        for half in range(2):
            rows = slice(half * t, half * t + N_META)
            _online_softmax_step(sm[half * N_META:(half + 1) * N_META], vm,
                                 m_ref.at[rows], acc_ref.at[rows])
        finish(0, N_META, ybm_ref, slice(0, N_META))

    def kv_tile(ref, j):
        return ref[0, pl.ds(pl.multiple_of(j * t, t), t), :]

    def step(cur_ref, nxt_ref, j):
        nxt_ref[...] = _scores(qs_ref[...], kv_tile(k_ref, j + 1))
        _online_softmax_step(cur_ref[...], kv_tile(v_ref, j), m_ref, acc_ref)

    def diagonal_steps(cur_ref, nxt_ref):
        for c in range(n):
            rows = slice(2 * t * c, 2 * t * n)
            n_rows = 2 * t * (n - c)
            if c + 1 < n:
                nxt_ref[0:n_rows - 2 * t, :] = _scores(qs_ref[2 * t * (c + 1):2 * t * n, :],
                                                       kv_tile(k_ref, n * g + c + 1))
            else:
                qn_ref[...] = _stack_subheads(qnext_ref[0], t)
                sc_ref[...] = _scores(qn_ref[...], kv_tile(k_ref, 0))
            _online_softmax_step(_causal_mask(cur_ref[0:n_rows, :], masked_rows=2 * t), kv_tile(v_ref, n * g + c),
                                 m_ref.at[rows], acc_ref.at[rows])
            finish(2 * t * c, t, yb_ref, slice(t * c, t * (c + 1)))
            cur_ref, nxt_ref = nxt_ref, cur_ref

    @pl.when(g == 0)
    def _():
        qs_ref[...] = _stack_subheads(q_ref[0], t)
        sa_ref[...] = _scores(qs_ref[...], kv_tile(k_ref, 0))
        _first_softmax_step(_scores(qs_ref[...], km), vm, m_ref, acc_ref)
        diagonal_steps(sa_ref, sb_ref)

    @pl.when(g > 0)
    def _():
        qs_ref[...] = qn_ref[...]
        _first_softmax_step(_scores(qs_ref[...], km), vm, m_ref, acc_ref)
        step(sc_ref, sa_ref, 0)

        n_quads = (n * g - 1) // 4

        @pl.loop(0, n_quads)
        def _(jj):
            step(sa_ref, sb_ref, 4 * jj + 1)
            step(sb_ref, sa_ref, 4 * jj + 2)
            step(sa_ref, sb_ref, 4 * jj + 3)
            step(sb_ref, sa_ref, 4 * jj + 4)

        @pl.when((n * g - 1) % 4 >= 2)
        def _():
            step(sa_ref, sb_ref, 4 * n_quads + 1)
            step(sb_ref, sa_ref, 4 * n_quads + 2)

        step(sa_ref, sb_ref, n * g - 1)
        diagonal_steps(sb_ref, sa_ref)


def _diff_attn(q, k, v, qm, km, vm, lq1, lk1, lq2, lk2, subln_g, lam_init):
    b, seq, width = q.shape
    t, n = ATTN_TILE, ATTN_SUBTILES
    assert seq % (n * t) == 0 and n % 2 == 0 and width == N_HEADS * HEAD_DIM
    ng = seq // (n * t)
    qgroup = pl.BlockSpec((1, n * t, HEAD_DIM), lambda bi, h, g: (bi, g, h))
    qnext = pl.BlockSpec((1, n * t, HEAD_DIM), lambda bi, h, g: (bi, jnp.minimum(g + 1, ng - 1), h))
    full = lambda w: pl.BlockSpec((1, seq, w), lambda bi, h, g: (bi, 0, h))
    mtile = lambda w: pl.BlockSpec((1, N_META, w), lambda bi, h, g: (bi, 0, h))
    rows = 2 * n * t
    return pl.pallas_call(
        functools.partial(_diff_attn_kernel, lam_init),
        grid=(b, N_HEADS, ng),
        in_specs=[qgroup, qnext, full(HEAD_DIM), full(2 * HEAD_DIM), mtile(HEAD_DIM), mtile(HEAD_DIM), mtile(2 * HEAD_DIM),
                  _const_spec(lq1.shape), _const_spec(lk1.shape), _const_spec(lq2.shape),
                  _const_spec(lk2.shape), _const_spec(subln_g.shape)],
        out_specs=[qgroup, mtile(HEAD_DIM)],
        out_shape=[jax.ShapeDtypeStruct((b, seq, width), BF16),
                   jax.ShapeDtypeStruct((b, N_META, width), BF16)],
        scratch_shapes=[pltpu.VMEM((rows, HEAD_DIM), BF16), pltpu.VMEM((rows, HEAD_DIM), BF16),
                        pltpu.VMEM((rows, LANES), F32), pltpu.VMEM((rows, 2 * HEAD_DIM), F32),
                        pltpu.VMEM((rows, t), F32), pltpu.VMEM((rows, t), F32), pltpu.VMEM((rows, t), F32)],
        compiler_params=pltpu.CompilerParams(
            dimension_semantics=("arbitrary", "arbitrary", "arbitrary"),
            vmem_limit_bytes=VMEM_LIMIT_BYTES),
        name="diff_attn",
    )(q, q, k, v, qm, km, vm, lq1, lk1, lq2, lk2, subln_g)


def _mix_out_kernel(x_ref, ya_ref, yb_ref, meta_ref, yam_ref, ybm_ref, wo_ref, g_ref,
                    wup_ref, cw_ref, cb_ref, wdown_ref, out_ref, nperm_ref, next_ref, act_ref, operm_ref):
    tm = FFN_SUBTILE
    d = x_ref.shape[2]
    pool_w = ya_ref.shape[-1]
    d_ff = wdown_ref.shape[0]
    n_slabs = d // LANES
    nv = tm // SUBLANES

    def mix_residual(rows, ya, yb):
        h1 = (rows + jnp.dot(ya, wo_ref[:pool_w, :], preferred_element_type=F32)
              + jnp.dot(yb, wo_ref[pool_w:, :], preferred_element_type=F32))
        return h1, _rmsnorm(h1, g_ref[...])

    @pl.when(pl.program_id(1) == 0)
    def _():
        _, nm = mix_residual(meta_ref[...], yam_ref[0], ybm_ref[0])
        halo = jnp.concatenate([nm[N_META - 9:N_META - 1], nm[N_META - 8:]], axis=0)
        for c in range(n_slabs):
            nperm_ref[c, 0:HALO, :] = halo[:, c * LANES:(c + 1) * LANES]

    def sub_tile(r0):
        sub = slice(r0, r0 + tm)
        h1, n2 = mix_residual(x_ref[0, sub, :], ya_ref[0, sub, :], yb_ref[0, sub, :])
        for s in range(SUBLANES):
            for v0 in range(0, nv, SUBLANES):
                t0 = s * nv + v0
                for c in range(n_slabs):
                    nperm_ref[c, pl.ds(HALO + v0 * SUBLANES + s, SUBLANES, stride=SUBLANES), :] = (
                        n2[t0:t0 + SUBLANES, c * LANES:(c + 1) * LANES])
        next_ref[...] = jnp.concatenate([nperm_ref[c] for c in range(n_slabs)], axis=1).astype(BF16)

        def conv(u, cols):
            cur = u[HALO:]
            wrap1 = jnp.concatenate([u[HALO - 1:HALO], u[HALO + tm - 8:HALO + tm - 1]], axis=0)
            wrap2 = jnp.concatenate([u[HALO - 9:HALO - 8], u[HALO + tm - 16:HALO + tm - 9]], axis=0)
            back1 = jnp.concatenate([wrap1, u[HALO:HALO + tm - 8]], axis=0)
            back2 = jnp.concatenate([wrap2, wrap1, u[HALO:HALO + tm - 16]], axis=0)
            return (cb_ref[:, cols] + back2 * cw_ref[0:1, cols] + back1 * cw_ref[1:2, cols]
                    + cur * cw_ref[2:3, cols])

        for c in range(d_ff // FF_CHUNK):
            gcols = slice(c * FF_CHUNK, (c + 1) * FF_CHUNK)
            vcols = slice(d_ff + c * FF_CHUNK, d_ff + (c + 1) * FF_CHUNK)
            ug = jnp.dot(next_ref[...], wup_ref[:, gcols], preferred_element_type=F32)
            uv = jnp.dot(next_ref[...], wup_ref[:, vcols], preferred_element_type=F32)
            gate = conv(ug, gcols)
            act_ref[:, gcols] = (gate * jax.nn.sigmoid(gate) * conv(uv, vcols)).astype(BF16)
        ffn = jnp.dot(act_ref[...], wdown_ref[...], preferred_element_type=F32)
        for c in range(n_slabs):
            operm_ref[c] = ffn[:, c * LANES:(c + 1) * LANES]
            nperm_ref[c, 0:HALO, :] = nperm_ref[c, tm:tm + HALO, :]
        for s in range(SUBLANES):
            for v0 in range(0, nv, SUBLANES):
                t0 = s * nv + v0
                for c in range(n_slabs):
                    lanes = slice(c * LANES, (c + 1) * LANES)
                    out_ref[0, r0 + t0:r0 + t0 + SUBLANES, lanes] = (
                        h1[t0:t0 + SUBLANES, lanes]
                        + operm_ref[c, pl.ds(v0 * SUBLANES + s, SUBLANES, stride=SUBLANES), :])

    for r0 in range(0, x_ref.shape[1], tm):
        sub_tile(r0)


def _mix_out(x, ya, yb, meta, yam, ybm, w_out, g, w_up, conv_w, conv_b, w_down):
    b, seq, d = x.shape
    pool_w = ya.shape[-1]
    tm, sub = OUT_TILE, FFN_SUBTILE
    d_ff = w_down.shape[1]
    assert seq % tm == 0 and tm % sub == 0 and d_ff % FF_CHUNK == 0
    tile = lambda w: pl.BlockSpec((1, tm, w), lambda bi, i: (bi, i, 0))
    mtile = lambda w: pl.BlockSpec((1, N_META, w), lambda bi, i: (bi, 0, 0))
    return pl.pallas_call(
        _mix_out_kernel,
        grid=(b, seq // tm),
        in_specs=[tile(d), tile(pool_w), tile(yb.shape[-1]), _const_spec(meta.shape),
                  mtile(pool_w), mtile(ybm.shape[-1]), _layer_const_spec(w_out.shape), _const_spec(g.shape),
                  _layer_const_spec(w_up.shape), _const_spec(conv_w.shape), _const_spec(conv_b.shape),
                  _layer_const_spec(w_down.shape)],
        out_specs=tile(d),
        out_shape=jax.ShapeDtypeStruct((b, seq, d), x.dtype),
        scratch_shapes=[pltpu.VMEM((d // LANES, HALO + sub, LANES), F32), pltpu.VMEM((HALO + sub, d), BF16),
                        pltpu.VMEM((sub, d_ff), BF16), pltpu.VMEM((d // LANES, sub, LANES), F32)],
        compiler_params=pltpu.CompilerParams(
            dimension_semantics=("arbitrary", "arbitrary"), vmem_limit_bytes=VMEM_LIMIT_BYTES),
        name="mix_out",
    )(x, ya, yb, meta, yam, ybm, w_out, g, w_up, conv_w, conv_b, w_down)


def kernel(x, meta_tokens, norm_mix_g, w_in, w_pool, b_pool, pool_scale, q_norm_g, k_norm_g,
           lambda_q1, lambda_k1, lambda_q2, lambda_k2, subln_g, w_out, norm_ffn_g,
           w_up, conv_w, conv_b, w_down):
    depth = w_in.shape[0]
    assert depth == 1 and meta_tokens.shape[0] == N_META
    row = lambda a: a.reshape(1, -1).astype(F32)
    h = x
    meta = meta_tokens.astype(x.dtype)
    for i in range(depth):
        lam_init = 0.8 - 0.6 * math.exp(-0.3 * i)
        qg = jnp.tile(row(q_norm_g[i]), (1, 2)) * (QK_DIM ** -0.5 * math.log2(math.e))
        kg = jnp.tile(row(k_norm_g[i]), (1, 2))
        layer = slice(i, i + 1)
        (ya, q, k, v, yam, qm, km, vm), (w_out_n, w_up_n, w_down_n) = _mix_in(
            h, meta, row(norm_mix_g[i]), w_in[layer], w_pool[layer],
            row(b_pool[i]), row(pool_scale[i]), qg, kg, (w_out[layer], w_up[layer], w_down[layer]))
        yb, ybm = _diff_attn(q, k, v, qm, km, vm, row(lambda_q1[i]), row(lambda_k1[i]),
                             row(lambda_q2[i]), row(lambda_k2[i]), row(subln_g[i]), lam_init)
        h = _mix_out(h, ya, yb, meta, yam, ybm, w_out_n, row(norm_ffn_g[i]),
                     w_up_n, conv_w[i].astype(F32), row(conv_b[i]), w_down_n)
    return h
```

```python
import functools
import math

import jax
import jax.numpy as jnp
from jax import lax
from jax.experimental import pallas as pl
from jax.experimental.pallas import tpu as pltpu

F32 = jnp.float32
BF16 = jnp.bfloat16

N_META = 16
POOL_WINDOWS = (2, 4, 8, 16)
N_HEADS = 4
HEAD_DIM = 128
QK_DIM = 64
CONV_WIDTH = 3
EPS = 1e-6

LANES = 128
SUBLANES = 8
HALO = 16
VMEM_LIMIT_BYTES = 56 * 1024 * 1024

IN_TILE = 1024
SEQ_TILE = 512
ATTN_TILE = 512
ATTN_SUBTILES = 2
FF_CHUNK = 256
BF16_SUBLANES = 16
N_LATER_WEIGHTS = 3
N_MIX_IN_OUTS = 10


def _rmsnorm(x, g):
    return x * lax.rsqrt(jnp.mean(x * x, axis=-1, keepdims=True) + EPS) * g


def _const_spec(shape):
    nd = len(shape)
    return pl.BlockSpec(shape, lambda *_: (0,) * nd, pipeline_mode=pl.Buffered(1))


def _layer_const_spec(shape):
    nd = len(shape)
    return pl.BlockSpec((None,) + tuple(shape[1:]), lambda *_: (0,) * nd, pipeline_mode=pl.Buffered(1))


def _mix_in_kernel(x_ref, meta_ref, g_ref, win_ref, wpool_ref, bpool_ref, pscale_ref,
                   qg_ref, kg_ref, *refs):
    later_w = refs[:N_LATER_WEIGHTS]
    (ya_ref, q_ref, k_ref, v_ref, yam_ref, qm_ref, km_ref, vm_ref,
     ktail_ref, vtail_ref) = refs[N_LATER_WEIGHTS:N_LATER_WEIGHTS + N_MIX_IN_OUTS]
    later_w_bf16 = refs[N_LATER_WEIGHTS + N_MIX_IN_OUTS:2 * N_LATER_WEIGHTS + N_MIX_IN_OUTS]
    uext_ref, win_bf16_ref, kcarry_ref, vcarry_ref = refs[-4:]
    pool_w = bpool_ref.shape[-1]
    qk_w = q_ref.shape[-1]

    for src, dst in zip(later_w, later_w_bf16):
        dst[...] = src[...].astype(dst.dtype)

    @pl.when((pl.program_id(0) == 0) & (pl.program_id(1) == 0))
    def _():
        win_bf16_ref[...] = win_ref[...].astype(BF16)

    def put_shifted(out, cols, val, carry_ref, tail_ref):
        r = val.shape[0]
        out[0, 0:N_META, cols] = carry_ref[:, cols]
        out[0, N_META:, cols] = val[:r - N_META]
        carry_ref[:, cols] = val[r - N_META:]
        tail_ref[0, :, cols] = val[r - N_META:]

    def heads(rows, is_meta, ya_out, q_out, k_out, v_out):
        r = rows.shape[0]
        inv_rms = lax.rsqrt(jnp.mean(rows * rows, axis=-1, keepdims=True) + EPS)
        proj = inv_rms * jnp.dot((rows * g_ref[...]).astype(BF16), win_bf16_ref[...],
                                 preferred_element_type=F32)
        u = proj[:, :pool_w]
        uext_ref[HALO:HALO + r, :] = u
        for gi, w in enumerate(POOL_WINDOWS):
            cols = slice(gi * LANES, (gi + 1) * LANES)
            win_sum = u[:, cols]
            for j in range(1, w):
                win_sum = win_sum + uext_ref[HALO - j:HALO - j + r, cols]
            if is_meta:
                pos = lax.broadcasted_iota(jnp.int32, (r, 1), 0)
                mean = win_sum / jnp.minimum(pos + 1, w).astype(F32)
            else:
                mean = win_sum * (1.0 / w)
            pooled = mean - u[:, cols]
            y = jnp.dot(pooled.astype(BF16), wpool_ref[gi].astype(BF16), preferred_element_type=F32)
            y = (y + bpool_ref[:, cols]) * pscale_ref[:, cols]
            ya_out[0, :, cols] = y.astype(ya_out.dtype)
        uext_ref[0:HALO, :] = uext_ref[r:r + HALO, :]

        lane = lax.broadcasted_iota(jnp.int32, (r, LANES), 1)
        lo = lane < QK_DIM
        for src, gain_ref, out in ((proj[:, pool_w:pool_w + qk_w], qg_ref, q_out),
                                   (proj[:, pool_w + qk_w:pool_w + 2 * qk_w], kg_ref, k_out)):
            for h in range(N_HEADS):
                cols = slice(h * LANES, (h + 1) * LANES)
                slab = src[:, cols]
                sq = slab * slab
                s_lo = jnp.sum(jnp.where(lo, sq, 0.0), axis=-1, keepdims=True)
                s_hi = jnp.sum(jnp.where(lo, 0.0, sq), axis=-1, keepdims=True)
                ms = jnp.where(lo, s_lo, s_hi) * (1.0 / QK_DIM)
                normed = (slab * lax.rsqrt(ms + EPS) * gain_ref[...]).astype(out.dtype)
                if out is k_out and not is_meta:
                    put_shifted(out, cols, normed, kcarry_ref, ktail_ref)
                else:
                    out[0, :, cols] = normed
        for h in range(N_HEADS):
            vcols = slice(pool_w + 2 * qk_w + h * HEAD_DIM, pool_w + 2 * qk_w + (h + 1) * HEAD_DIM)
            out_cols = slice(2 * h * HEAD_DIM, (2 * h + 1) * HEAD_DIM)
            one_cols = slice((2 * h + 1) * HEAD_DIM, (2 * h + 2) * HEAD_DIM)
            vals = proj[:, vcols].astype(v_out.dtype)
            if is_meta:
                v_out[0, :, out_cols] = vals
            else:
                put_shifted(v_out, out_cols, vals, vcarry_ref, vtail_ref)
                vtail_ref[0, :, one_cols] = jnp.ones((N_META, HEAD_DIM), v_out.dtype)
            v_out[0, :, one_cols] = jnp.ones((r, HEAD_DIM), v_out.dtype)

    @pl.when(pl.program_id(1) == 0)
    def _():
        uext_ref[0:HALO, :] = jnp.zeros((HALO, pool_w), F32)
        heads(meta_ref[...], True, yam_ref, qm_ref, km_ref, vm_ref)
        kcarry_ref[...] = km_ref[0]
        vcarry_ref[...] = vm_ref[0]

    heads(x_ref[0], False, ya_ref, q_ref, k_ref, v_ref)


def _mix_in(x, meta, g, w_in, w_pool, b_pool, pool_scale, qg, kg, later_weights):
    b, seq, d = x.shape
    pool_w = b_pool.shape[-1]
    qk_w = N_HEADS * HEAD_DIM
    tm = IN_TILE
    nt = seq // tm
    assert seq % tm == 0 and len(later_weights) == N_LATER_WEIGHTS
    tile = lambda w: pl.BlockSpec((1, tm, w), lambda bi, i: (bi, i, 0))
    mtile = lambda w: pl.BlockSpec((1, N_META, w), lambda bi, i: (bi, 0, 0))
    big = lambda w: jax.ShapeDtypeStruct((b, seq, w), BF16)
    small = lambda w: jax.ShapeDtypeStruct((b, N_META, w), BF16)
    def wblock(w):
        rows, steps = w.shape[1], b * nt
        hold = next(h for h in (1, 2, 4, 8) if (rows * h) % (steps * BF16_SUBLANES) == 0)
        return pl.BlockSpec((None, rows * hold // steps, w.shape[2]), lambda bi, i: (0, (bi * nt + i) // hold, 0))
    outs = pl.pallas_call(
        _mix_in_kernel,
        grid=(b, nt),
        in_specs=[tile(d), _const_spec(meta.shape), _const_spec(g.shape), _layer_const_spec(w_in.shape),
                  _layer_const_spec(w_pool.shape), _const_spec(b_pool.shape), _const_spec(pool_scale.shape),
                  _const_spec(qg.shape), _const_spec(kg.shape)] + [wblock(w) for w in later_weights],
        out_specs=[tile(pool_w), tile(qk_w), tile(qk_w), tile(2 * qk_w),
                   mtile(pool_w), mtile(qk_w), mtile(qk_w), mtile(2 * qk_w), mtile(qk_w), mtile(2 * qk_w)]
                  + [wblock(w) for w in later_weights],
        out_shape=[big(pool_w), big(qk_w), big(qk_w), big(2 * qk_w),
                   small(pool_w), small(qk_w), small(qk_w), small(2 * qk_w), small(qk_w), small(2 * qk_w)]
                  + [jax.ShapeDtypeStruct(w.shape, BF16) for w in later_weights],
        scratch_shapes=[pltpu.VMEM((HALO + tm, pool_w), F32), pltpu.VMEM(w_in.shape[1:], BF16),
                        pltpu.VMEM((N_META, qk_w), BF16), pltpu.VMEM((N_META, 2 * qk_w), BF16)],
        compiler_params=pltpu.CompilerParams(
            dimension_semantics=("arbitrary", "arbitrary"), vmem_limit_bytes=VMEM_LIMIT_BYTES),
        name="mix_in",
    )(x, meta, g, w_in, w_pool, b_pool, pool_scale, qg, kg, *later_weights)
    return outs[:N_MIX_IN_OUTS], outs[N_MIX_IN_OUTS:]


def _stack_subheads(q, sub):
    lane = lax.broadcasted_iota(jnp.int32, (sub, q.shape[1]), 1)
    zero = jnp.zeros((sub, q.shape[1]), q.dtype)
    parts = []
    for r0 in range(0, q.shape[0], sub):
        part = q[r0:r0 + sub]
        parts += [jnp.where(lane < QK_DIM, part, zero), jnp.where(lane < QK_DIM, zero, part)]
    return jnp.concatenate(parts, axis=0)


def _scores(qs, k):
    return lax.dot_general(qs, k, (((1,), (1,)), ((), ())), preferred_element_type=F32)


def _causal_mask(s, masked_rows=None, ahead=0):
    sub = s.shape[1]
    row = lax.broadcasted_iota(jnp.int32, s.shape, 0)
    col = lax.broadcasted_iota(jnp.int32, s.shape, 1)
    visible = col <= (row & (sub - 1)) + ahead
    if masked_rows is not None and masked_rows < s.shape[0]:
        visible = visible | (row >= masked_rows)
    return jnp.where(visible, s, -jnp.inf)


def _online_softmax_step(s, v_ext, m_ref, acc_ref):
    width = s.shape[1]
    m = m_ref[...]
    m_new = jnp.maximum(m, jnp.max(s, axis=-1, keepdims=True))
    alpha = jnp.exp2(m - m_new)
    m_wide = jnp.concatenate([m_new] * (width // LANES), axis=1) if width > LANES else m_new[:, :width]
    p = jnp.exp2(s - m_wide)
    acc_ref[...] = (jnp.concatenate([alpha, alpha], axis=1) * acc_ref[...]
                    + jnp.dot(p.astype(v_ext.dtype), v_ext, preferred_element_type=F32))
    m_ref[...] = m_new


def _init_state(m_ref, acc_ref):
    m_ref[...] = jnp.full(m_ref.shape, -jnp.inf, F32)
    acc_ref[...] = jnp.zeros(acc_ref.shape, F32)


def _first_softmax_step(s, v_ext, m_ref, acc_ref):
    m = jnp.max(s, axis=-1, keepdims=True)
    p = jnp.exp2(s - m)
    m_ref[...] = jnp.broadcast_to(m, m_ref.shape)
    acc_ref[...] = jnp.dot(p.astype(v_ext.dtype), v_ext, preferred_element_type=F32)


def _diff_attn_kernel(lam_init, q_ref, qnext_ref, k_ref, v_ref, qm_ref, km_ref, vm_ref, ktail_ref, vtail_ref,
                      lq1_ref, lk1_ref, lq2_ref, lk2_ref, sg_ref, yb_ref, ybm_ref,
                      qs_ref, qn_ref, m_ref, acc_ref, sa_ref, sb_ref, sc_ref):
    g = pl.program_id(2)
    n = ATTN_SUBTILES
    t = q_ref.shape[1] // n
    n_tiles = k_ref.shape[1] // t
    lam = (jnp.exp(jnp.sum(lq1_ref[...] * lk1_ref[...], axis=-1, keepdims=True))
           - jnp.exp(jnp.sum(lq2_ref[...] * lk2_ref[...], axis=-1, keepdims=True)) + lam_init)

    def finish(row0, r, out_ref, out_rows):
        o = (acc_ref[row0:row0 + r, :HEAD_DIM] / acc_ref[row0:row0 + r, HEAD_DIM:]
             - lam * (acc_ref[row0 + t:row0 + t + r, :HEAD_DIM] / acc_ref[row0 + t:row0 + t + r, HEAD_DIM:]))
        out_ref[0, out_rows, :] = (_rmsnorm(o, sg_ref[...]) * (1.0 - lam_init)).astype(out_ref.dtype)

    @pl.when(g == 0)
    def _():
        qm = _stack_subheads(qm_ref[0], N_META)
        for half in range(2):
            rows = slice(half * t, half * t + N_META)
            _init_state(m_ref.at[rows], acc_ref.at[rows])
        sm = _causal_mask(_scores(qm, km_ref[0]))
        for half in range(2):
            rows = slice(half * t, half * t + N_META)
            _online_softmax_step(sm[half * N_META:(half + 1) * N_META], vm_ref[0],
                                 m_ref.at[rows], acc_ref.at[rows])
        finish(0, N_META, ybm_ref, slice(0, N_META))

    def kv_tile(ref, j):
        return ref[0, pl.ds(pl.multiple_of(j * t, t), t), :]

    def step(cur_ref, nxt_ref, j):
        nxt_ref[...] = _scores(qs_ref[...], kv_tile(k_ref, j + 1))
        _online_softmax_step(cur_ref[...], kv_tile(v_ref, j), m_ref, acc_ref)

    def corner(c):
        nxt = n * g + c + 1
        in_range = nxt < n_tiles
        start = pl.multiple_of(jnp.minimum(nxt, n_tiles - 1) * t, t)
        k_first = jnp.where(in_range, k_ref[0, pl.ds(start, N_META), :], ktail_ref[0])
        v_first = jnp.where(in_range, v_ref[0, pl.ds(start, N_META), :], vtail_ref[0])
        for half in range(2):
            r0 = 2 * t * c + half * t + t - N_META
            rows = slice(r0, r0 + N_META)
            s = _scores(qs_ref[rows, :], k_first)
            row = lax.broadcasted_iota(jnp.int32, s.shape, 0)
            col = lax.broadcasted_iota(jnp.int32, s.shape, 1)
            _online_softmax_step(jnp.where(col <= row, s, -jnp.inf), v_first, m_ref.at[rows], acc_ref.at[rows])

    def diagonal_steps(cur_ref, nxt_ref, first):
        for c in range(n):
            rows = slice(2 * t * c, 2 * t * n)
            n_rows = 2 * t * (n - c)
            if c + 1 < n:
                nxt_ref[0:n_rows - 2 * t, :] = _scores(qs_ref[2 * t * (c + 1):2 * t * n, :],
                                                       kv_tile(k_ref, n * g + c + 1))
            else:
                qn_ref[...] = _stack_subheads(qnext_ref[0], t)
                sc_ref[...] = _scores(qn_ref[...], kv_tile(k_ref, 0))
            update = _first_softmax_step if (first and c == 0) else _online_softmax_step
            update(_causal_mask(cur_ref[0:n_rows, :], masked_rows=2 * t, ahead=N_META), kv_tile(v_ref, n * g + c),
                   m_ref.at[rows], acc_ref.at[rows])
            corner(c)
            finish(2 * t * c, t, yb_ref, slice(t * c, t * (c + 1)))
            cur_ref, nxt_ref = nxt_ref, cur_ref

    @pl.when(g == 0)
    def _():
        qs_ref[...] = _stack_subheads(q_ref[0], t)
        sa_ref[...] = _scores(qs_ref[...], kv_tile(k_ref, 0))
        diagonal_steps(sa_ref, sb_ref, first=True)

    @pl.when(g > 0)
    def _():
        qs_ref[...] = qn_ref[...]
        sa_ref[...] = _scores(qs_ref[...], kv_tile(k_ref, 1))
        _first_softmax_step(sc_ref[...], kv_tile(v_ref, 0), m_ref, acc_ref)

        n_quads = (n * g - 1) // 4

        @pl.loop(0, n_quads)
        def _(jj):
            step(sa_ref, sb_ref, 4 * jj + 1)
            step(sb_ref, sa_ref, 4 * jj + 2)
            step(sa_ref, sb_ref, 4 * jj + 3)
            step(sb_ref, sa_ref, 4 * jj + 4)

        @pl.when((n * g - 1) % 4 >= 2)
        def _():
            step(sa_ref, sb_ref, 4 * n_quads + 1)
            step(sb_ref, sa_ref, 4 * n_quads + 2)

        step(sa_ref, sb_ref, n * g - 1)
        diagonal_steps(sb_ref, sa_ref, first=False)


def _diff_attn(q, k, v, qm, km, vm, ktail, vtail, lq1, lk1, lq2, lk2, subln_g, lam_init):
    b, seq, width = q.shape
    t, n = ATTN_TILE, ATTN_SUBTILES
    assert seq % (n * t) == 0 and n % 2 == 0 and width == N_HEADS * HEAD_DIM
    ng = seq // (n * t)
    qgroup = pl.BlockSpec((1, n * t, HEAD_DIM), lambda bi, h, g: (bi, g, h))
    qnext = pl.BlockSpec((1, n * t, HEAD_DIM), lambda bi, h, g: (bi, jnp.minimum(g + 1, ng - 1), h))
    full = lambda w: pl.BlockSpec((1, seq, w), lambda bi, h, g: (bi, 0, h))
    mtile = lambda w: pl.BlockSpec((1, N_META, w), lambda bi, h, g: (bi, 0, h))
    rows = 2 * n * t
    return pl.pallas_call(
        functools.partial(_diff_attn_kernel, lam_init),
        grid=(b, N_HEADS, ng),
        in_specs=[qgroup, qnext, full(HEAD_DIM), full(2 * HEAD_DIM), mtile(HEAD_DIM), mtile(HEAD_DIM), mtile(2 * HEAD_DIM),
                  mtile(HEAD_DIM), mtile(2 * HEAD_DIM), _const_spec(lq1.shape), _const_spec(lk1.shape), _const_spec(lq2.shape),
                  _const_spec(lk2.shape), _const_spec(subln_g.shape)],
        out_specs=[qgroup, mtile(HEAD_DIM)],
        out_shape=[jax.ShapeDtypeStruct((b, seq, width), BF16),
                   jax.ShapeDtypeStruct((b, N_META, width), BF16)],
        scratch_shapes=[pltpu.VMEM((rows, HEAD_DIM), BF16), pltpu.VMEM((rows, HEAD_DIM), BF16),
                        pltpu.VMEM((rows, LANES), F32), pltpu.VMEM((rows, 2 * HEAD_DIM), F32),
                        pltpu.VMEM((rows, t), F32), pltpu.VMEM((rows, t), F32), pltpu.VMEM((rows, t), F32)],
        compiler_params=pltpu.CompilerParams(
            dimension_semantics=("arbitrary", "arbitrary", "arbitrary"),
            vmem_limit_bytes=VMEM_LIMIT_BYTES),
        name="diff_attn",
    )(q, q, k, v, qm, km, vm, ktail, vtail, lq1, lk1, lq2, lk2, subln_g)


def _mix_out_kernel(x_ref, ya_ref, yb_ref, meta_ref, yam_ref, ybm_ref, wo_ref, g_ref,
                    wup_ref, cw_ref, cb_ref, wdown_ref, out_ref, nperm_ref, next_ref, act_ref, operm_ref):
    tm = x_ref.shape[1]
    d = x_ref.shape[2]
    pool_w = ya_ref.shape[-1]
    d_ff = wdown_ref.shape[0]
    n_slabs = d // LANES
    nv = tm // SUBLANES

    def mix_residual(rows, ya, yb):
        h1 = (rows + jnp.dot(ya, wo_ref[:pool_w, :], preferred_element_type=F32)
              + jnp.dot(yb, wo_ref[pool_w:, :], preferred_element_type=F32))
        return h1, _rmsnorm(h1, g_ref[...])

    @pl.when(pl.program_id(1) == 0)
    def _():
        _, nm = mix_residual(meta_ref[...], yam_ref[0], ybm_ref[0])
        halo = jnp.concatenate([nm[N_META - 9:N_META - 1], nm[N_META - 8:]], axis=0)
        for c in range(n_slabs):
            nperm_ref[c, 0:HALO, :] = halo[:, c * LANES:(c + 1) * LANES]

    h1, n2 = mix_residual(x_ref[0], ya_ref[0], yb_ref[0])
    for s in range(SUBLANES):
        for v0 in range(0, nv, SUBLANES):
            t0 = s * nv + v0
            for c in range(n_slabs):
                nperm_ref[c, pl.ds(HALO + v0 * SUBLANES + s, SUBLANES, stride=SUBLANES), :] = (
                    n2[t0:t0 + SUBLANES, c * LANES:(c + 1) * LANES])
    next_ref[...] = jnp.concatenate([nperm_ref[c] for c in range(n_slabs)], axis=1).astype(BF16)

    def conv(u, cols):
        cur = u[HALO:]
        wrap1 = jnp.concatenate([u[HALO - 1:HALO], u[HALO + tm - 8:HALO + tm - 1]], axis=0)
        wrap2 = jnp.concatenate([u[HALO - 9:HALO - 8], u[HALO + tm - 16:HALO + tm - 9]], axis=0)
        back1 = jnp.concatenate([wrap1, u[HALO:HALO + tm - 8]], axis=0)
        back2 = jnp.concatenate([wrap2, wrap1, u[HALO:HALO + tm - 16]], axis=0)
        return (cb_ref[:, cols] + back2 * cw_ref[0:1, cols] + back1 * cw_ref[1:2, cols]
                + cur * cw_ref[2:3, cols])

    for c in range(d_ff // FF_CHUNK):
        gcols = slice(c * FF_CHUNK, (c + 1) * FF_CHUNK)
        vcols = slice(d_ff + c * FF_CHUNK, d_ff + (c + 1) * FF_CHUNK)
        ug = jnp.dot(next_ref[...], wup_ref[:, gcols], preferred_element_type=F32)
        uv = jnp.dot(next_ref[...], wup_ref[:, vcols], preferred_element_type=F32)
        gate = conv(ug, gcols)
        act_ref[:, gcols] = (gate * jax.nn.sigmoid(gate) * conv(uv, vcols)).astype(BF16)
    ffn = jnp.dot(act_ref[...], wdown_ref[...], preferred_element_type=F32)
    for c in range(n_slabs):
        operm_ref[c] = ffn[:, c * LANES:(c + 1) * LANES]
        nperm_ref[c, 0:HALO, :] = nperm_ref[c, tm:tm + HALO, :]
    for s in range(SUBLANES):
        for v0 in range(0, nv, SUBLANES):
            t0 = s * nv + v0
            for c in range(n_slabs):
                lanes = slice(c * LANES, (c + 1) * LANES)
                out_ref[0, t0:t0 + SUBLANES, lanes] = (
                    h1[t0:t0 + SUBLANES, lanes]
                    + operm_ref[c, pl.ds(v0 * SUBLANES + s, SUBLANES, stride=SUBLANES), :])


def _mix_out(x, ya, yb, meta, yam, ybm, w_out, g, w_up, conv_w, conv_b, w_down):
    b, seq, d = x.shape
    pool_w = ya.shape[-1]
    tm = SEQ_TILE
    d_ff = w_down.shape[1]
    assert seq % tm == 0 and d_ff % FF_CHUNK == 0
    tile = lambda w: pl.BlockSpec((1, tm, w), lambda bi, i: (bi, i, 0))
    mtile = lambda w: pl.BlockSpec((1, N_META, w), lambda bi, i: (bi, 0, 0))
    return pl.pallas_call(
        _mix_out_kernel,
        grid=(b, seq // tm),
        in_specs=[tile(d), tile(pool_w), tile(yb.shape[-1]), _const_spec(meta.shape),
                  mtile(pool_w), mtile(ybm.shape[-1]), _layer_const_spec(w_out.shape), _const_spec(g.shape),
                  _layer_const_spec(w_up.shape), _const_spec(conv_w.shape), _const_spec(conv_b.shape),
                  _layer_const_spec(w_down.shape)],
        out_specs=tile(d),
        out_shape=jax.ShapeDtypeStruct((b, seq, d), x.dtype),
        scratch_shapes=[pltpu.VMEM((d // LANES, HALO + tm, LANES), F32), pltpu.VMEM((HALO + tm, d), BF16),
                        pltpu.VMEM((tm, d_ff), BF16), pltpu.VMEM((d // LANES, tm, LANES), F32)],
        compiler_params=pltpu.CompilerParams(
            dimension_semantics=("arbitrary", "arbitrary"), vmem_limit_bytes=VMEM_LIMIT_BYTES),
        name="mix_out",
    )(x, ya, yb, meta, yam, ybm, w_out, g, w_up, conv_w, conv_b, w_down)


def kernel(x, meta_tokens, norm_mix_g, w_in, w_pool, b_pool, pool_scale, q_norm_g, k_norm_g,
           lambda_q1, lambda_k1, lambda_q2, lambda_k2, subln_g, w_out, norm_ffn_g,
           w_up, conv_w, conv_b, w_down):
    depth = w_in.shape[0]
    assert depth == 1 and meta_tokens.shape[0] == N_META
    row = lambda a: a.reshape(1, -1).astype(F32)
    h = x
    meta = meta_tokens.astype(x.dtype)
    for i in range(depth):
        lam_init = 0.8 - 0.6 * math.exp(-0.3 * i)
        qg = jnp.tile(row(q_norm_g[i]), (1, 2)) * (QK_DIM ** -0.5 * math.log2(math.e))
        kg = jnp.tile(row(k_norm_g[i]), (1, 2))
        layer = slice(i, i + 1)
        (ya, q, k, v, yam, qm, km, vm, kt, vt), (w_out_n, w_up_n, w_down_n) = _mix_in(
            h, meta, row(norm_mix_g[i]), w_in[layer], w_pool[layer],
            row(b_pool[i]), row(pool_scale[i]), qg, kg, (w_out[layer], w_up[layer], w_down[layer]))
        yb, ybm = _diff_attn(q, k, v, qm, km, vm, kt, vt, row(lambda_q1[i]), row(lambda_k1[i]),
                             row(lambda_q2[i]), row(lambda_k2[i]), row(subln_g[i]), lam_init)
        h = _mix_out(h, ya, yb, meta, yam, ybm, w_out_n, row(norm_ffn_g[i]),
                     w_up_n, conv_w[i].astype(F32), row(conv_b[i]), w_down_n)
    return h
```

```python
import functools
import math

import jax
import jax.numpy as jnp
from jax import lax
from jax.experimental import pallas as pl
from jax.experimental.pallas import tpu as pltpu

F32 = jnp.float32
BF16 = jnp.bfloat16

N_META = 16
POOL_WINDOWS = (2, 4, 8, 16)
N_HEADS = 4
HEAD_DIM = 128
QK_DIM = 64
CONV_WIDTH = 3
EPS = 1e-6

LANES = 128
SUBLANES = 8
HALO = 16
VMEM_LIMIT_BYTES = 56 * 1024 * 1024

IN_TILE = 1024
SEQ_TILE = 512
ATTN_TILE = 512
ATTN_SUBTILES = 2
FF_CHUNK = 256
BF16_SUBLANES = 16
N_LATER_WEIGHTS = 3
N_MIX_IN_OUTS = 10


def _rmsnorm(x, g):
    return x * lax.rsqrt(jnp.mean(x * x, axis=-1, keepdims=True) + EPS) * g


def _const_spec(shape):
    nd = len(shape)
    return pl.BlockSpec(shape, lambda *_: (0,) * nd, pipeline_mode=pl.Buffered(1))


def _layer_const_spec(shape):
    nd = len(shape)
    return pl.BlockSpec((None,) + tuple(shape[1:]), lambda *_: (0,) * nd, pipeline_mode=pl.Buffered(1))


def _mix_in_kernel(x_ref, meta_ref, g_ref, win_ref, wpool_ref, bpool_ref, pscale_ref,
                   qg_ref, kg_ref, *refs):
    later_w = refs[:N_LATER_WEIGHTS]
    (ya_ref, q_ref, k_ref, v_ref, yam_ref, qm_ref, km_ref, vm_ref,
     ktail_ref, vtail_ref) = refs[N_LATER_WEIGHTS:N_LATER_WEIGHTS + N_MIX_IN_OUTS]
    later_w_bf16 = refs[N_LATER_WEIGHTS + N_MIX_IN_OUTS:2 * N_LATER_WEIGHTS + N_MIX_IN_OUTS]
    uext_ref, win_bf16_ref, kcarry_ref, vcarry_ref = refs[-4:]
    pool_w = bpool_ref.shape[-1]
    qk_w = q_ref.shape[-1]

    for src, dst in zip(later_w, later_w_bf16):
        dst[...] = src[...].astype(dst.dtype)

    @pl.when((pl.program_id(0) == 0) & (pl.program_id(1) == 0))
    def _():
        win_bf16_ref[...] = win_ref[...].astype(BF16)

    def put_shifted(out, cols, val, carry_ref, tail_ref):
        r = val.shape[0]
        out[0, 0:N_META, cols] = carry_ref[:, cols]
        out[0, N_META:, cols] = val[:r - N_META]
        carry_ref[:, cols] = val[r - N_META:]
        tail_ref[0, :, cols] = val[r - N_META:]

    def heads(rows, is_meta, ya_out, q_out, k_out, v_out):
        r = rows.shape[0]
        inv_rms = lax.rsqrt(jnp.mean(rows * rows, axis=-1, keepdims=True) + EPS)
        proj = inv_rms * jnp.dot((rows * g_ref[...]).astype(BF16), win_bf16_ref[...],
                                 preferred_element_type=F32)
        u = proj[:, :pool_w]
        uext_ref[HALO:HALO + r, :] = u
        pooled = []
        for gi, w in enumerate(POOL_WINDOWS):
            cols = slice(gi * LANES, (gi + 1) * LANES)
            win_sum = u[:, cols]
            for j in range(1, w):
                win_sum = win_sum + uext_ref[HALO - j:HALO - j + r, cols]
            if is_meta:
                pos = lax.broadcasted_iota(jnp.int32, (r, 1), 0)
                mean = win_sum / jnp.minimum(pos + 1, w).astype(F32)
            else:
                mean = win_sum * (1.0 / w)
            pooled.append((mean - u[:, cols]).astype(BF16))
        for pi in range(len(POOL_WINDOWS) // 2):
            cols = slice(2 * pi * LANES, (2 * pi + 2) * LANES)
            y = jnp.dot(jnp.concatenate(pooled[2 * pi:2 * pi + 2], axis=1), wpool_ref[pi].astype(BF16),
                        preferred_element_type=F32)
            y = (y + bpool_ref[:, cols]) * pscale_ref[:, cols]
            ya_out[0, :, cols] = y.astype(ya_out.dtype)
        uext_ref[0:HALO, :] = uext_ref[r:r + HALO, :]

        lane = lax.broadcasted_iota(jnp.int32, (r, LANES), 1)
        lo = lane < QK_DIM
        for src, gain_ref, out in ((proj[:, pool_w:pool_w + qk_w], qg_ref, q_out),
                                   (proj[:, pool_w + qk_w:pool_w + 2 * qk_w], kg_ref, k_out)):
            for h in range(N_HEADS):
                cols = slice(h * LANES, (h + 1) * LANES)
                slab = src[:, cols]
                sq = slab * slab
                s_lo = jnp.sum(jnp.where(lo, sq, 0.0), axis=-1, keepdims=True)
                s_hi = jnp.sum(jnp.where(lo, 0.0, sq), axis=-1, keepdims=True)
                ms = jnp.where(lo, s_lo, s_hi) * (1.0 / QK_DIM)
                normed = (slab * lax.rsqrt(ms + EPS) * gain_ref[...]).astype(out.dtype)
                if out is k_out and not is_meta:
                    put_shifted(out, cols, normed, kcarry_ref, ktail_ref)
                else:
                    out[0, :, cols] = normed
        for h in range(N_HEADS):
            vcols = slice(pool_w + 2 * qk_w + h * HEAD_DIM, pool_w + 2 * qk_w + (h + 1) * HEAD_DIM)
            out_cols = slice(2 * h * HEAD_DIM, (2 * h + 1) * HEAD_DIM)
            one_cols = slice((2 * h + 1) * HEAD_DIM, (2 * h + 2) * HEAD_DIM)
            vals = proj[:, vcols].astype(v_out.dtype)
            if is_meta:
                v_out[0, :, out_cols] = vals
            else:
                put_shifted(v_out, out_cols, vals, vcarry_ref, vtail_ref)
                vtail_ref[0, :, one_cols] = jnp.ones((N_META, HEAD_DIM), v_out.dtype)
            v_out[0, :, one_cols] = jnp.ones((r, HEAD_DIM), v_out.dtype)

    @pl.when(pl.program_id(1) == 0)
    def _():
        uext_ref[0:HALO, :] = jnp.zeros((HALO, pool_w), F32)
        heads(meta_ref[...], True, yam_ref, qm_ref, km_ref, vm_ref)
        kcarry_ref[...] = km_ref[0]
        vcarry_ref[...] = vm_ref[0]

    heads(x_ref[0], False, ya_ref, q_ref, k_ref, v_ref)


def _mix_in(x, meta, g, w_in, w_pool, b_pool, pool_scale, qg, kg, later_weights):
    b, seq, d = x.shape
    pool_w = b_pool.shape[-1]
    qk_w = N_HEADS * HEAD_DIM
    tm = IN_TILE
    nt = seq // tm
    assert seq % tm == 0 and len(later_weights) == N_LATER_WEIGHTS
    tile = lambda w: pl.BlockSpec((1, tm, w), lambda bi, i: (bi, i, 0))
    mtile = lambda w: pl.BlockSpec((1, N_META, w), lambda bi, i: (bi, 0, 0))
    big = lambda w: jax.ShapeDtypeStruct((b, seq, w), BF16)
    small = lambda w: jax.ShapeDtypeStruct((b, N_META, w), BF16)
    def wblock(w):
        rows, steps = w.shape[1], b * nt
        hold = next(h for h in (1, 2, 4, 8) if (rows * h) % (steps * BF16_SUBLANES) == 0)
        return pl.BlockSpec((None, rows * hold // steps, w.shape[2]), lambda bi, i: (0, (bi * nt + i) // hold, 0))
    outs = pl.pallas_call(
        _mix_in_kernel,
        grid=(b, nt),
        in_specs=[tile(d), _const_spec(meta.shape), _const_spec(g.shape), _layer_const_spec(w_in.shape),
                  _const_spec(w_pool.shape), _const_spec(b_pool.shape), _const_spec(pool_scale.shape),
                  _const_spec(qg.shape), _const_spec(kg.shape)] + [wblock(w) for w in later_weights],
        out_specs=[tile(pool_w), tile(qk_w), tile(qk_w), tile(2 * qk_w),
                   mtile(pool_w), mtile(qk_w), mtile(qk_w), mtile(2 * qk_w), mtile(qk_w), mtile(2 * qk_w)]
                  + [wblock(w) for w in later_weights],
        out_shape=[big(pool_w), big(qk_w), big(qk_w), big(2 * qk_w),
                   small(pool_w), small(qk_w), small(qk_w), small(2 * qk_w), small(qk_w), small(2 * qk_w)]
                  + [jax.ShapeDtypeStruct(w.shape, BF16) for w in later_weights],
        scratch_shapes=[pltpu.VMEM((HALO + tm, pool_w), F32), pltpu.VMEM(w_in.shape[1:], BF16),
                        pltpu.VMEM((N_META, qk_w), BF16), pltpu.VMEM((N_META, 2 * qk_w), BF16)],
        compiler_params=pltpu.CompilerParams(
            dimension_semantics=("arbitrary", "arbitrary"), vmem_limit_bytes=VMEM_LIMIT_BYTES),
        name="mix_in",
    )(x, meta, g, w_in, w_pool, b_pool, pool_scale, qg, kg, *later_weights)
    return outs[:N_MIX_IN_OUTS], outs[N_MIX_IN_OUTS:]


def _stack_subheads(q, sub):
    lane = lax.broadcasted_iota(jnp.int32, (sub, q.shape[1]), 1)
    zero = jnp.zeros((sub, q.shape[1]), q.dtype)
    parts = []
    for r0 in range(0, q.shape[0], sub):
        part = q[r0:r0 + sub]
        parts += [jnp.where(lane < QK_DIM, part, zero), jnp.where(lane < QK_DIM, zero, part)]
    return jnp.concatenate(parts, axis=0)


def _scores(qs, k):
    return lax.dot_general(qs, k, (((1,), (1,)), ((), ())), preferred_element_type=F32)


def _causal_mask(s, masked_rows=None, ahead=0):
    sub = s.shape[1]
    row = lax.broadcasted_iota(jnp.int32, s.shape, 0)
    col = lax.broadcasted_iota(jnp.int32, s.shape, 1)
    visible = col <= (row & (sub - 1)) + ahead
    if masked_rows is not None and masked_rows < s.shape[0]:
        visible = visible | (row >= masked_rows)
    return jnp.where(visible, s, -jnp.inf)


def _online_softmax_step(s, v_ext, m_ref, acc_ref):
    width = s.shape[1]
    m = m_ref[...]
    m_new = jnp.maximum(m, jnp.max(s, axis=-1, keepdims=True))
    alpha = jnp.exp2(m - m_new)
    m_wide = jnp.concatenate([m_new] * (width // LANES), axis=1) if width > LANES else m_new[:, :width]
    p = jnp.exp2(s - m_wide)
    acc_ref[...] = (jnp.concatenate([alpha, alpha], axis=1) * acc_ref[...]
                    + jnp.dot(p.astype(v_ext.dtype), v_ext, preferred_element_type=F32))
    m_ref[...] = m_new


def _init_state(m_ref, acc_ref):
    m_ref[...] = jnp.full(m_ref.shape, -jnp.inf, F32)
    acc_ref[...] = jnp.zeros(acc_ref.shape, F32)


def _first_softmax_step(s, v_ext, m_ref, acc_ref):
    m = jnp.max(s, axis=-1, keepdims=True)
    p = jnp.exp2(s - m)
    m_ref[...] = jnp.broadcast_to(m, m_ref.shape)
    acc_ref[...] = jnp.dot(p.astype(v_ext.dtype), v_ext, preferred_element_type=F32)


def _diff_attn_kernel(lam_init, q_ref, qnext_ref, k_ref, v_ref, qm_ref, km_ref, vm_ref, ktail_ref, vtail_ref,
                      lq1_ref, lk1_ref, lq2_ref, lk2_ref, sg_ref, yb_ref, ybm_ref,
                      qs_ref, qn_ref, m_ref, acc_ref, sa_ref, sb_ref, sc_ref):
    g = pl.program_id(2)
    n = ATTN_SUBTILES
    t = q_ref.shape[1] // n
    n_tiles = k_ref.shape[1] // t
    lam = (jnp.exp(jnp.sum(lq1_ref[...] * lk1_ref[...], axis=-1, keepdims=True))
           - jnp.exp(jnp.sum(lq2_ref[...] * lk2_ref[...], axis=-1, keepdims=True)) + lam_init)

    def finish(row0, r, out_ref, out_rows):
        o = (acc_ref[row0:row0 + r, :HEAD_DIM] / acc_ref[row0:row0 + r, HEAD_DIM:]
             - lam * (acc_ref[row0 + t:row0 + t + r, :HEAD_DIM] / acc_ref[row0 + t:row0 + t + r, HEAD_DIM:]))
        out_ref[0, out_rows, :] = (_rmsnorm(o, sg_ref[...]) * (1.0 - lam_init)).astype(out_ref.dtype)

    @pl.when(g == 0)
    def _():
        qm = _stack_subheads(qm_ref[0], N_META)
        for half in range(2):
            rows = slice(half * t, half * t + N_META)
            _init_state(m_ref.at[rows], acc_ref.at[rows])
        sm = _causal_mask(_scores(qm, km_ref[0]))
        for half in range(2):
            rows = slice(half * t, half * t + N_META)
            _online_softmax_step(sm[half * N_META:(half + 1) * N_META], vm_ref[0],
                                 m_ref.at[rows], acc_ref.at[rows])
        finish(0, N_META, ybm_ref, slice(0, N_META))

    def kv_tile(ref, j):
        return ref[0, pl.ds(pl.multiple_of(j * t, t), t), :]

    def step(cur_ref, nxt_ref, j):
        nxt_ref[...] = _scores(qs_ref[...], kv_tile(k_ref, j + 1))
        _online_softmax_step(cur_ref[...], kv_tile(v_ref, j), m_ref, acc_ref)

    def corner(c):
        nxt = n * g + c + 1
        in_range = nxt < n_tiles
        start = pl.multiple_of(jnp.minimum(nxt, n_tiles - 1) * t, t)
        k_first = jnp.where(in_range, k_ref[0, pl.ds(start, N_META), :], ktail_ref[0])
        v_first = jnp.where(in_range, v_ref[0, pl.ds(start, N_META), :], vtail_ref[0])
        for half in range(2):
            r0 = 2 * t * c + half * t + t - N_META
            rows = slice(r0, r0 + N_META)
            s = _scores(qs_ref[rows, :], k_first)
            row = lax.broadcasted_iota(jnp.int32, s.shape, 0)
            col = lax.broadcasted_iota(jnp.int32, s.shape, 1)
            _online_softmax_step(jnp.where(col <= row, s, -jnp.inf), v_first, m_ref.at[rows], acc_ref.at[rows])

    def diagonal_steps(cur_ref, nxt_ref, first):
        for c in range(n):
            rows = slice(2 * t * c, 2 * t * n)
            n_rows = 2 * t * (n - c)
            if c + 1 < n:
                nxt_ref[0:n_rows - 2 * t, :] = _scores(qs_ref[2 * t * (c + 1):2 * t * n, :],
                                                       kv_tile(k_ref, n * g + c + 1))
            else:
                qn_ref[...] = _stack_subheads(qnext_ref[0], t)
                sc_ref[...] = _scores(qn_ref[...], kv_tile(k_ref, 0))
            update = _first_softmax_step if (first and c == 0) else _online_softmax_step
            update(_causal_mask(cur_ref[0:n_rows, :], masked_rows=2 * t, ahead=N_META), kv_tile(v_ref, n * g + c),
                   m_ref.at[rows], acc_ref.at[rows])
            corner(c)
            finish(2 * t * c, t, yb_ref, slice(t * c, t * (c + 1)))
            cur_ref, nxt_ref = nxt_ref, cur_ref

    @pl.when(g == 0)
    def _():
        qs_ref[...] = _stack_subheads(q_ref[0], t)
        sa_ref[...] = _scores(qs_ref[...], kv_tile(k_ref, 0))
        diagonal_steps(sa_ref, sb_ref, first=True)

    @pl.when(g > 0)
    def _():
        qs_ref[...] = qn_ref[...]
        sa_ref[...] = _scores(qs_ref[...], kv_tile(k_ref, 1))
        _first_softmax_step(sc_ref[...], kv_tile(v_ref, 0), m_ref, acc_ref)

        n_quads = (n * g - 1) // 4

        @pl.loop(0, n_quads)
        def _(jj):
            step(sa_ref, sb_ref, 4 * jj + 1)
            step(sb_ref, sa_ref, 4 * jj + 2)
            step(sa_ref, sb_ref, 4 * jj + 3)
            step(sb_ref, sa_ref, 4 * jj + 4)

        @pl.when((n * g - 1) % 4 >= 2)
        def _():
            step(sa_ref, sb_ref, 4 * n_quads + 1)
            step(sb_ref, sa_ref, 4 * n_quads + 2)

        step(sa_ref, sb_ref, n * g - 1)
        diagonal_steps(sb_ref, sa_ref, first=False)


def _diff_attn(q, k, v, qm, km, vm, ktail, vtail, lq1, lk1, lq2, lk2, subln_g, lam_init):
    b, seq, width = q.shape
    t, n = ATTN_TILE, ATTN_SUBTILES
    assert seq % (n * t) == 0 and n % 2 == 0 and width == N_HEADS * HEAD_DIM
    ng = seq // (n * t)
    qgroup = pl.BlockSpec((1, n * t, HEAD_DIM), lambda bi, h, g: (bi, g, h))
    qnext = pl.BlockSpec((1, n * t, HEAD_DIM), lambda bi, h, g: (bi, jnp.minimum(g + 1, ng - 1), h))
    full = lambda w: pl.BlockSpec((1, seq, w), lambda bi, h, g: (bi, 0, h))
    mtile = lambda w: pl.BlockSpec((1, N_META, w), lambda bi, h, g: (bi, 0, h))
    rows = 2 * n * t
    return pl.pallas_call(
        functools.partial(_diff_attn_kernel, lam_init),
        grid=(b, N_HEADS, ng),
        in_specs=[qgroup, qnext, full(HEAD_DIM), full(2 * HEAD_DIM), mtile(HEAD_DIM), mtile(HEAD_DIM), mtile(2 * HEAD_DIM),
                  mtile(HEAD_DIM), mtile(2 * HEAD_DIM), _const_spec(lq1.shape), _const_spec(lk1.shape), _const_spec(lq2.shape),
                  _const_spec(lk2.shape), _const_spec(subln_g.shape)],
        out_specs=[qgroup, mtile(HEAD_DIM)],
        out_shape=[jax.ShapeDtypeStruct((b, seq, width), BF16),
                   jax.ShapeDtypeStruct((b, N_META, width), BF16)],
        scratch_shapes=[pltpu.VMEM((rows, HEAD_DIM), BF16), pltpu.VMEM((rows, HEAD_DIM), BF16),
                        pltpu.VMEM((rows, LANES), F32), pltpu.VMEM((rows, 2 * HEAD_DIM), F32),
                        pltpu.VMEM((rows, t), F32), pltpu.VMEM((rows, t), F32), pltpu.VMEM((rows, t), F32)],
        compiler_params=pltpu.CompilerParams(
            dimension_semantics=("arbitrary", "arbitrary", "arbitrary"),
            vmem_limit_bytes=VMEM_LIMIT_BYTES),
        name="diff_attn",
    )(q, q, k, v, qm, km, vm, ktail, vtail, lq1, lk1, lq2, lk2, subln_g)


def _mix_out_kernel(x_ref, ya_ref, yb_ref, meta_ref, yam_ref, ybm_ref, wo_ref, g_ref,
                    wup_ref, cw_ref, cb_ref, wdown_ref, out_ref, nperm_ref, next_ref, act_ref, operm_ref):
    tm = x_ref.shape[1]
    d = x_ref.shape[2]
    pool_w = ya_ref.shape[-1]
    d_ff = wdown_ref.shape[0]
    n_slabs = d // LANES
    nv = tm // SUBLANES

    def mix_residual(rows, ya, yb):
        h1 = (rows + jnp.dot(ya, wo_ref[:pool_w, :], preferred_element_type=F32)
              + jnp.dot(yb, wo_ref[pool_w:, :], preferred_element_type=F32))
        return h1, _rmsnorm(h1, g_ref[...])

    @pl.when(pl.program_id(1) == 0)
    def _():
        _, nm = mix_residual(meta_ref[...], yam_ref[0], ybm_ref[0])
        halo = jnp.concatenate([nm[N_META - 9:N_META - 1], nm[N_META - 8:]], axis=0)
        for c in range(n_slabs):
            nperm_ref[c, 0:HALO, :] = halo[:, c * LANES:(c + 1) * LANES]

    h1, n2 = mix_residual(x_ref[0], ya_ref[0], yb_ref[0])
    for s in range(SUBLANES):
        for v0 in range(0, nv, SUBLANES):
            t0 = s * nv + v0
            for c in range(n_slabs):
                nperm_ref[c, pl.ds(HALO + v0 * SUBLANES + s, SUBLANES, stride=SUBLANES), :] = (
                    n2[t0:t0 + SUBLANES, c * LANES:(c + 1) * LANES])
    next_ref[...] = jnp.concatenate([nperm_ref[c] for c in range(n_slabs)], axis=1).astype(BF16)

    def conv(u, cols):
        cur = u[HALO:]
        wrap1 = jnp.concatenate([u[HALO - 1:HALO], u[HALO + tm - 8:HALO + tm - 1]], axis=0)
        wrap2 = jnp.concatenate([u[HALO - 9:HALO - 8], u[HALO + tm - 16:HALO + tm - 9]], axis=0)
        back1 = jnp.concatenate([wrap1, u[HALO:HALO + tm - 8]], axis=0)
        back2 = jnp.concatenate([wrap2, wrap1, u[HALO:HALO + tm - 16]], axis=0)
        return (cb_ref[:, cols] + back2 * cw_ref[0:1, cols] + back1 * cw_ref[1:2, cols]
                + cur * cw_ref[2:3, cols])

    for c in range(d_ff // FF_CHUNK):
        gcols = slice(c * FF_CHUNK, (c + 1) * FF_CHUNK)
        vcols = slice(d_ff + c * FF_CHUNK, d_ff + (c + 1) * FF_CHUNK)
        ug = jnp.dot(next_ref[...], wup_ref[:, gcols], preferred_element_type=F32)
        uv = jnp.dot(next_ref[...], wup_ref[:, vcols], preferred_element_type=F32)
        gate = conv(ug, gcols)
        act_ref[:, gcols] = (gate * jax.nn.sigmoid(gate) * conv(uv, vcols)).astype(BF16)
    ffn = jnp.dot(act_ref[...], wdown_ref[...], preferred_element_type=F32)
    for c in range(n_slabs):
        operm_ref[c] = ffn[:, c * LANES:(c + 1) * LANES]
        nperm_ref[c, 0:HALO, :] = nperm_ref[c, tm:tm + HALO, :]
    for s in range(SUBLANES):
        for v0 in range(0, nv, SUBLANES):
            t0 = s * nv + v0
            for c in range(n_slabs):
                lanes = slice(c * LANES, (c + 1) * LANES)
                out_ref[0, t0:t0 + SUBLANES, lanes] = (
                    h1[t0:t0 + SUBLANES, lanes]
                    + operm_ref[c, pl.ds(v0 * SUBLANES + s, SUBLANES, stride=SUBLANES), :])


def _mix_out(x, ya, yb, meta, yam, ybm, w_out, g, w_up, conv_w, conv_b, w_down):
    b, seq, d = x.shape
    pool_w = ya.shape[-1]
    tm = SEQ_TILE
    d_ff = w_down.shape[1]
    assert seq % tm == 0 and d_ff % FF_CHUNK == 0
    tile = lambda w: pl.BlockSpec((1, tm, w), lambda bi, i: (bi, i, 0))
    mtile = lambda w: pl.BlockSpec((1, N_META, w), lambda bi, i: (bi, 0, 0))
    return pl.pallas_call(
        _mix_out_kernel,
        grid=(b, seq // tm),
        in_specs=[tile(d), tile(pool_w), tile(yb.shape[-1]), _const_spec(meta.shape),
                  mtile(pool_w), mtile(ybm.shape[-1]), _layer_const_spec(w_out.shape), _const_spec(g.shape),
                  _layer_const_spec(w_up.shape), _const_spec(conv_w.shape), _const_spec(conv_b.shape),
                  _layer_const_spec(w_down.shape)],
        out_specs=tile(d),
        out_shape=jax.ShapeDtypeStruct((b, seq, d), x.dtype),
        scratch_shapes=[pltpu.VMEM((d // LANES, HALO + tm, LANES), F32), pltpu.VMEM((HALO + tm, d), BF16),
                        pltpu.VMEM((tm, d_ff), BF16), pltpu.VMEM((d // LANES, tm, LANES), F32)],
        compiler_params=pltpu.CompilerParams(
            dimension_semantics=("arbitrary", "arbitrary"), vmem_limit_bytes=VMEM_LIMIT_BYTES),
        name="mix_out",
    )(x, ya, yb, meta, yam, ybm, w_out, g, w_up, conv_w, conv_b, w_down)


def _pair_block_diagonal(w):
    groups, c, _ = w.shape
    pairs = w.reshape(groups // 2, 2, c, c)
    zero = jnp.zeros_like(pairs[:, 0])
    return jnp.concatenate([jnp.concatenate([pairs[:, 0], zero], axis=2),
                            jnp.concatenate([zero, pairs[:, 1]], axis=2)], axis=1)


def kernel(x, meta_tokens, norm_mix_g, w_in, w_pool, b_pool, pool_scale, q_norm_g, k_norm_g,
           lambda_q1, lambda_k1, lambda_q2, lambda_k2, subln_g, w_out, norm_ffn_g,
           w_up, conv_w, conv_b, w_down):
    depth = w_in.shape[0]
    assert depth == 1 and meta_tokens.shape[0] == N_META
    row = lambda a: a.reshape(1, -1).astype(F32)
    h = x
    meta = meta_tokens.astype(x.dtype)
    for i in range(depth):
        lam_init = 0.8 - 0.6 * math.exp(-0.3 * i)
        qg = jnp.tile(row(q_norm_g[i]), (1, 2)) * (QK_DIM ** -0.5 * math.log2(math.e))
        kg = jnp.tile(row(k_norm_g[i]), (1, 2))
        layer = slice(i, i + 1)
        (ya, q, k, v, yam, qm, km, vm, kt, vt), (w_out_n, w_up_n, w_down_n) = _mix_in(
            h, meta, row(norm_mix_g[i]), w_in[layer], _pair_block_diagonal(w_pool[i]),
            row(b_pool[i]), row(pool_scale[i]), qg, kg, (w_out[layer], w_up[layer], w_down[layer]))
        yb, ybm = _diff_attn(q, k, v, qm, km, vm, kt, vt, row(lambda_q1[i]), row(lambda_k1[i]),
                             row(lambda_q2[i]), row(lambda_k2[i]), row(subln_g[i]), lam_init)
        h = _mix_out(h, ya, yb, meta, yam, ybm, w_out_n, row(norm_ffn_g[i]),
                     w_up_n, conv_w[i].astype(F32), row(conv_b[i]), w_down_n)
    return h
```

```python
import functools
import math

import jax
import jax.numpy as jnp
from jax import lax
from jax.experimental import pallas as pl
from jax.experimental.pallas import tpu as pltpu

F32 = jnp.float32
BF16 = jnp.bfloat16

N_META = 16
POOL_WINDOWS = (2, 4, 8, 16)
N_HEADS = 4
HEAD_DIM = 128
QK_DIM = 64
CONV_WIDTH = 3
EPS = 1e-6

LANES = 128
SUBLANES = 8
HALO = 16
VMEM_LIMIT_BYTES = 56 * 1024 * 1024

IN_TILE = 1024
SEQ_TILE = 512
ATTN_TILE = 512
ATTN_SUBTILES = 2
FF_CHUNK = 256
BF16_SUBLANES = 16
N_LATER_WEIGHTS = 3
N_MIX_IN_OUTS = 10


def _rmsnorm(x, g):
    return x * lax.rsqrt(jnp.mean(x * x, axis=-1, keepdims=True) + EPS) * g


def _const_spec(shape):
    nd = len(shape)
    return pl.BlockSpec(shape, lambda *_: (0,) * nd, pipeline_mode=pl.Buffered(1))


def _layer_const_spec(shape):
    nd = len(shape)
    return pl.BlockSpec((None,) + tuple(shape[1:]), lambda *_: (0,) * nd, pipeline_mode=pl.Buffered(1))


def _mix_in_kernel(x_ref, meta_ref, g_ref, win_ref, wpool_ref, bpool_ref, pscale_ref,
                   qg_ref, kg_ref, *refs):
    later_w = refs[:N_LATER_WEIGHTS]
    (ya_ref, q_ref, k_ref, v_ref, yam_ref, qm_ref, km_ref, vm_ref,
     ktail_ref, vtail_ref) = refs[N_LATER_WEIGHTS:N_LATER_WEIGHTS + N_MIX_IN_OUTS]
    later_w_bf16 = refs[N_LATER_WEIGHTS + N_MIX_IN_OUTS:2 * N_LATER_WEIGHTS + N_MIX_IN_OUTS]
    uext_ref, win_bf16_ref, kcarry_ref, vcarry_ref = refs[-4:]
    pool_w = bpool_ref.shape[-1]
    qk_w = q_ref.shape[-1]

    for src, dst in zip(later_w, later_w_bf16):
        dst[...] = src[...].astype(dst.dtype)

    @pl.when((pl.program_id(0) == 0) & (pl.program_id(1) == 0))
    def _():
        win_bf16_ref[...] = win_ref[...].astype(BF16)

    def put_shifted(out, cols, val, carry_ref, tail_ref):
        r = val.shape[0]
        out[0, 0:N_META, cols] = carry_ref[:, cols]
        out[0, N_META:, cols] = val[:r - N_META]
        carry_ref[:, cols] = val[r - N_META:]
        tail_ref[0, :, cols] = val[r - N_META:]

    def heads(rows, is_meta, ya_out, q_out, k_out, v_out):
        r = rows.shape[0]
        inv_rms = lax.rsqrt(jnp.mean(rows * rows, axis=-1, keepdims=True) + EPS)
        proj = inv_rms * jnp.dot((rows * g_ref[...]).astype(BF16), win_bf16_ref[...],
                                 preferred_element_type=F32)
        u = proj[:, :pool_w]
        uext_ref[HALO:HALO + r, :] = u
        for gi, w in enumerate(POOL_WINDOWS):
            cols = slice(gi * LANES, (gi + 1) * LANES)
            win_sum = u[:, cols]
            for j in range(1, w):
                win_sum = win_sum + uext_ref[HALO - j:HALO - j + r, cols]
            if is_meta:
                pos = lax.broadcasted_iota(jnp.int32, (r, 1), 0)
                mean = win_sum / jnp.minimum(pos + 1, w).astype(F32)
            else:
                mean = win_sum * (1.0 / w)
            pooled = mean - u[:, cols]
            y = jnp.dot(pooled.astype(BF16), wpool_ref[gi].astype(BF16), preferred_element_type=F32)
            y = (y + bpool_ref[:, cols]) * pscale_ref[:, cols]
            ya_out[0, :, cols] = y.astype(ya_out.dtype)
        uext_ref[0:HALO, :] = uext_ref[r:r + HALO, :]

        lane = lax.broadcasted_iota(jnp.int32, (r, LANES), 1)
        lo = lane < QK_DIM
        for src, gain_ref, out in ((proj[:, pool_w:pool_w + qk_w], qg_ref, q_out),
                                   (proj[:, pool_w + qk_w:pool_w + 2 * qk_w], kg_ref, k_out)):
            for h in range(N_HEADS):
                cols = slice(h * LANES, (h + 1) * LANES)
                slab = src[:, cols]
                sq = slab * slab
                s_lo = jnp.sum(jnp.where(lo, sq, 0.0), axis=-1, keepdims=True)
                s_hi = jnp.sum(jnp.where(lo, 0.0, sq), axis=-1, keepdims=True)
                ms = jnp.where(lo, s_lo, s_hi) * (1.0 / QK_DIM)
                normed = (slab * lax.rsqrt(ms + EPS) * gain_ref[...]).astype(out.dtype)
                if out is k_out and not is_meta:
                    put_shifted(out, cols, normed, kcarry_ref, ktail_ref)
                else:
                    out[0, :, cols] = normed
        for h in range(N_HEADS):
            vcols = slice(pool_w + 2 * qk_w + h * HEAD_DIM, pool_w + 2 * qk_w + (h + 1) * HEAD_DIM)
            out_cols = slice(2 * h * HEAD_DIM, (2 * h + 1) * HEAD_DIM)
            one_cols = slice((2 * h + 1) * HEAD_DIM, (2 * h + 2) * HEAD_DIM)
            vals = proj[:, vcols].astype(v_out.dtype)
            if is_meta:
                v_out[0, :, out_cols] = vals
            else:
                put_shifted(v_out, out_cols, vals, vcarry_ref, vtail_ref)
                vtail_ref[0, :, one_cols] = jnp.ones((N_META, HEAD_DIM), v_out.dtype)
            v_out[0, :, one_cols] = jnp.ones((r, HEAD_DIM), v_out.dtype)

    @pl.when(pl.program_id(1) == 0)
    def _():
        uext_ref[0:HALO, :] = jnp.zeros((HALO, pool_w), F32)
        heads(meta_ref[...], True, yam_ref, qm_ref, km_ref, vm_ref)
        kcarry_ref[...] = km_ref[0]
        vcarry_ref[...] = vm_ref[0]

    heads(x_ref[0], False, ya_ref, q_ref, k_ref, v_ref)


def _mix_in(x, meta, g, w_in, w_pool, b_pool, pool_scale, qg, kg, later_weights):
    b, seq, d = x.shape
    pool_w = b_pool.shape[-1]
    qk_w = N_HEADS * HEAD_DIM
    tm = IN_TILE
    nt = seq // tm
    assert seq % tm == 0 and len(later_weights) == N_LATER_WEIGHTS
    tile = lambda w: pl.BlockSpec((1, tm, w), lambda bi, i: (bi, i, 0))
    mtile = lambda w: pl.BlockSpec((1, N_META, w), lambda bi, i: (bi, 0, 0))
    big = lambda w: jax.ShapeDtypeStruct((b, seq, w), BF16)
    small = lambda w: jax.ShapeDtypeStruct((b, N_META, w), BF16)
    def wblock(w):
        rows, steps = w.shape[1], b * nt
        hold = next(h for h in (1, 2, 4, 8) if (rows * h) % (steps * BF16_SUBLANES) == 0)
        return pl.BlockSpec((None, rows * hold // steps, w.shape[2]), lambda bi, i: (0, (bi * nt + i) // hold, 0))
    outs = pl.pallas_call(
        _mix_in_kernel,
        grid=(b, nt),
        in_specs=[tile(d), _const_spec(meta.shape), _const_spec(g.shape), _layer_const_spec(w_in.shape),
                  _layer_const_spec(w_pool.shape), _const_spec(b_pool.shape), _const_spec(pool_scale.shape),
                  _const_spec(qg.shape), _const_spec(kg.shape)] + [wblock(w) for w in later_weights],
        out_specs=[tile(pool_w), tile(qk_w), tile(qk_w), tile(2 * qk_w),
                   mtile(pool_w), mtile(qk_w), mtile(qk_w), mtile(2 * qk_w), mtile(qk_w), mtile(2 * qk_w)]
                  + [wblock(w) for w in later_weights],
        out_shape=[big(pool_w), big(qk_w), big(qk_w), big(2 * qk_w),
                   small(pool_w), small(qk_w), small(qk_w), small(2 * qk_w), small(qk_w), small(2 * qk_w)]
                  + [jax.ShapeDtypeStruct(w.shape, BF16) for w in later_weights],
        scratch_shapes=[pltpu.VMEM((HALO + tm, pool_w), F32), pltpu.VMEM(w_in.shape[1:], BF16),
                        pltpu.VMEM((N_META, qk_w), BF16), pltpu.VMEM((N_META, 2 * qk_w), BF16)],
        compiler_params=pltpu.CompilerParams(
            dimension_semantics=("arbitrary", "arbitrary"), vmem_limit_bytes=VMEM_LIMIT_BYTES),
        name="mix_in",
    )(x, meta, g, w_in, w_pool, b_pool, pool_scale, qg, kg, *later_weights)
    return outs[:N_MIX_IN_OUTS], outs[N_MIX_IN_OUTS:]


def _stack_subheads(q, sub):
    lane = lax.broadcasted_iota(jnp.int32, (sub, q.shape[1]), 1)
    zero = jnp.zeros((sub, q.shape[1]), q.dtype)
    parts = []
    for r0 in range(0, q.shape[0], sub):
        part = q[r0:r0 + sub]
        parts += [jnp.where(lane < QK_DIM, part, zero), jnp.where(lane < QK_DIM, zero, part)]
    return jnp.concatenate(parts, axis=0)


def _scores(qs, k):
    return lax.dot_general(qs, k, (((1,), (1,)), ((), ())), preferred_element_type=F32)


def _causal_mask(s, masked_rows=None, ahead=0):
    sub = s.shape[1]
    row = lax.broadcasted_iota(jnp.int32, s.shape, 0)
    col = lax.broadcasted_iota(jnp.int32, s.shape, 1)
    visible = col <= (row & (sub - 1)) + ahead
    if masked_rows is not None and masked_rows < s.shape[0]:
        visible = visible | (row >= masked_rows)
    return jnp.where(visible, s, -jnp.inf)


def _online_softmax_step(s, v_ext, m_ref, acc_ref):
    width = s.shape[1]
    m = m_ref[...]
    m_new = jnp.maximum(m, jnp.max(s, axis=-1, keepdims=True))
    alpha = jnp.exp2(m - m_new)
    m_wide = jnp.concatenate([m_new] * (width // LANES), axis=1) if width > LANES else m_new[:, :width]
    p = jnp.exp2(s - m_wide)
    acc_ref[...] = (jnp.concatenate([alpha, alpha], axis=1) * acc_ref[...]
                    + jnp.dot(p.astype(v_ext.dtype), v_ext, preferred_element_type=F32))
    m_ref[...] = m_new


def _init_state(m_ref, acc_ref):
    m_ref[...] = jnp.full(m_ref.shape, -jnp.inf, F32)
    acc_ref[...] = jnp.zeros(acc_ref.shape, F32)


def _first_softmax_step(s, v_ext, m_ref, acc_ref):
    m = jnp.max(s, axis=-1, keepdims=True)
    p = jnp.exp2(s - m)
    m_ref[...] = jnp.broadcast_to(m, m_ref.shape)
    acc_ref[...] = jnp.dot(p.astype(v_ext.dtype), v_ext, preferred_element_type=F32)


def _diff_attn_kernel(lam_init, q_ref, qnext_ref, knext_ref, k_ref, v_ref, qm_ref, km_ref, vm_ref, ktail_ref, vtail_ref,
                      lq1_ref, lk1_ref, lq2_ref, lk2_ref, sg_ref, yb_ref, ybm_ref,
                      qs_ref, qn_ref, m_ref, acc_ref, sa_ref, sb_ref, sc_ref):
    g = pl.program_id(2)
    n = ATTN_SUBTILES
    t = q_ref.shape[1] // n
    n_tiles = k_ref.shape[1] // t
    lam = (jnp.exp(jnp.sum(lq1_ref[...] * lk1_ref[...], axis=-1, keepdims=True))
           - jnp.exp(jnp.sum(lq2_ref[...] * lk2_ref[...], axis=-1, keepdims=True)) + lam_init)

    def finish(row0, r, out_ref, out_rows):
        o = (acc_ref[row0:row0 + r, :HEAD_DIM] / acc_ref[row0:row0 + r, HEAD_DIM:]
             - lam * (acc_ref[row0 + t:row0 + t + r, :HEAD_DIM] / acc_ref[row0 + t:row0 + t + r, HEAD_DIM:]))
        out_ref[0, out_rows, :] = (_rmsnorm(o, sg_ref[...]) * (1.0 - lam_init)).astype(out_ref.dtype)

    @pl.when(g == 0)
    def _():
        qm = _stack_subheads(qm_ref[0], N_META)
        for half in range(2):
            rows = slice(half * t, half * t + N_META)
            _init_state(m_ref.at[rows], acc_ref.at[rows])
        sm = _causal_mask(_scores(qm, km_ref[0]))
        for half in range(2):
            rows = slice(half * t, half * t + N_META)
            _online_softmax_step(sm[half * N_META:(half + 1) * N_META], vm_ref[0],
                                 m_ref.at[rows], acc_ref.at[rows])
        finish(0, N_META, ybm_ref, slice(0, N_META))

    def kv_tile(ref, j):
        return ref[0, pl.ds(pl.multiple_of(j * t, t), t), :]

    def step(cur_ref, nxt_ref, j):
        nxt_ref[...] = _scores(qs_ref[...], kv_tile(k_ref, j + 1))
        _online_softmax_step(cur_ref[...], kv_tile(v_ref, j), m_ref, acc_ref)

    def corner(c):
        nxt = n * g + c + 1
        in_range = nxt < n_tiles
        start = pl.multiple_of(jnp.minimum(nxt, n_tiles - 1) * t, t)
        k_first = jnp.where(in_range, k_ref[0, pl.ds(start, N_META), :], ktail_ref[0])
        v_first = jnp.where(in_range, v_ref[0, pl.ds(start, N_META), :], vtail_ref[0])
        for half in range(2):
            r0 = 2 * t * c + half * t + t - N_META
            rows = slice(r0, r0 + N_META)
            s = _scores(qs_ref[rows, :], k_first)
            row = lax.broadcasted_iota(jnp.int32, s.shape, 0)
            col = lax.broadcasted_iota(jnp.int32, s.shape, 1)
            _online_softmax_step(jnp.where(col <= row, s, -jnp.inf), v_first, m_ref.at[rows], acc_ref.at[rows])

    def diagonal_steps(cur_ref, nxt_ref, first):
        for c in range(n):
            rows = slice(2 * t * c, 2 * t * n)
            n_rows = 2 * t * (n - c)
            if c + 1 < n:
                nxt_ref[0:n_rows - 2 * t, :] = _scores(qs_ref[2 * t * (c + 1):2 * t * n, :],
                                                       kv_tile(k_ref, n * g + c + 1))
            else:
                qn_ref[...] = _stack_subheads(qnext_ref[0], t)
                same_head = g + 1 < pl.num_programs(2)
                sc_ref[...] = _scores(qn_ref[...], jnp.where(same_head, kv_tile(k_ref, 0), knext_ref[0]))
            update = _first_softmax_step if (first and c == 0) else _online_softmax_step
            update(_causal_mask(cur_ref[0:n_rows, :], masked_rows=2 * t, ahead=N_META), kv_tile(v_ref, n * g + c),
                   m_ref.at[rows], acc_ref.at[rows])
            corner(c)
            finish(2 * t * c, t, yb_ref, slice(t * c, t * (c + 1)))
            cur_ref, nxt_ref = nxt_ref, cur_ref

    very_first = (pl.program_id(0) == 0) & (pl.program_id(1) == 0)

    @pl.when((g == 0) & very_first)
    def _():
        qs_ref[...] = _stack_subheads(q_ref[0], t)
        sa_ref[...] = _scores(qs_ref[...], kv_tile(k_ref, 0))
        diagonal_steps(sa_ref, sb_ref, first=True)

    @pl.when((g == 0) & jnp.logical_not(very_first))
    def _():
        qs_ref[...] = qn_ref[...]
        diagonal_steps(sc_ref, sb_ref, first=True)

    @pl.when(g > 0)
    def _():
        qs_ref[...] = qn_ref[...]
        sa_ref[...] = _scores(qs_ref[...], kv_tile(k_ref, 1))
        _first_softmax_step(sc_ref[...], kv_tile(v_ref, 0), m_ref, acc_ref)

        n_quads = (n * g - 1) // 4

        @pl.loop(0, n_quads)
        def _(jj):
            step(sa_ref, sb_ref, 4 * jj + 1)
            step(sb_ref, sa_ref, 4 * jj + 2)
            step(sa_ref, sb_ref, 4 * jj + 3)
            step(sb_ref, sa_ref, 4 * jj + 4)

        @pl.when((n * g - 1) % 4 >= 2)
        def _():
            step(sa_ref, sb_ref, 4 * n_quads + 1)
            step(sb_ref, sa_ref, 4 * n_quads + 2)

        step(sa_ref, sb_ref, n * g - 1)
        diagonal_steps(sb_ref, sa_ref, first=False)


def _diff_attn(q, k, v, qm, km, vm, ktail, vtail, lq1, lk1, lq2, lk2, subln_g, lam_init):
    b, seq, width = q.shape
    t, n = ATTN_TILE, ATTN_SUBTILES
    assert seq % (n * t) == 0 and n % 2 == 0 and width == N_HEADS * HEAD_DIM
    ng = seq // (n * t)
    qgroup = pl.BlockSpec((1, n * t, HEAD_DIM), lambda bi, h, g: (bi, g, h))

    def following(bi, h, g):
        wrap_g = g == ng - 1
        wrap_h = wrap_g & (h == N_HEADS - 1)
        return (jnp.where(wrap_h, jnp.minimum(bi + 1, b - 1), bi),
                jnp.where(wrap_g, jnp.where(wrap_h, 0, h + 1), h),
                jnp.where(wrap_g, 0, g + 1))

    def qnext_index(bi, h, g):
        b2, h2, g2 = following(bi, h, g)
        return b2, g2, h2

    def knext_index(bi, h, g):
        b2, h2, _ = following(bi, h, g)
        return b2, 0, h2

    qnext = pl.BlockSpec((1, n * t, HEAD_DIM), qnext_index)
    knext = pl.BlockSpec((1, t, HEAD_DIM), knext_index)
    full = lambda w: pl.BlockSpec((1, seq, w), lambda bi, h, g: (bi, 0, h))
    mtile = lambda w: pl.BlockSpec((1, N_META, w), lambda bi, h, g: (bi, 0, h))
    rows = 2 * n * t
    return pl.pallas_call(
        functools.partial(_diff_attn_kernel, lam_init),
        grid=(b, N_HEADS, ng),
        in_specs=[qgroup, qnext, knext, full(HEAD_DIM), full(2 * HEAD_DIM), mtile(HEAD_DIM), mtile(HEAD_DIM), mtile(2 * HEAD_DIM),
                  mtile(HEAD_DIM), mtile(2 * HEAD_DIM), _const_spec(lq1.shape), _const_spec(lk1.shape), _const_spec(lq2.shape),
                  _const_spec(lk2.shape), _const_spec(subln_g.shape)],
        out_specs=[qgroup, mtile(HEAD_DIM)],
        out_shape=[jax.ShapeDtypeStruct((b, seq, width), BF16),
                   jax.ShapeDtypeStruct((b, N_META, width), BF16)],
        scratch_shapes=[pltpu.VMEM((rows, HEAD_DIM), BF16), pltpu.VMEM((rows, HEAD_DIM), BF16),
                        pltpu.VMEM((rows, LANES), F32), pltpu.VMEM((rows, 2 * HEAD_DIM), F32),
                        pltpu.VMEM((rows, t), F32), pltpu.VMEM((rows, t), F32), pltpu.VMEM((rows, t), F32)],
        compiler_params=pltpu.CompilerParams(
            dimension_semantics=("arbitrary", "arbitrary", "arbitrary"),
            vmem_limit_bytes=VMEM_LIMIT_BYTES),
        name="diff_attn",
    )(q, q, k, k, v, qm, km, vm, ktail, vtail, lq1, lk1, lq2, lk2, subln_g)


def _mix_out_kernel(x_ref, ya_ref, yb_ref, meta_ref, yam_ref, ybm_ref, wo_ref, g_ref,
                    wup_ref, cw_ref, cb_ref, wdown_ref, out_ref, nperm_ref, next_ref, act_ref, operm_ref):
    tm = x_ref.shape[1]
    d = x_ref.shape[2]
    pool_w = ya_ref.shape[-1]
    d_ff = wdown_ref.shape[0]
    n_slabs = d // LANES
    nv = tm // SUBLANES

    def mix_residual(rows, ya, yb):
        h1 = (rows + jnp.dot(ya, wo_ref[:pool_w, :], preferred_element_type=F32)
              + jnp.dot(yb, wo_ref[pool_w:, :], preferred_element_type=F32))
        return h1, _rmsnorm(h1, g_ref[...])

    @pl.when(pl.program_id(1) == 0)
    def _():
        _, nm = mix_residual(meta_ref[...], yam_ref[0], ybm_ref[0])
        halo = jnp.concatenate([nm[N_META - 9:N_META - 1], nm[N_META - 8:]], axis=0)
        for c in range(n_slabs):
            nperm_ref[c, 0:HALO, :] = halo[:, c * LANES:(c + 1) * LANES]

    h1, n2 = mix_residual(x_ref[0], ya_ref[0], yb_ref[0])
    for s in range(SUBLANES):
        for v0 in range(0, nv, SUBLANES):
            t0 = s * nv + v0
            for c in range(n_slabs):
                nperm_ref[c, pl.ds(HALO + v0 * SUBLANES + s, SUBLANES, stride=SUBLANES), :] = (
                    n2[t0:t0 + SUBLANES, c * LANES:(c + 1) * LANES])
    next_ref[...] = jnp.concatenate([nperm_ref[c] for c in range(n_slabs)], axis=1).astype(BF16)

    def conv(u, cols):
        cur = u[HALO:]
        wrap1 = jnp.concatenate([u[HALO - 1:HALO], u[HALO + tm - 8:HALO + tm - 1]], axis=0)
        wrap2 = jnp.concatenate([u[HALO - 9:HALO - 8], u[HALO + tm - 16:HALO + tm - 9]], axis=0)
        back1 = jnp.concatenate([wrap1, u[HALO:HALO + tm - 8]], axis=0)
        back2 = jnp.concatenate([wrap2, wrap1, u[HALO:HALO + tm - 16]], axis=0)
        return (cb_ref[:, cols] + back2 * cw_ref[0:1, cols] + back1 * cw_ref[1:2, cols]
                + cur * cw_ref[2:3, cols])

    for c in range(d_ff // FF_CHUNK):
        gcols = slice(c * FF_CHUNK, (c + 1) * FF_CHUNK)
        vcols = slice(d_ff + c * FF_CHUNK, d_ff + (c + 1) * FF_CHUNK)
        ug = jnp.dot(next_ref[...], wup_ref[:, gcols], preferred_element_type=F32)
        uv = jnp.dot(next_ref[...], wup_ref[:, vcols], preferred_element_type=F32)
        gate = conv(ug, gcols)
        act_ref[:, gcols] = (gate * jax.nn.sigmoid(gate) * conv(uv, vcols)).astype(BF16)
    ffn = jnp.dot(act_ref[...], wdown_ref[...], preferred_element_type=F32)
    for c in range(n_slabs):
        operm_ref[c] = ffn[:, c * LANES:(c + 1) * LANES]
        nperm_ref[c, 0:HALO, :] = nperm_ref[c, tm:tm + HALO, :]
    for s in range(SUBLANES):
        for v0 in range(0, nv, SUBLANES):
            t0 = s * nv + v0
            for c in range(n_slabs):
                lanes = slice(c * LANES, (c + 1) * LANES)
                out_ref[0, t0:t0 + SUBLANES, lanes] = (
                    h1[t0:t0 + SUBLANES, lanes]
                    + operm_ref[c, pl.ds(v0 * SUBLANES + s, SUBLANES, stride=SUBLANES), :])


def _mix_out(x, ya, yb, meta, yam, ybm, w_out, g, w_up, conv_w, conv_b, w_down):
    b, seq, d = x.shape
    pool_w = ya.shape[-1]
    tm = SEQ_TILE
    d_ff = w_down.shape[1]
    assert seq % tm == 0 and d_ff % FF_CHUNK == 0
    tile = lambda w: pl.BlockSpec((1, tm, w), lambda bi, i: (bi, i, 0))
    mtile = lambda w: pl.BlockSpec((1, N_META, w), lambda bi, i: (bi, 0, 0))
    return pl.pallas_call(
        _mix_out_kernel,
        grid=(b, seq // tm),
        in_specs=[tile(d), tile(pool_w), tile(yb.shape[-1]), _const_spec(meta.shape),
                  mtile(pool_w), mtile(ybm.shape[-1]), _layer_const_spec(w_out.shape), _const_spec(g.shape),
                  _layer_const_spec(w_up.shape), _const_spec(conv_w.shape), _const_spec(conv_b.shape),
                  _layer_const_spec(w_down.shape)],
        out_specs=tile(d),
        out_shape=jax.ShapeDtypeStruct((b, seq, d), x.dtype),
        scratch_shapes=[pltpu.VMEM((d // LANES, HALO + tm, LANES), F32), pltpu.VMEM((HALO + tm, d), BF16),
                        pltpu.VMEM((tm, d_ff), BF16), pltpu.VMEM((d // LANES, tm, LANES), F32)],
        compiler_params=pltpu.CompilerParams(
            dimension_semantics=("arbitrary", "arbitrary"), vmem_limit_bytes=VMEM_LIMIT_BYTES),
        name="mix_out",
    )(x, ya, yb, meta, yam, ybm, w_out, g, w_up, conv_w, conv_b, w_down)


def kernel(x, meta_tokens, norm_mix_g, w_in, w_pool, b_pool, pool_scale, q_norm_g, k_norm_g,
           lambda_q1, lambda_k1, lambda_q2, lambda_k2, subln_g, w_out, norm_ffn_g,
           w_up, conv_w, conv_b, w_down):
    depth = w_in.shape[0]
    assert depth == 1 and meta_tokens.shape[0] == N_META
    row = lambda a: a.reshape(1, -1).astype(F32)
    h = x
    meta = meta_tokens.astype(x.dtype)
    for i in range(depth):
        lam_init = 0.8 - 0.6 * math.exp(-0.3 * i)
        qg = jnp.tile(row(q_norm_g[i]), (1, 2)) * (QK_DIM ** -0.5 * math.log2(math.e))
        kg = jnp.tile(row(k_norm_g[i]), (1, 2))
        layer = slice(i, i + 1)
        (ya, q, k, v, yam, qm, km, vm, kt, vt), (w_out_n, w_up_n, w_down_n) = _mix_in(
            h, meta, row(norm_mix_g[i]), w_in[layer], w_pool[layer],
            row(b_pool[i]), row(pool_scale[i]), qg, kg, (w_out[layer], w_up[layer], w_down[layer]))
        yb, ybm = _diff_attn(q, k, v, qm, km, vm, kt, vt, row(lambda_q1[i]), row(lambda_k1[i]),
                             row(lambda_q2[i]), row(lambda_k2[i]), row(subln_g[i]), lam_init)
        h = _mix_out(h, ya, yb, meta, yam, ybm, w_out_n, row(norm_ffn_g[i]),
                     w_up_n, conv_w[i].astype(F32), row(conv_b[i]), w_down_n)
    return h
```

```python
import functools
import math

import jax
import jax.numpy as jnp
from jax import lax
from jax.experimental import pallas as pl
from jax.experimental.pallas import tpu as pltpu

F32 = jnp.float32
BF16 = jnp.bfloat16

N_META = 16
POOL_WINDOWS = (2, 4, 8, 16)
N_HEADS = 4
HEAD_DIM = 128
QK_DIM = 64
CONV_WIDTH = 3
EPS = 1e-6

LANES = 128
SUBLANES = 8
HALO = 16
VMEM_LIMIT_BYTES = 56 * 1024 * 1024

IN_TILE = 1024
SEQ_TILE = 512
ATTN_TILE = 512
ATTN_SUBTILES = 2
FF_CHUNK = 256
BF16_SUBLANES = 16
N_LATER_WEIGHTS = 3
N_MIX_IN_OUTS = 10


def _rmsnorm(x, g):
    return x * lax.rsqrt(jnp.mean(x * x, axis=-1, keepdims=True) + EPS) * g


def _const_spec(shape):
    nd = len(shape)
    return pl.BlockSpec(shape, lambda *_: (0,) * nd, pipeline_mode=pl.Buffered(1))


def _layer_const_spec(shape):
    nd = len(shape)
    return pl.BlockSpec((None,) + tuple(shape[1:]), lambda *_: (0,) * nd, pipeline_mode=pl.Buffered(1))


def _mix_in_kernel(x_ref, meta_ref, g_ref, win_ref, wpool_ref, bpool_ref, pscale_ref,
                   qg_ref, kg_ref, *refs):
    later_w = refs[:N_LATER_WEIGHTS]
    (ya_ref, q_ref, k_ref, v_ref, yam_ref, qm_ref, km_ref, vm_ref,
     ktail_ref, vtail_ref) = refs[N_LATER_WEIGHTS:N_LATER_WEIGHTS + N_MIX_IN_OUTS]
    later_w_bf16 = refs[N_LATER_WEIGHTS + N_MIX_IN_OUTS:2 * N_LATER_WEIGHTS + N_MIX_IN_OUTS]
    uext_ref, win_bf16_ref, kcarry_ref, vcarry_ref = refs[-4:]
    pool_w = bpool_ref.shape[-1]
    qk_w = q_ref.shape[-1]

    for src, dst in zip(later_w, later_w_bf16):
        dst[...] = src[...].astype(dst.dtype)

    @pl.when((pl.program_id(0) == 0) & (pl.program_id(1) == 0))
    def _():
        win_bf16_ref[...] = win_ref[...].astype(BF16)

    def put_shifted(out, cols, val, carry_ref, tail_ref):
        r = val.shape[0]
        out[0, 0:N_META, cols] = carry_ref[:, cols]
        out[0, N_META:, cols] = val[:r - N_META]
        carry_ref[:, cols] = val[r - N_META:]
        tail_ref[0, :, cols] = val[r - N_META:]

    def heads(rows, is_meta, ya_out, q_out, k_out, v_out):
        r = rows.shape[0]
        inv_rms = lax.rsqrt(jnp.mean(rows * rows, axis=-1, keepdims=True) + EPS)
        proj = inv_rms * jnp.dot((rows * g_ref[...]).astype(BF16), win_bf16_ref[...],
                                 preferred_element_type=F32)
        u = proj[:, :pool_w]
        uext_ref[HALO:HALO + r, :] = u
        for gi, w in enumerate(POOL_WINDOWS):
            cols = slice(gi * LANES, (gi + 1) * LANES)
            win_sum = u[:, cols]
            for j in range(1, w):
                win_sum = win_sum + uext_ref[HALO - j:HALO - j + r, cols]
            if is_meta:
                pos = lax.broadcasted_iota(jnp.int32, (r, 1), 0)
                mean = win_sum / jnp.minimum(pos + 1, w).astype(F32)
            else:
                mean = win_sum * (1.0 / w)
            pooled = mean - u[:, cols]
            y = jnp.dot(pooled.astype(BF16), wpool_ref[gi].astype(BF16), preferred_element_type=F32)
            y = (y + bpool_ref[:, cols]) * pscale_ref[:, cols]
            ya_out[0, :, cols] = y.astype(ya_out.dtype)
        uext_ref[0:HALO, :] = uext_ref[r:r + HALO, :]

        lane = lax.broadcasted_iota(jnp.int32, (r, LANES), 1)
        lo = lane < QK_DIM
        for src, gain_ref, out in ((proj[:, pool_w:pool_w + qk_w], qg_ref, q_out),
                                   (proj[:, pool_w + qk_w:pool_w + 2 * qk_w], kg_ref, k_out)):
            for h in range(N_HEADS):
                cols = slice(h * LANES, (h + 1) * LANES)
                slab = src[:, cols]
                sq = slab * slab
                s_lo = jnp.sum(jnp.where(lo, sq, 0.0), axis=-1, keepdims=True)
                s_hi = jnp.sum(jnp.where(lo, 0.0, sq), axis=-1, keepdims=True)
                ms = jnp.where(lo, s_lo, s_hi) * (1.0 / QK_DIM)
                normed = (slab * lax.rsqrt(ms + EPS) * gain_ref[...]).astype(out.dtype)
                if out is k_out and not is_meta:
                    put_shifted(out, cols, normed, kcarry_ref, ktail_ref)
                else:
                    out[0, :, cols] = normed
        for h in range(N_HEADS):
            vcols = slice(pool_w + 2 * qk_w + h * HEAD_DIM, pool_w + 2 * qk_w + (h + 1) * HEAD_DIM)
            out_cols = slice(2 * h * HEAD_DIM, (2 * h + 1) * HEAD_DIM)
            one_cols = slice((2 * h + 1) * HEAD_DIM, (2 * h + 2) * HEAD_DIM)
            vals = proj[:, vcols].astype(v_out.dtype)
            if is_meta:
                v_out[0, :, out_cols] = vals
            else:
                put_shifted(v_out, out_cols, vals, vcarry_ref, vtail_ref)
                vtail_ref[0, :, one_cols] = jnp.ones((N_META, HEAD_DIM), v_out.dtype)
            v_out[0, :, one_cols] = jnp.ones((r, HEAD_DIM), v_out.dtype)

    @pl.when(pl.program_id(1) == 0)
    def _():
        uext_ref[0:HALO, :] = jnp.zeros((HALO, pool_w), F32)
        heads(meta_ref[...], True, yam_ref, qm_ref, km_ref, vm_ref)
        kcarry_ref[...] = km_ref[0]
        vcarry_ref[...] = vm_ref[0]

    heads(x_ref[0], False, ya_ref, q_ref, k_ref, v_ref)


def _mix_in(x, meta, g, w_in, w_pool, b_pool, pool_scale, qg, kg, later_weights):
    b, seq, d = x.shape
    pool_w = b_pool.shape[-1]
    qk_w = N_HEADS * HEAD_DIM
    tm = IN_TILE
    nt = seq // tm
    assert seq % tm == 0 and len(later_weights) == N_LATER_WEIGHTS
    tile = lambda w: pl.BlockSpec((1, tm, w), lambda bi, i: (bi, i, 0))
    mtile = lambda w: pl.BlockSpec((1, N_META, w), lambda bi, i: (bi, 0, 0))
    big = lambda w: jax.ShapeDtypeStruct((b, seq, w), BF16)
    small = lambda w: jax.ShapeDtypeStruct((b, N_META, w), BF16)
    def wblock(w):
        rows, steps = w.shape[1], b * nt
        hold = next(h for h in (1, 2, 4, 8) if (rows * h) % (steps * BF16_SUBLANES) == 0)
        return pl.BlockSpec((None, rows * hold // steps, w.shape[2]), lambda bi, i: (0, (bi * nt + i) // hold, 0))
    outs = pl.pallas_call(
        _mix_in_kernel,
        grid=(b, nt),
        in_specs=[tile(d), _const_spec(meta.shape), _const_spec(g.shape), _layer_const_spec(w_in.shape),
                  _layer_const_spec(w_pool.shape), _const_spec(b_pool.shape), _const_spec(pool_scale.shape),
                  _const_spec(qg.shape), _const_spec(kg.shape)] + [wblock(w) for w in later_weights],
        out_specs=[tile(pool_w), tile(qk_w), tile(qk_w), tile(2 * qk_w),
                   mtile(pool_w), mtile(qk_w), mtile(qk_w), mtile(2 * qk_w), mtile(qk_w), mtile(2 * qk_w)]
                  + [wblock(w) for w in later_weights],
        out_shape=[big(pool_w), big(qk_w), big(qk_w), big(2 * qk_w),
                   small(pool_w), small(qk_w), small(qk_w), small(2 * qk_w), small(qk_w), small(2 * qk_w)]
                  + [jax.ShapeDtypeStruct(w.shape, BF16) for w in later_weights],
        scratch_shapes=[pltpu.VMEM((HALO + tm, pool_w), F32), pltpu.VMEM(w_in.shape[1:], BF16),
                        pltpu.VMEM((N_META, qk_w), BF16), pltpu.VMEM((N_META, 2 * qk_w), BF16)],
        compiler_params=pltpu.CompilerParams(
            dimension_semantics=("arbitrary", "arbitrary"), vmem_limit_bytes=VMEM_LIMIT_BYTES),
        name="mix_in",
    )(x, meta, g, w_in, w_pool, b_pool, pool_scale, qg, kg, *later_weights)
    return outs[:N_MIX_IN_OUTS], outs[N_MIX_IN_OUTS:]


def _stack_subheads(q, sub):
    lane = lax.broadcasted_iota(jnp.int32, (sub, q.shape[1]), 1)
    zero = jnp.zeros((sub, q.shape[1]), q.dtype)
    parts = []
    for r0 in range(0, q.shape[0], sub):
        part = q[r0:r0 + sub]
        parts += [jnp.where(lane < QK_DIM, part, zero), jnp.where(lane < QK_DIM, zero, part)]
    return jnp.concatenate(parts, axis=0)


def _scores(qs, k):
    return lax.dot_general(qs, k, (((1,), (1,)), ((), ())), preferred_element_type=F32)


def _causal_mask(s, masked_rows=None, ahead=0):
    sub = s.shape[1]
    row = lax.broadcasted_iota(jnp.int32, s.shape, 0)
    col = lax.broadcasted_iota(jnp.int32, s.shape, 1)
    visible = col <= (row & (sub - 1)) + ahead
    if masked_rows is not None and masked_rows < s.shape[0]:
        visible = visible | (row >= masked_rows)
    return jnp.where(visible, s, -jnp.inf)


def _online_softmax_step(s, v_ext, m_ref, acc_ref):
    width = s.shape[1]
    m = m_ref[...]
    m_new = jnp.maximum(m, jnp.max(s, axis=-1, keepdims=True))
    alpha = jnp.exp2(m - m_new)
    m_wide = jnp.concatenate([m_new] * (width // LANES), axis=1) if width > LANES else m_new[:, :width]
    p = jnp.exp2(s - m_wide)
    acc_ref[...] = (jnp.concatenate([alpha, alpha], axis=1) * acc_ref[...]
                    + jnp.dot(p.astype(v_ext.dtype), v_ext, preferred_element_type=F32))
    m_ref[...] = m_new


def _init_state(m_ref, acc_ref):
    m_ref[...] = jnp.full(m_ref.shape, -jnp.inf, F32)
    acc_ref[...] = jnp.zeros(acc_ref.shape, F32)


def _first_softmax_step(s, v_ext, m_ref, acc_ref):
    m = jnp.max(s, axis=-1, keepdims=True)
    p = jnp.exp2(s - m)
    m_ref[...] = jnp.broadcast_to(m, m_ref.shape)
    acc_ref[...] = jnp.dot(p.astype(v_ext.dtype), v_ext, preferred_element_type=F32)


def _diff_attn_kernel(lam_init, q_ref, qnext_ref, k_ref, v_ref, qm_ref, km_ref, vm_ref, ktail_ref, vtail_ref,
                      lq1_ref, lk1_ref, lq2_ref, lk2_ref, sg_ref, yb_ref, ybm_ref,
                      qs_ref, qn_ref, m_ref, acc_ref, sa_ref, sb_ref, sc_ref):
    g = pl.program_id(2)
    n = ATTN_SUBTILES
    t = q_ref.shape[1] // n
    n_tiles = k_ref.shape[1] // t
    lam = (jnp.exp(jnp.sum(lq1_ref[...] * lk1_ref[...], axis=-1, keepdims=True))
           - jnp.exp(jnp.sum(lq2_ref[...] * lk2_ref[...], axis=-1, keepdims=True)) + lam_init)

    def finish(row0, r, out_ref, out_rows):
        o = (acc_ref[row0:row0 + r, :HEAD_DIM] / acc_ref[row0:row0 + r, HEAD_DIM:]
             - lam * (acc_ref[row0 + t:row0 + t + r, :HEAD_DIM] / acc_ref[row0 + t:row0 + t + r, HEAD_DIM:]))
        out_ref[0, out_rows, :] = (_rmsnorm(o, sg_ref[...]) * (1.0 - lam_init)).astype(out_ref.dtype)

    @pl.when(g == 0)
    def _():
        qm = _stack_subheads(qm_ref[0], N_META)
        for half in range(2):
            rows = slice(half * t, half * t + N_META)
            _init_state(m_ref.at[rows], acc_ref.at[rows])
        sm = _causal_mask(_scores(qm, km_ref[0]))
        for half in range(2):
            rows = slice(half * t, half * t + N_META)
            _online_softmax_step(sm[half * N_META:(half + 1) * N_META], vm_ref[0],
                                 m_ref.at[rows], acc_ref.at[rows])
        finish(0, N_META, ybm_ref, slice(0, N_META))

    def kv_tile(ref, j):
        return ref[0, pl.ds(pl.multiple_of(j * t, t), t), :]

    def step(cur_ref, nxt_ref, j):
        nxt_ref[...] = _scores(qs_ref[...], kv_tile(k_ref, j + 1))
        _online_softmax_step(cur_ref[...], kv_tile(v_ref, j), m_ref, acc_ref)

    def corner(c):
        nxt = n * g + c + 1
        in_range = nxt < n_tiles
        start = pl.multiple_of(jnp.minimum(nxt, n_tiles - 1) * t, t)
        k_first = jnp.where(in_range, k_ref[0, pl.ds(start, N_META), :], ktail_ref[0])
        v_first = jnp.where(in_range, v_ref[0, pl.ds(start, N_META), :], vtail_ref[0])
        for half in range(2):
            r0 = 2 * t * c + half * t + t - N_META
            rows = slice(r0, r0 + N_META)
            s = _scores(qs_ref[rows, :], k_first)
            row = lax.broadcasted_iota(jnp.int32, s.shape, 0)
            col = lax.broadcasted_iota(jnp.int32, s.shape, 1)
            _online_softmax_step(jnp.where(col <= row, s, -jnp.inf), v_first, m_ref.at[rows], acc_ref.at[rows])

    def diagonal_steps(cur_ref, nxt_ref, first):
        for c in range(n):
            rows = slice(2 * t * c, 2 * t * n)
            n_rows = 2 * t * (n - c)
            if c + 1 < n:
                nxt_ref[0:n_rows - 2 * t, :] = _scores(qs_ref[2 * t * (c + 1):2 * t * n, :],
                                                       kv_tile(k_ref, n * g + c + 1))
            else:
                qn_ref[...] = _stack_subheads(qnext_ref[0], t)
                sc_ref[...] = _scores(qn_ref[...], kv_tile(k_ref, 0))
            update = _first_softmax_step if (first and c == 0) else _online_softmax_step
            update(_causal_mask(cur_ref[0:n_rows, :], masked_rows=2 * t, ahead=N_META), kv_tile(v_ref, n * g + c),
                   m_ref.at[rows], acc_ref.at[rows])
            corner(c)
            finish(2 * t * c, t, yb_ref, slice(t * c, t * (c + 1)))
            cur_ref, nxt_ref = nxt_ref, cur_ref

    @pl.when(g == 0)
    def _():
        qs_ref[...] = _stack_subheads(q_ref[0], t)
        sa_ref[...] = _scores(qs_ref[...], kv_tile(k_ref, 0))
        diagonal_steps(sa_ref, sb_ref, first=True)

    @pl.when(g > 0)
    def _():
        qs_ref[...] = qn_ref[...]
        sa_ref[...] = _scores(qs_ref[...], kv_tile(k_ref, 1))
        _first_softmax_step(sc_ref[...], kv_tile(v_ref, 0), m_ref, acc_ref)

        n_quads = (n * g - 1) // 4

        @pl.loop(0, n_quads)
        def _(jj):
            step(sa_ref, sb_ref, 4 * jj + 1)
            step(sb_ref, sa_ref, 4 * jj + 2)
            step(sa_ref, sb_ref, 4 * jj + 3)
            step(sb_ref, sa_ref, 4 * jj + 4)

        @pl.when((n * g - 1) % 4 >= 2)
        def _():
            step(sa_ref, sb_ref, 4 * n_quads + 1)
            step(sb_ref, sa_ref, 4 * n_quads + 2)

        step(sa_ref, sb_ref, n * g - 1)
        diagonal_steps(sb_ref, sa_ref, first=False)


def _diff_attn(q, k, v, qm, km, vm, ktail, vtail, lq1, lk1, lq2, lk2, subln_g, lam_init):
    b, seq, width = q.shape
    t, n = ATTN_TILE, ATTN_SUBTILES
    assert seq % (n * t) == 0 and n % 2 == 0 and width == N_HEADS * HEAD_DIM
    ng = seq // (n * t)
    qgroup = pl.BlockSpec((1, n * t, HEAD_DIM), lambda bi, h, g: (bi, g, h))
    qnext = pl.BlockSpec((1, n * t, HEAD_DIM), lambda bi, h, g: (bi, jnp.minimum(g + 1, ng - 1), h))
    full = lambda w: pl.BlockSpec((1, seq, w), lambda bi, h, g: (bi, 0, h))
    mtile = lambda w: pl.BlockSpec((1, N_META, w), lambda bi, h, g: (bi, 0, h))
    rows = 2 * n * t
    return pl.pallas_call(
        functools.partial(_diff_attn_kernel, lam_init),
        grid=(b, N_HEADS, ng),
        in_specs=[qgroup, qnext, full(HEAD_DIM), full(2 * HEAD_DIM), mtile(HEAD_DIM), mtile(HEAD_DIM), mtile(2 * HEAD_DIM),
                  mtile(HEAD_DIM), mtile(2 * HEAD_DIM), _const_spec(lq1.shape), _const_spec(lk1.shape), _const_spec(lq2.shape),
                  _const_spec(lk2.shape), _const_spec(subln_g.shape)],
        out_specs=[qgroup, mtile(HEAD_DIM)],
        out_shape=[jax.ShapeDtypeStruct((b, seq, width), BF16),
                   jax.ShapeDtypeStruct((b, N_META, width), BF16)],
        scratch_shapes=[pltpu.VMEM((rows, HEAD_DIM), BF16), pltpu.VMEM((rows, HEAD_DIM), BF16),
                        pltpu.VMEM((rows, LANES), F32), pltpu.VMEM((rows, 2 * HEAD_DIM), F32),
                        pltpu.VMEM((rows, t), F32), pltpu.VMEM((rows, t), F32), pltpu.VMEM((rows, t), F32)],
        compiler_params=pltpu.CompilerParams(
            dimension_semantics=("arbitrary", "arbitrary", "arbitrary"),
            vmem_limit_bytes=VMEM_LIMIT_BYTES),
        name="diff_attn",
    )(q, q, k, v, qm, km, vm, ktail, vtail, lq1, lk1, lq2, lk2, subln_g)


def _mix_out_kernel(x0_ref, ya0_ref, yb0_ref, xn_ref, yan_ref, ybn_ref, meta_ref, yam_ref, ybm_ref, wo_ref, g_ref,
                    wup_ref, cw_ref, cb_ref, wdown_ref, out_ref, nperm_ref, next_ref, h1_ref, act_ref, operm_ref):
    i = pl.program_id(1)
    tm = xn_ref.shape[1]
    d = xn_ref.shape[2]
    pool_w = yan_ref.shape[-1]
    d_ff = wdown_ref.shape[0]
    n_slabs = d // LANES
    nv = tm // SUBLANES

    def mix_residual(rows, ya, yb):
        h1 = (rows + jnp.dot(ya, wo_ref[:pool_w, :], preferred_element_type=F32)
              + jnp.dot(yb, wo_ref[pool_w:, :], preferred_element_type=F32))
        return h1, _rmsnorm(h1, g_ref[...])

    def prepare(x_rows, ya, yb, slot):
        h1, n2 = mix_residual(x_rows, ya, yb)
        h1_ref[slot] = h1
        for s in range(SUBLANES):
            for v0 in range(0, nv, SUBLANES):
                t0 = s * nv + v0
                for c in range(n_slabs):
                    nperm_ref[c, pl.ds(HALO + v0 * SUBLANES + s, SUBLANES, stride=SUBLANES), :] = (
                        n2[t0:t0 + SUBLANES, c * LANES:(c + 1) * LANES])
        next_ref[...] = jnp.concatenate([nperm_ref[c] for c in range(n_slabs)], axis=1).astype(BF16)

    @pl.when(i == 0)
    def _():
        _, nm = mix_residual(meta_ref[...], yam_ref[0], ybm_ref[0])
        halo = jnp.concatenate([nm[N_META - 9:N_META - 1], nm[N_META - 8:]], axis=0)
        for c in range(n_slabs):
            nperm_ref[c, 0:HALO, :] = halo[:, c * LANES:(c + 1) * LANES]
        prepare(x0_ref[0], ya0_ref[0], yb0_ref[0], 0)

    def conv(u, cols):
        cur = u[HALO:]
        wrap1 = jnp.concatenate([u[HALO - 1:HALO], u[HALO + tm - 8:HALO + tm - 1]], axis=0)
        wrap2 = jnp.concatenate([u[HALO - 9:HALO - 8], u[HALO + tm - 16:HALO + tm - 9]], axis=0)
        back1 = jnp.concatenate([wrap1, u[HALO:HALO + tm - 8]], axis=0)
        back2 = jnp.concatenate([wrap2, wrap1, u[HALO:HALO + tm - 16]], axis=0)
        return (cb_ref[:, cols] + back2 * cw_ref[0:1, cols] + back1 * cw_ref[1:2, cols]
                + cur * cw_ref[2:3, cols])

    for c in range(d_ff // FF_CHUNK):
        gcols = slice(c * FF_CHUNK, (c + 1) * FF_CHUNK)
        vcols = slice(d_ff + c * FF_CHUNK, d_ff + (c + 1) * FF_CHUNK)
        ug = jnp.dot(next_ref[...], wup_ref[:, gcols], preferred_element_type=F32)
        uv = jnp.dot(next_ref[...], wup_ref[:, vcols], preferred_element_type=F32)
        gate = conv(ug, gcols)
        act_ref[:, gcols] = (gate * jax.nn.sigmoid(gate) * conv(uv, vcols)).astype(BF16)
    for c in range(n_slabs):
        nperm_ref[c, 0:HALO, :] = nperm_ref[c, tm:tm + HALO, :]
    prepare(xn_ref[0], yan_ref[0], ybn_ref[0], (i + 1) % 2)

    ffn = jnp.dot(act_ref[...], wdown_ref[...], preferred_element_type=F32)
    for c in range(n_slabs):
        operm_ref[c] = ffn[:, c * LANES:(c + 1) * LANES]
    for s in range(SUBLANES):
        for v0 in range(0, nv, SUBLANES):
            t0 = s * nv + v0
            for c in range(n_slabs):
                lanes = slice(c * LANES, (c + 1) * LANES)
                out_ref[0, t0:t0 + SUBLANES, lanes] = (
                    h1_ref[i % 2, t0:t0 + SUBLANES, lanes]
                    + operm_ref[c, pl.ds(v0 * SUBLANES + s, SUBLANES, stride=SUBLANES), :])


def _mix_out(x, ya, yb, meta, yam, ybm, w_out, g, w_up, conv_w, conv_b, w_down):
    b, seq, d = x.shape
    pool_w = ya.shape[-1]
    tm = SEQ_TILE
    d_ff = w_down.shape[1]
    assert seq % tm == 0 and d_ff % FF_CHUNK == 0
    n_tiles = seq // tm
    tile = lambda w: pl.BlockSpec((1, tm, w), lambda bi, i: (bi, i, 0))
    first = lambda w: pl.BlockSpec((1, tm, w), lambda bi, i: (bi, 0, 0))
    following = lambda w: pl.BlockSpec((1, tm, w), lambda bi, i: (bi, jnp.minimum(i + 1, n_tiles - 1), 0))
    mtile = lambda w: pl.BlockSpec((1, N_META, w), lambda bi, i: (bi, 0, 0))
    return pl.pallas_call(
        _mix_out_kernel,
        grid=(b, n_tiles),
        in_specs=[first(d), first(pool_w), first(yb.shape[-1]), following(d), following(pool_w), following(yb.shape[-1]),
                  _const_spec(meta.shape),
                  mtile(pool_w), mtile(ybm.shape[-1]), _layer_const_spec(w_out.shape), _const_spec(g.shape),
                  _layer_const_spec(w_up.shape), _const_spec(conv_w.shape), _const_spec(conv_b.shape),
                  _layer_const_spec(w_down.shape)],
        out_specs=tile(d),
        out_shape=jax.ShapeDtypeStruct((b, seq, d), x.dtype),
        scratch_shapes=[pltpu.VMEM((d // LANES, HALO + tm, LANES), F32), pltpu.VMEM((HALO + tm, d), BF16),
                        pltpu.VMEM((2, tm, d), F32),
                        pltpu.VMEM((tm, d_ff), BF16), pltpu.VMEM((d // LANES, tm, LANES), F32)],
        compiler_params=pltpu.CompilerParams(
            dimension_semantics=("arbitrary", "arbitrary"), vmem_limit_bytes=VMEM_LIMIT_BYTES),
        name="mix_out",
    )(x, ya, yb, x, ya, yb, meta, yam, ybm, w_out, g, w_up, conv_w, conv_b, w_down)


def kernel(x, meta_tokens, norm_mix_g, w_in, w_pool, b_pool, pool_scale, q_norm_g, k_norm_g,
           lambda_q1, lambda_k1, lambda_q2, lambda_k2, subln_g, w_out, norm_ffn_g,
           w_up, conv_w, conv_b, w_down):
    depth = w_in.shape[0]
    assert depth == 1 and meta_tokens.shape[0] == N_META
    row = lambda a: a.reshape(1, -1).astype(F32)
    h = x
    meta = meta_tokens.astype(x.dtype)
    for i in range(depth):
        lam_init = 0.8 - 0.6 * math.exp(-0.3 * i)
        qg = jnp.tile(row(q_norm_g[i]), (1, 2)) * (QK_DIM ** -0.5 * math.log2(math.e))
        kg = jnp.tile(row(k_norm_g[i]), (1, 2))
        layer = slice(i, i + 1)
        (ya, q, k, v, yam, qm, km, vm, kt, vt), (w_out_n, w_up_n, w_down_n) = _mix_in(
            h, meta, row(norm_mix_g[i]), w_in[layer], w_pool[layer],
            row(b_pool[i]), row(pool_scale[i]), qg, kg, (w_out[layer], w_up[layer], w_down[layer]))
        yb, ybm = _diff_attn(q, k, v, qm, km, vm, kt, vt, row(lambda_q1[i]), row(lambda_k1[i]),
                             row(lambda_q2[i]), row(lambda_k2[i]), row(subln_g[i]), lam_init)
        h = _mix_out(h, ya, yb, meta, yam, ybm, w_out_n, row(norm_ffn_g[i]),
                     w_up_n, conv_w[i].astype(F32), row(conv_b[i]), w_down_n)
    return h
```
